```python
import math
import jax, jax.numpy as jnp
from jax import lax
import numpy as np

D_MODEL = 1024
BATCH = 2
SEQ = 8192
DEPTH = 2

GRID_W = 64
CTX_LEN = 256
EPS = 1e-6

A_WIDTH = 512
A_GROUPS = 4
A_GROUP_DIM = A_WIDTH // A_GROUPS
A_CHUNK = 128

B_HEADS = 8
B_Q_LORA = 384
B_KV_LORA = 256
B_NOPE = 64
B_ROPE = 32
B_VDIM = 64
B_BLOCK = 128
ROPE_BASE = 10000.0
ROPE_FREQS = B_ROPE // 4
MLA_SCALE = (B_NOPE + B_ROPE) ** -0.5

C_HEADS = 8
C_HEAD = 64
C_WIDTH = C_HEADS * C_HEAD
C_DECAY_LORA = 64
C_AAA_LORA = 64
C_GATE_LORA = 128
C_N_DIR = 2
C_GN_EPS = 64e-5
DECAY_SCALE = 0.6065306597126334

B_COLS = B_Q_LORA + B_KV_LORA + B_ROPE
C_COLS = 3 * C_WIDTH + C_N_DIR * C_DECAY_LORA + C_N_DIR * C_AAA_LORA + C_GATE_LORA
A_COLS = 2 * A_WIDTH
G_COLS = 3 * D_MODEL
IN_COLS = B_COLS + C_COLS + A_COLS + G_COLS
Q_SL = slice(0, B_Q_LORA)
KV_SL = slice(B_Q_LORA, B_Q_LORA + B_KV_LORA)
KR_SL = slice(B_Q_LORA + B_KV_LORA, B_COLS)
C_SL = slice(B_COLS, B_COLS + C_COLS)
A_SL = slice(B_COLS + C_COLS, B_COLS + C_COLS + A_COLS)
G_SL = slice(B_COLS + C_COLS + A_COLS, IN_COLS)

N_EXPERTS = 16
N_GROUPS = 4
EXPERTS_PER_GROUP = N_EXPERTS // N_GROUPS
TOP_GROUPS = 1
TOP_K = 2
D_EXPERT = 512

kernel_name = 'hybrid_mla_rwkv7_gmlp_moe_dit'


def rmsnorm(x, g):
    xf = x.astype(jnp.float32)
    y = xf * lax.rsqrt(jnp.mean(xf * xf, axis=-1, keepdims=True) + EPS)
    return (y * g.astype(jnp.float32)).astype(x.dtype)


def axial_rope_tables(rows, dtype):
    row = jnp.repeat(jnp.arange(rows, dtype=jnp.float32), GRID_W)
    col = jnp.tile(jnp.arange(GRID_W, dtype=jnp.float32), rows)
    inv = jnp.power(ROPE_BASE, -jnp.arange(ROPE_FREQS, dtype=jnp.float32) / ROPE_FREQS)
    ang = jnp.stack([row[:, None] * inv, col[:, None] * inv], axis=1)
    return jnp.cos(ang).astype(dtype), jnp.sin(ang).astype(dtype)


def apply_axial_rope(x, cos, sin):
    xs = x.reshape(x.shape[:-1] + (2, 2, ROPE_FREQS))
    x1, x2 = xs[..., 0, :], xs[..., 1, :]
    out = jnp.stack([x1 * cos - x2 * sin, x2 * cos + x1 * sin], axis=-2)
    return out.reshape(x.shape)


def centred_shift(z):
    zp = jnp.pad(z, ((0, 0), (1, 1), (0, 0)))
    return 0.5 * (zp[:, :-2] + zp[:, 2:])


def chunk_gmlp(za, v_gain, w_s, b_s):
    zg = jax.nn.gelu(za)
    u, v = zg[..., :A_WIDTH], zg[..., A_WIDTH:]
    v = rmsnorm(v, v_gain)
    Bn, T, _ = v.shape
    vc = v.reshape(Bn, T // A_CHUNK, A_CHUNK, A_GROUPS, A_GROUP_DIM)
    mixed = jnp.einsum('gij,bnjgc->bnigc', w_s, vc) + b_s.T[None, None, :, :, None]
    return u * mixed.reshape(Bn, T, A_WIDTH)


def mla_queries(zq, q_norm_g, w_uq):
    Bn, T, _ = zq.shape
    q = (rmsnorm(zq, q_norm_g) @ w_uq).reshape(Bn, T, B_HEADS, B_NOPE + B_ROPE)
    return q[..., :B_NOPE], q[..., B_NOPE:]


def mla_keys_values(zkv, kv_norm_g, w_ukv):
    Bn, T, _ = zkv.shape
    kv = (rmsnorm(zkv, kv_norm_g) @ w_ukv).reshape(Bn, T, B_HEADS, B_NOPE + B_VDIM)
    return kv[..., :B_NOPE], kv[..., B_NOPE:]


def mla_attend(q_nope, q_rope, k_nope, k_rope, v):
    Bn, Tq = q_nope.shape[:2]
    nblk = Tq // B_BLOCK

    def to_blocks(t):
        return jnp.moveaxis(t.reshape((Bn, nblk, B_BLOCK) + t.shape[2:]), 1, 0)

    def one_block(qb):
        qn, qr = qb
        s = jnp.einsum('bqhd,bkhd->bhqk', qn, k_nope) + jnp.einsum('bqhr,bkr->bhqk', qr, k_rope)
        p = jax.nn.softmax(s.astype(jnp.float32) * MLA_SCALE, axis=-1).astype(v.dtype)
        return jnp.einsum('bhqk,bkhd->bqhd', p, v)

    o = lax.map(one_block, (to_blocks(q_nope), to_blocks(q_rope)))
    return jnp.moveaxis(o, 0, 1).reshape(Bn, Tq, B_HEADS * B_VDIM)


def rwkv_features(zc, mu, w0, w2, a0, a2, k_k, k_a):
    f32 = jnp.float32
    zc = zc + mu * (centred_shift(zc) - zc)
    Bn, T, _ = zc.shape
    heads = lambda t: t.reshape(Bn, T, C_HEADS, C_HEAD)
    dirs = lambda t: t.reshape(Bn, T, C_N_DIR, C_HEADS, C_HEAD)
    r = heads(zc[..., 0:C_WIDTH]).astype(f32)
    k = zc[..., C_WIDTH:2 * C_WIDTH]
    v = heads(zc[..., 2 * C_WIDTH:3 * C_WIDTH]).astype(f32)
    o = 3 * C_WIDTH
    zw = zc[..., o:o + C_N_DIR * C_DECAY_LORA].reshape(Bn, T, C_N_DIR, C_DECAY_LORA)
    o += C_N_DIR * C_DECAY_LORA
    za = zc[..., o:o + C_N_DIR * C_AAA_LORA].reshape(Bn, T, C_N_DIR, C_AAA_LORA)
    o += C_N_DIR * C_AAA_LORA
    zg = zc[..., o:]
    w_logit = w0 + jnp.einsum('btdl,dlc->btdc', jnp.tanh(zw), w2)
    w = jnp.exp(-DECAY_SCALE * jax.nn.sigmoid(w_logit.astype(f32)))
    a = jax.nn.sigmoid((a0 + jnp.einsum('btdl,dlc->btdc', za, a2)).astype(f32))
    kk = heads(k * k_k).astype(f32)
    kk = kk * lax.rsqrt(jnp.maximum(jnp.sum(kk * kk, axis=-1, keepdims=True), 1e-12))
    kd = k.astype(f32)[:, :, None, :] * (1.0 + (a - 1.0) * k_a.astype(f32))
    return r, v, kk, dirs(w), dirs(a), dirs(kd), zg


def rwkv_scan(r, w, k, v, kk, a, s0, reverse, emit):
    def step(S, inp):
        if emit:
            w_t, k_t, v_t, kk_t, a_t, r_t = inp
        else:
            w_t, k_t, v_t, kk_t, a_t = inp
        sa = jnp.einsum('bhvk,bhk->bhv', S, -kk_t)
        S = S * w_t[:, :, None, :] + sa[..., None] * (kk_t * a_t)[:, :, None, :] + v_t[..., None] * k_t[:, :, None, :]
        return S, (jnp.einsum('bhvk,bhk->bhv', S, r_t) if emit else None)

    seq = (w, k, v, kk, a, r) if emit else (w, k, v, kk, a)
    xs = tuple(jnp.moveaxis(t, 1, 0) for t in seq)
    S, out = lax.scan(step, s0, xs, reverse=reverse)
    return S, (jnp.moveaxis(out, 0, 1) if emit else None)


def rwkv_direction(feats, d, s0, emit):
    r, v, kk, w, a, kd, _ = feats
    return rwkv_scan(r, w[:, :, d], kd[:, :, d], v, kk, a[:, :, d], s0, d == 1, emit)


def rwkv_readout(o, feats, g2, r_k, gn_g, gn_b):
    r, v, kk, w, a, kd, zg = feats
    Bn, T = o.shape[:2]
    m = jnp.mean(o, axis=-1, keepdims=True)
    var = jnp.mean(jnp.square(o - m), axis=-1, keepdims=True)
    y = ((o - m) * lax.rsqrt(var + C_GN_EPS)).reshape(Bn, T, C_WIDTH) * gn_g + gn_b
    bonus = jnp.sum(r[:, :, None] * kd * r_k, axis=-1, keepdims=True) * v[:, :, None]
    y = y + jnp.sum(bonus, axis=2).reshape(Bn, T, C_WIDTH)
    g = jax.nn.sigmoid(zg) @ g2
    return y.astype(g.dtype) * g


def merge_branches(zg, y_a, y_b, y_c, w_up_a, w_up_b, w_up_c, w_o):
    g = jax.nn.sigmoid(zg)
    g_a, g_b, g_c = g[..., :D_MODEL], g[..., D_MODEL:2 * D_MODEL], g[..., 2 * D_MODEL:]
    return (g_a * (y_a @ w_up_a) + g_b * (y_b @ w_up_b) + g_c * (y_c @ w_up_c)) @ w_o


def moe(h, router_w, router_b, w1, w3, w2):
    Bn, T, D = h.shape
    t = h.reshape(Bn * T, D)
    scores = jax.nn.sigmoid((t @ router_w).astype(jnp.float32))
    sel = scores + router_b.astype(jnp.float32)
    grp_score = jnp.sum(lax.top_k(sel.reshape(-1, N_GROUPS, EXPERTS_PER_GROUP), 2)[0], axis=-1)
    _, gidx = lax.top_k(grp_score, TOP_GROUPS)
    gmask = jnp.sum(jax.nn.one_hot(gidx, N_GROUPS, dtype=jnp.float32), axis=1)
    emask = jnp.repeat(gmask, EXPERTS_PER_GROUP, axis=1)
    _, eidx = lax.top_k(jnp.where(emask > 0, sel, -jnp.inf), TOP_K)
    wsel = jnp.take_along_axis(scores, eidx, axis=-1)
    wsel = wsel / jnp.sum(wsel, axis=-1, keepdims=True)
    comb = jnp.sum(jax.nn.one_hot(eidx, N_EXPERTS, dtype=jnp.float32) * wsel[..., None], axis=1).astype(t.dtype)
    out = jnp.zeros_like(t)
    for e in range(N_EXPERTS):
        he = jax.nn.silu(t @ w1[e]) * (t @ w3[e])
        out = out + comb[:, e:e + 1] * (he @ w2[e])
    return out.reshape(Bn, T, D)


def setup_inputs(seed: int = 0) -> dict:
    key = jax.random.key(seed)
    ks = jax.random.split(key, 48)
    counter = [0]

    def nxt():
        k = ks[counter[0]]
        counter[0] += 1
        return k

    def nrm(shape, scale):
        return jax.random.normal(nxt(), shape, jnp.float32) * scale

    def gain(shape):
        return 1.0 + nrm(shape, 0.02)

    L, D = DEPTH, D_MODEL
    return {
        'x': nrm((BATCH, SEQ, D), 1.0),
        'c': nrm((BATCH, D), 1.0),
        'ctx': nrm((BATCH, CTX_LEN, D), 1.0),
        'c_ctx': nrm((D,), 1.0),
        'mod_w': nrm((L, D, 6 * D), 0.5 * D ** -0.5),
        'mod_b': nrm((L, 6 * D), 0.02),
        'norm1_g': gain((L, D)),
        'norm2_g': gain((L, D)),
        'w_in': nrm((L, D, IN_COLS), D ** -0.5),
        'a_v_gain': gain((L, A_WIDTH)),
        'a_ws': nrm((L, A_GROUPS, A_CHUNK, A_CHUNK), 0.5 * A_CHUNK ** -0.5),
        'a_bs': gain((L, A_GROUPS, A_CHUNK)),
        'b_q_norm': gain((L, B_Q_LORA)),
        'b_w_uq': nrm((L, B_Q_LORA, B_HEADS * (B_NOPE + B_ROPE)), B_Q_LORA ** -0.5),
        'b_kv_norm': gain((L, B_KV_LORA)),
        'b_w_ukv': nrm((L, B_KV_LORA, B_HEADS * (B_NOPE + B_VDIM)), B_KV_LORA ** -0.5),
        'c_mu': jax.random.uniform(nxt(), (L, C_COLS), jnp.float32),
        'c_w0': nrm((L, C_N_DIR, C_WIDTH), 0.5),
        'c_w2': nrm((L, C_N_DIR, C_DECAY_LORA, C_WIDTH), 0.5 * C_DECAY_LORA ** -0.5),
        'c_a0': nrm((L, C_N_DIR, C_WIDTH), 0.5),
        'c_a2': nrm((L, C_N_DIR, C_AAA_LORA, C_WIDTH), 0.5 * C_AAA_LORA ** -0.5),
        'c_g2': nrm((L, C_GATE_LORA, C_WIDTH), C_GATE_LORA ** -0.5),
        'c_k_k': 1.0 + nrm((L, C_WIDTH), 0.1),
        'c_k_a': 1.0 + nrm((L, C_WIDTH), 0.1),
        'c_r_k': nrm((L, C_HEADS, C_HEAD), 0.1),
        'c_gn_g': gain((L, C_WIDTH)),
        'c_gn_b': nrm((L, C_WIDTH), 0.02),
        'w_up_a': nrm((L, A_WIDTH, D), A_WIDTH ** -0.5),
        'w_up_b': nrm((L, B_HEADS * B_VDIM, D), (B_HEADS * B_VDIM) ** -0.5),
        'w_up_c': nrm((L, C_WIDTH, D), C_WIDTH ** -0.5),
        'w_o': nrm((L, D, D), D ** -0.5),
        'router_w': nrm((D, N_EXPERTS), D ** -0.5),
        'router_b': nrm((N_EXPERTS,), 0.01),
        'moe_w1': nrm((L, N_EXPERTS, D, D_EXPERT), D ** -0.5),
        'moe_w3': nrm((L, N_EXPERTS, D, D_EXPERT), D ** -0.5),
        'moe_w2': nrm((L, N_EXPERTS, D_EXPERT, D), D_EXPERT ** -0.5),
        'final_g': gain((D,)),
    }


def reference(x, c, ctx, c_ctx, mod_w, mod_b, norm1_g, norm2_g, w_in,
              a_v_gain, a_ws, a_bs, b_q_norm, b_w_uq, b_kv_norm, b_w_ukv,
              c_mu, c_w0, c_w2, c_a0, c_a2, c_g2, c_k_k, c_k_a, c_r_k, c_gn_g, c_gn_b,
              w_up_a, w_up_b, w_up_c, w_o, router_w, router_b, moe_w1, moe_w3, moe_w2, final_g):
    Bn, T, _ = x.shape
    rows = T // GRID_W
    cos, sin = axial_rope_tables(rows, x.dtype)
    c_act = jax.nn.silu(c)
    cc_act = jax.nn.silu(c_ctx)
    xc = ctx
    for l in range(DEPTH):
        last = l == DEPTH - 1
        sh1, sc1, ga1, sh2, sc2, ga2 = jnp.split((c_act @ mod_w[l] + mod_b[l])[:, None, :], 6, axis=-1)
        sh1c, sc1c, ga1c, sh2c, sc2c, ga2c = jnp.split((cc_act @ mod_w[l] + mod_b[l])[None, None, :], 6, axis=-1)

        h = rmsnorm(x, norm1_g[l]) * (1 + sc1) + sh1
        hc = rmsnorm(xc, norm1_g[l]) * (1 + sc1c) + sh1c
        z = h @ w_in[l]
        zc = hc @ (w_in[l][:, :B_COLS + C_COLS] if last else w_in[l])

        kn_c, v_c = mla_keys_values(zc[..., KV_SL], b_kv_norm[l], b_w_ukv[l])
        kr_c = zc[..., KR_SL]
        kn, v = mla_keys_values(z[..., KV_SL], b_kv_norm[l], b_w_ukv[l])
        kr = apply_axial_rope(z[..., KR_SL], cos, sin)
        qn, qr = mla_queries(z[..., Q_SL], b_q_norm[l], b_w_uq[l])
        qr = apply_axial_rope(qr, cos[:, None], sin[:, None])
        y_b = mla_attend(qn, qr, jnp.concatenate([kn_c, kn], axis=1),
                         jnp.concatenate([kr_c, kr], axis=1), jnp.concatenate([v_c, v], axis=1))

        rw = (c_mu[l], c_w0[l], c_w2[l], c_a0[l], c_a2[l], c_k_k[l], c_k_a[l])
        f_lat = rwkv_features(z[..., C_SL], *rw)
        f_ctx = rwkv_features(zc[..., C_SL], *rw)
        s0 = jnp.zeros((Bn, C_HEADS, C_HEAD, C_HEAD), jnp.float32)
        s_f, oc_f = rwkv_direction(f_ctx, 0, s0, not last)
        s_b, oc_b = rwkv_direction(f_ctx, 1, s0, not last)
        _, ol_f = rwkv_direction(f_lat, 0, s_f, True)
        _, ol_b = rwkv_direction(f_lat, 1, s_b, True)
        y_c = rwkv_readout(ol_f + ol_b, f_lat, c_g2[l], c_r_k[l], c_gn_g[l], c_gn_b[l])

        y_a = chunk_gmlp(z[..., A_SL], a_v_gain[l], a_ws[l], a_bs[l])

        x_new = x + ga1 * merge_branches(z[..., G_SL], y_a, y_b, y_c, w_up_a[l], w_up_b[l], w_up_c[l], w_o[l])
        h2 = rmsnorm(x_new, norm2_g[l]) * (1 + sc2) + sh2
        x_new = x_new + ga2 * moe(h2, router_w, router_b, moe_w1[l], moe_w3[l], moe_w2[l])

        if not last:
            qn_c, qr_c = mla_queries(zc[..., Q_SL], b_q_norm[l], b_w_uq[l])
            y_bc = mla_attend(qn_c, qr_c, kn_c, kr_c, v_c)
            y_cc = rwkv_readout(oc_f + oc_b, f_ctx, c_g2[l], c_r_k[l], c_gn_g[l], c_gn_b[l])
            y_ac = chunk_gmlp(zc[..., A_SL], a_v_gain[l], a_ws[l], a_bs[l])
            xc_new = xc + ga1c * merge_branches(zc[..., G_SL], y_ac, y_bc, y_cc, w_up_a[l], w_up_b[l], w_up_c[l], w_o[l])
            h2c = rmsnorm(xc_new, norm2_g[l]) * (1 + sc2c) + sh2c
            xc = xc_new + ga2c * moe(h2c, router_w, router_b, moe_w1[l], moe_w3[l], moe_w2[l])
        x = x_new
    return rmsnorm(x, final_g)
```

```python
import functools
import math

import jax
import jax.numpy as jnp
from jax import lax
from jax.experimental import pallas as pl
from jax.experimental.pallas import tpu as pltpu

F32 = jnp.float32
BF16 = jnp.bfloat16
HIGHEST = lax.Precision.HIGHEST

D_MODEL = 1024
GRID_W = 64
EPS = 1e-6

A_WIDTH = 512
A_GROUPS = 4
A_CHUNK = 128

B_HEADS = 8
B_Q_LORA = 384
B_KV_LORA = 256
B_NOPE = 64
B_ROPE = 32
B_VDIM = 64
ROPE_BASE = 10000.0
ROPE_FREQS = B_ROPE // 4
MLA_SCALE = (B_NOPE + B_ROPE) ** -0.5

C_HEADS = 8
C_HEAD = 64
C_WIDTH = C_HEADS * C_HEAD
C_DECAY_LORA = 64
C_AAA_LORA = 64
C_GATE_LORA = 128
C_GN_EPS = 64e-5
DECAY_SCALE = 0.6065306597126334

B_COLS = B_Q_LORA + B_KV_LORA + B_ROPE
C_COLS = 3 * C_WIDTH + 2 * C_DECAY_LORA + 2 * C_AAA_LORA + C_GATE_LORA
A_COLS = 2 * A_WIDTH
G_COLS = 3 * D_MODEL

N_EXPERTS = 16
N_GROUPS = 4
EXPERTS_PER_GROUP = 4
D_EXPERT = 512

LANE = 128
HP = C_HEADS * LANE
TM = 256
SCAN_C = 64
MOE_SUB = 3
VMEM_LIMIT = 52 * 1024 * 1024


def _dot(a, b, precision=None):
    return jnp.dot(a, b, preferred_element_type=F32, precision=precision)


def _dot_nt(a, b, precision=None):
    return lax.dot_general(a, b, (((1,), (1,)), ((), ())),
                           preferred_element_type=F32, precision=precision)


def _dot_tn(a, b, precision=None):
    return lax.dot_general(a, b, (((0,), (0,)), ((), ())),
                           preferred_element_type=F32, precision=precision)


def _rms(x, eps=EPS):
    return x * lax.rsqrt(jnp.mean(x * x, axis=-1, keepdims=True) + eps)


def _normmod(x, g, mod, first):
    return _rms(x) * g * (1.0 + mod[first + 1:first + 2]) + mod[first:first + 1]


def _sigmoid(x):
    return 1.0 / (1.0 + jnp.exp(-x))


def _params(n_grid):
    return pltpu.CompilerParams(
        dimension_semantics=("arbitrary",) * n_grid, vmem_limit_bytes=VMEM_LIMIT)


def _full(shape):
    n = len(shape)
    return pl.BlockSpec(shape, lambda *_: (0,) * n)


def _mod_kernel(c_ref, w_ref, b_ref, o_ref):
    c = c_ref[...]
    act = (c * _sigmoid(c)).astype(BF16)
    o_ref[0] = _dot(act, w_ref[0].astype(BF16)) + b_ref[0]


def _modulation(cvec, mod_w, mod_b):
    depth, d, n = mod_w.shape
    tn = 1536
    return pl.pallas_call(
        _mod_kernel,
        grid=(depth, n // tn),
        in_specs=[
            pl.BlockSpec((8, d), lambda l, j: (0, 0)),
            pl.BlockSpec((1, d, tn), lambda l, j: (l, 0, j)),
            pl.BlockSpec((1, 1, tn), lambda l, j: (l, 0, j)),
        ],
        out_specs=pl.BlockSpec((1, 8, tn), lambda l, j: (l, 0, j)),
        out_shape=jax.ShapeDtypeStruct((depth, 8, n), F32),
        compiler_params=_params(2),
        name="modulation",
    )(cvec, mod_w, mod_b.reshape(depth, 1, n))


def _mla_kernel(x_ref, g1_ref, mod_ref, win_ref, qg_ref, kvg_ref, wq_ref, wqs_ref,
                wk_ref, wv_ref, place_ref, cq_ref, sq_ref, ck_ref, sk_ref,
                q_out, k_out, v_out):
    h = _normmod(x_ref[...], g1_ref[...], mod_ref[0], 0).astype(BF16)
    z = _dot(h, win_ref[...])
    zq = z[:, :B_Q_LORA]
    zkv = z[:, B_Q_LORA:B_Q_LORA + B_KV_LORA]
    kr_a = z[:, B_Q_LORA + B_KV_LORA:B_Q_LORA + B_KV_LORA + LANE]
    kr_b = z[:, B_Q_LORA + B_KV_LORA + LANE:]
    qn = (_rms(zq) * qg_ref[...]).astype(BF16)
    kvn = (_rms(zkv) * kvg_ref[...]).astype(BF16)
    q1 = _dot(qn, wq_ref[...])
    q2 = _dot(qn, wqs_ref[...])
    kr = (kr_a * ck_ref[...] + kr_b * sk_ref[...]).astype(BF16)
    k = _dot(kvn, wk_ref[...]) + _dot(kr, place_ref[...])
    cq = cq_ref[...]
    sq = sq_ref[...]
    for hh in range(B_HEADS):
        sl = slice(hh * LANE, (hh + 1) * LANE)
        q_out[:, sl] = (q1[:, sl] * cq + q2[:, sl] * sq).astype(BF16)
    k_out[...] = k.astype(BF16)
    v_out[...] = _dot(kvn, wv_ref[...]).astype(BF16)


def _gelu_tanh(x):
    return 0.5 * x * (1.0 + jnp.tanh(math.sqrt(2.0 / math.pi) * (x + 0.044715 * (x * x * x))))


def _gmlp_kernel(x_ref, g1_ref, mod_ref, wa_ref, wg_ref, vg_ref, ws_ref, bsb_ref,
                 ya_out, gate_out):
    h = _normmod(x_ref[...], g1_ref[...], mod_ref[0], 0).astype(BF16)
    gate_out[...] = _sigmoid(_dot(h, wg_ref[...])).astype(BF16)
    zg = _gelu_tanh(_dot(h, wa_ref[...]))
    u = zg[:, :A_WIDTH]
    v = (_rms(zg[:, A_WIDTH:]) * vg_ref[...]).astype(BF16)
    for c in range(TM // A_CHUNK):
        rows = slice(c * A_CHUNK, (c + 1) * A_CHUNK)
        for g in range(A_GROUPS):
            cols = slice(g * LANE, (g + 1) * LANE)
            mixed = _dot(ws_ref[g], v[rows, cols]) + bsb_ref[:, cols]
            ya_out[rows, cols] = (u[rows, cols] * mixed).astype(BF16)


def _feat_kernel(x_ref, xp_ref, xn_ref, g1_ref, mod_ref, wc_ref, mu_ref,
                 w0_ref, w2_ref, a0_ref, a2_ref, kk_ref, ka_ref, rk_ref, g2_ref,
                 r_out, v_out, nkk_out, logw_out, beta_out, kd_out, bonus_out, g_out,
                 *, tiles_per_batch):
    i = pl.program_id(0)
    t = i % tiles_per_batch
    prev_on = jnp.where((t == 0) | (t == 1), 0.0, 1.0)
    next_on = jnp.where((t == 0) | (t == tiles_per_batch - 1), 0.0, 1.0)
    mod = mod_ref[0]
    g1 = g1_ref[...]
    wc = wc_ref[...]
    z = _dot(_normmod(x_ref[...], g1, mod, 0).astype(BF16), wc)
    zp = _dot(_normmod(xp_ref[...], g1, mod, 0).astype(BF16), wc)[7:8] * prev_on
    zn = _dot(_normmod(xn_ref[...], g1, mod, 0).astype(BF16), wc)[0:1] * next_on
    row = lax.broadcasted_iota(jnp.int32, (TM, 1), 0)
    z_up = jnp.where(row == 0, zp, pltpu.roll(z, 1, 0))
    z_dn = jnp.where(row == TM - 1, zn, pltpu.roll(z, TM - 1, 0))
    z = z + mu_ref[...] * (0.5 * (z_up + z_dn) - z)

    r = z[:, 0:HP]
    k = z[:, HP:2 * HP]
    v = z[:, 2 * HP:3 * HP]
    zw = jnp.tanh(z[:, 3 * HP:3 * HP + LANE]).astype(BF16)
    za = z[:, 3 * HP + LANE:3 * HP + 2 * LANE].astype(BF16)
    zg = _sigmoid(z[:, 3 * HP + 2 * LANE:]).astype(BF16)

    kk = k * kk_ref[...]
    ka = ka_ref[...]
    rk = rk_ref[...]
    a = [_sigmoid(a0_ref[d:d + 1] + _dot(za, a2_ref[d])) for d in range(2)]
    kd = [k * (1.0 + (a[d] - 1.0) * ka) for d in range(2)]
    for d in range(2):
        logw_out[d] = -DECAY_SCALE * _sigmoid(w0_ref[d:d + 1] + _dot(zw, w2_ref[d]))
        kd_out[d] = kd[d]
    r_out[...] = r
    v_out[...] = v
    g_out[...] = _dot(zg, g2_ref[...])
    for hh in range(C_HEADS):
        sl = slice(hh * LANE, (hh + 1) * LANE)
        kh = kk[:, sl]
        kh = kh * lax.rsqrt(jnp.maximum(jnp.sum(kh * kh, axis=-1, keepdims=True), 1e-12))
        nkk_out[:, sl] = -kh
        rh = r[:, sl] * rk[:, sl]
        bon = jnp.zeros((TM, 1), F32)
        for d in range(2):
            beta_out[d, :, sl] = kh * a[d][:, sl]
            bon = bon + jnp.sum(rh * kd[d][:, sl], axis=-1, keepdims=True)
        bonus_out[:, sl] = bon * v[:, sl]


def _scan_kernel(r_ref, v_ref, nkk_ref, logw_ref, beta_ref, kd_ref, o_ref, s_ref):
    d = pl.program_id(1)
    j = pl.program_id(2)

    @pl.when(j == 0)
    def _():
        s_ref[...] = jnp.zeros_like(s_ref)

    c = SCAN_C
    row = lax.broadcasted_iota(jnp.int32, (c, c), 0)
    col = lax.broadcasted_iota(jnp.int32, (c, c), 1)
    delta = (col - row) * (1 - 2 * d)
    strict = delta < 0
    incl = delta <= 0
    tri = jnp.where(incl, 1.0, 0.0).astype(F32)
    eye = jnp.where(delta == 0, 1.0, 0.0).astype(F32)
    r128 = lax.broadcasted_iota(jnp.int32, (LANE, LANE), 0)
    c128 = lax.broadcasted_iota(jnp.int32, (LANE, LANE), 1)
    eye128 = jnp.where(r128 == c128, 1.0, 0.0).astype(F32)

    for hh in range(C_HEADS):
        sl = slice(hh * LANE, (hh + 1) * LANE)
        lw = logw_ref[0, :, sl]
        cs = _dot(tri, lw, HIGHEST)
        tot = jnp.where(d == 0, cs[c - 1:c], cs[0:1])
        p_end = jnp.exp(tot)
        p_inv = jnp.exp(-cs)
        al = nkk_ref[:, sl] * jnp.exp(cs - lw)
        bh = beta_ref[0, :, sl] * p_inv
        kh = kd_ref[0, :, sl] * p_inv
        rb = r_ref[:, sl] * jnp.exp(cs)
        v = v_ref[:, sl]

        ar = jnp.concatenate([al, rb], axis=0)
        xb = _dot_nt(ar, bh, HIGHEST)
        xk = _dot_nt(ar, kh, HIGHEST)
        l_ab = jnp.where(strict, xb[:c], 0.0)
        m_rb = jnp.where(incl, xb[c:], 0.0)
        l_ak = jnp.where(strict, xk[:c], 0.0)
        m_rk = jnp.where(incl, xk[c:], 0.0)

        t_inv = eye + l_ab
        l_pow = l_ab
        for _ in range(5):
            l_pow = _dot(l_pow, l_pow, HIGHEST)
            t_inv = t_inv + _dot(t_inv, l_pow, HIGHEST)

        um = _dot(t_inv, jnp.concatenate([_dot(l_ak, v, HIGHEST), al], axis=1), HIGHEST)
        u0 = um[:, :LANE]
        m1 = um[:, LANE:]
        g_mat = (eye128 + _dot_tn(m1, bh, HIGHEST)) * p_end
        h_mat = _dot_tn(jnp.concatenate([u0, v], axis=0),
                        jnp.concatenate([bh, kh], axis=0), HIGHEST) * p_end
        om = _dot(m_rb, um, HIGHEST)
        s0 = s_ref[hh]
        o_ref[0, :, sl] = (_dot_nt(rb + om[:, LANE:], s0, HIGHEST) + om[:, :LANE]
                           + _dot(m_rk, v, HIGHEST))
        s_ref[hh] = _dot(s0, g_mat, HIGHEST) + h_mat


def _attn_kernel(q_ref, k_ref, v_ref, o_ref, *, n_ctx, k_chunk):
    qi = pl.program_id(2)
    q = q_ref[0]

    @pl.when(qi == 0)
    def _():
        s = _dot_nt(q, k_ref[0, :n_ctx, :])
        p = jnp.exp(s - jnp.max(s, axis=-1, keepdims=True))
        l = jnp.sum(p, axis=-1, keepdims=True)
        o_ref[0] = (_dot(p.astype(BF16), v_ref[0, :n_ctx, :]) / l).astype(BF16)

    @pl.when(qi > 0)
    def _():
        n_k = k_ref.shape[1]
        m = jnp.full((q.shape[0], 1), -jnp.inf, F32)
        l = jnp.zeros((q.shape[0], 1), F32)
        acc = jnp.zeros((q.shape[0], LANE), F32)
        for c0 in range(0, n_k, k_chunk):
            s = _dot_nt(q, k_ref[0, c0:c0 + k_chunk, :])
            m_new = jnp.maximum(m, jnp.max(s, axis=-1, keepdims=True))
            p = jnp.exp(s - m_new)
            corr = jnp.exp(m - m_new)
            l = l * corr + jnp.sum(p, axis=-1, keepdims=True)
            acc = acc * corr + _dot(p.astype(BF16), v_ref[0, c0:c0 + k_chunk, :])
            m = m_new
        o_ref[0] = (acc / l).astype(BF16)


def _merge_kernel(of_ref, ob_ref, bonus_ref, g_ref, ya_ref, yb_ref, gate_ref, x_ref,
                  mod_ref, gng_ref, gnb_ref, wa_ref, wb_ref, wc_ref, wo_ref, n2_ref,
                  rw_ref, rb_ref, xn_out, h2_out, comb_out):
    mod = mod_ref[0]
    o = of_ref[0] + ob_ref[0]
    lane = lax.broadcasted_iota(jnp.int32, (1, LANE), 1)
    real = lane < C_HEAD
    ycs = []
    for hh in range(C_HEADS):
        sl = slice(hh * LANE, (hh + 1) * LANE)
        oh = o[:, sl]
        mean = jnp.sum(oh, axis=-1, keepdims=True) * (1.0 / C_HEAD)
        dev = jnp.where(real, oh - mean, 0.0)
        var = jnp.sum(dev * dev, axis=-1, keepdims=True) * (1.0 / C_HEAD)
        y = dev * lax.rsqrt(var + C_GN_EPS) * gng_ref[:, sl] + gnb_ref[:, sl]
        ycs.append(((y + bonus_ref[:, sl]) * g_ref[:, sl]).astype(BF16))
    yc = jnp.concatenate(ycs, axis=1)
    gate = gate_ref[...].astype(F32)
    merged = (gate[:, :D_MODEL] * _dot(ya_ref[...], wa_ref[...])
              + gate[:, D_MODEL:2 * D_MODEL] * _dot(yb_ref[...], wb_ref[...])
              + gate[:, 2 * D_MODEL:] * _dot(yc, wc_ref[...]))
    xn = x_ref[...] + mod[2:3] * _dot(merged.astype(BF16), wo_ref[...])
    xn_out[...] = xn
    h2 = _normmod(xn, n2_ref[...], mod, 3).astype(BF16)
    h2_out[...] = h2

    scores = _sigmoid(_dot(h2, rw_ref[...]))
    neg = -jnp.inf
    sel = jnp.where(lane < N_EXPERTS, scores + rb_ref[...], neg)

    lane_f = lane.astype(F32)

    def top1(s):
        mx = jnp.max(s, axis=-1, keepdims=True)
        idx = jnp.min(jnp.where(s == mx, lane_f, float(LANE)), axis=-1, keepdims=True)
        return mx, idx.astype(jnp.int32)

    best = None
    for g in range(N_GROUPS):
        in_g = (lane >= g * EXPERTS_PER_GROUP) & (lane < (g + 1) * EXPERTS_PER_GROUP)
        s = jnp.where(in_g, sel, neg)
        m1, i1 = top1(s)
        m2, _ = top1(jnp.where(lane == i1, neg, s))
        score = m1 + m2
        if best is None:
            best, gidx = score, jnp.zeros_like(i1)
        else:
            better = score > best
            gidx = jnp.where(better, g, gidx)
            best = jnp.where(better, score, best)
    lo = gidx * EXPERTS_PER_GROUP
    s = jnp.where((lane >= lo) & (lane < lo + EXPERTS_PER_GROUP), sel, neg)
    _, i1 = top1(s)
    _, i2 = top1(jnp.where(lane == i1, neg, s))
    picked = jnp.where((lane == i1) | (lane == i2), scores, 0.0)
    comb_out[...] = picked / jnp.sum(picked, axis=-1, keepdims=True)


def _moe_kernel(h2_ref, comb_ref, xn_ref, m0_ref, m1_ref, m2_ref, w1_ref, w3_ref, w2_ref,
                fg_ref, out_ref, acc_ref, *, last):
    e = pl.program_id(1)

    @pl.when(e == 0)
    def _():
        acc_ref[...] = jnp.zeros_like(acc_ref)

    t = h2_ref[...]
    a = _dot(t, w1_ref[0])
    he = (a * _sigmoid(a) * _dot(t, w3_ref[0])).astype(BF16)
    lane = lax.broadcasted_iota(jnp.int32, (1, LANE), 1)
    ce = jnp.sum(jnp.where(lane == e, comb_ref[...], 0.0), axis=-1, keepdims=True)
    acc_ref[...] += ce * _dot(he, w2_ref[0])

    @pl.when(e == N_EXPERTS - 1)
    def _():
        for s, m_ref in enumerate((m0_ref, m1_ref, m2_ref)):
            rows = slice(s * TM, (s + 1) * TM)
            y = xn_ref[rows] + m_ref[0][5:6] * acc_ref[rows]
            if last:
                y = _rms(y) * fg_ref[...]
            out_ref[rows] = y


def _head_pad_cols(w, width):
    lead = w.shape[:-1]
    w = w.reshape(lead + (C_HEADS, width))
    w = jnp.pad(w, [(0, 0)] * len(lead) + [(0, 0), (0, LANE - width)])
    return w.reshape(lead + (HP,))


def _head_pad_rows(w, width):
    n = w.shape[-1]
    w = w.reshape(C_HEADS, width, n)
    w = jnp.pad(w, ((0, 0), (0, LANE - width), (0, 0)))
    return w.reshape(HP, n)


def _split_rope(w):
    lead = w.shape[:-1]
    w = w.reshape(lead + (2, 2, ROPE_FREQS))
    x1 = w[..., 0, :].reshape(lead + (2 * ROPE_FREQS,))
    x2 = w[..., 1, :].reshape(lead + (2 * ROPE_FREQS,))
    return x1, x2


def _rope_tables(n_ctx, t):
    rows = t // GRID_W
    row = jnp.repeat(jnp.arange(rows, dtype=F32), GRID_W)
    col = jnp.tile(jnp.arange(GRID_W, dtype=F32), rows)
    inv = jnp.power(ROPE_BASE, -jnp.arange(ROPE_FREQS, dtype=F32) / ROPE_FREQS)
    ang = jnp.concatenate([row[:, None] * inv, col[:, None] * inv], axis=1)
    cos = jnp.concatenate([jnp.ones((n_ctx, 2 * ROPE_FREQS), F32), jnp.cos(ang)], axis=0)
    sin = jnp.concatenate([jnp.zeros((n_ctx, 2 * ROPE_FREQS), F32), jnp.sin(ang)], axis=0)
    n = n_ctx + t
    one = jnp.ones((n, B_NOPE), F32)
    zero = lambda w: jnp.zeros((n, w), F32)
    cq = jnp.concatenate([one, cos, cos, zero(LANE - B_NOPE - B_ROPE)], axis=1) * MLA_SCALE
    sq = jnp.concatenate([zero(B_NOPE), -sin, sin, zero(LANE - B_NOPE - B_ROPE)], axis=1) * MLA_SCALE
    ck = jnp.concatenate([cos, cos, zero(LANE - B_ROPE)], axis=1)
    sk = jnp.concatenate([-sin, sin, zero(LANE - B_ROPE)], axis=1)
    return cq, sq, ck, sk


def _layer_weights(p, l):
    w = {}
    w_in = p['w_in'][l]
    d = w_in.shape[0]
    o = 0
    w_q = w_in[:, o:o + B_Q_LORA]; o += B_Q_LORA
    w_kv = w_in[:, o:o + B_KV_LORA]; o += B_KV_LORA
    w_kr = w_in[:, o:o + B_ROPE]; o += B_ROPE
    w_c = w_in[:, o:o + C_COLS]; o += C_COLS
    w_a = w_in[:, o:o + A_COLS]; o += A_COLS
    w_g = w_in[:, o:o + G_COLS]

    k1, k2 = _split_rope(w_kr)
    zpad = jnp.zeros((d, LANE - B_ROPE), F32)
    w['win_b'] = jnp.concatenate([w_q, w_kv, k1, k2, zpad, k2, k1, zpad], axis=1).astype(BF16)

    wuq = p['b_w_uq'][l].reshape(B_Q_LORA, B_HEADS, B_NOPE + B_ROPE)
    q1, q2 = _split_rope(wuq[..., B_NOPE:])
    qz = jnp.zeros((B_Q_LORA, B_HEADS, LANE - B_NOPE - B_ROPE), F32)
    w['wq'] = jnp.concatenate([wuq[..., :B_NOPE], q1, q2, qz], axis=-1).reshape(B_Q_LORA, HP).astype(BF16)
    w['wqs'] = jnp.concatenate([wuq[..., :B_NOPE], q2, q1, qz], axis=-1).reshape(B_Q_LORA, HP).astype(BF16)
    wukv = p['b_w_ukv'][l].reshape(B_KV_LORA, B_HEADS, B_NOPE + B_VDIM)
    w['wk'] = _head_pad_cols(wukv[..., :B_NOPE].reshape(B_KV_LORA, -1), B_NOPE).astype(BF16)
    w['wv'] = _head_pad_cols(wukv[..., B_NOPE:].reshape(B_KV_LORA, -1), B_VDIM).astype(BF16)
    place = jnp.zeros((LANE, B_HEADS, LANE), F32)
    idx = jnp.arange(B_ROPE)
    place = place.at[idx, :, B_NOPE + idx].set(1.0)
    w['place'] = place.reshape(LANE, HP).astype(BF16)
    w['qg'] = p['b_q_norm'][l][None]
    w['kvg'] = p['b_kv_norm'][l][None]

    w['wa'] = w_a.astype(BF16)
    w['wg'] = w_g.astype(BF16)
    w['vg'] = p['a_v_gain'][l][None]
    w['ws'] = p['a_ws'][l].astype(BF16)
    w['bsb'] = jnp.broadcast_to(p['a_bs'][l].T[:, :, None], (A_CHUNK, A_GROUPS, LANE)).reshape(A_CHUNK, A_WIDTH)

    cw = C_WIDTH
    pieces = [_head_pad_cols(w_c[:, i * cw:(i + 1) * cw], C_HEAD) for i in range(3)]
    w['wc'] = jnp.concatenate(pieces + [w_c[:, 3 * cw:]], axis=1).astype(BF16)
    mu = p['c_mu'][l]
    w['mu'] = jnp.concatenate([_head_pad_cols(mu[i * cw:(i + 1) * cw], C_HEAD) for i in range(3)]
                              + [mu[3 * cw:]])[None]

    def lora_block(m, rank):
        m = _head_pad_cols(m, C_HEAD)
        z = jnp.zeros_like(m[0])
        return jnp.stack([jnp.concatenate([m[0], z], axis=0), jnp.concatenate([z, m[1]], axis=0)]).astype(BF16)

    w['w2'] = lora_block(p['c_w2'][l], C_DECAY_LORA)
    w['a2'] = lora_block(p['c_a2'][l], C_AAA_LORA)
    w['w0'] = _head_pad_cols(p['c_w0'][l], C_HEAD)
    w['a0'] = _head_pad_cols(p['c_a0'][l], C_HEAD)
    w['kk'] = _head_pad_cols(p['c_k_k'][l], C_HEAD)[None]
    w['ka'] = _head_pad_cols(p['c_k_a'][l], C_HEAD)[None]
    w['rk'] = _head_pad_cols(p['c_r_k'][l].reshape(-1), C_HEAD)[None]
    w['g2'] = _head_pad_cols(p['c_g2'][l], C_HEAD).astype(BF16)
    w['gng'] = _head_pad_cols(p['c_gn_g'][l], C_HEAD)[None]
    w['gnb'] = _head_pad_cols(p['c_gn_b'][l], C_HEAD)[None]

    w['wup_a'] = p['w_up_a'][l].astype(BF16)
    w['wup_b'] = _head_pad_rows(p['w_up_b'][l], B_VDIM).astype(BF16)
    w['wup_c'] = _head_pad_rows(p['w_up_c'][l], C_HEAD).astype(BF16)
    w['wo'] = p['w_o'][l].astype(BF16)
    w['g1'] = p['norm1_g'][l][None]
    w['n2'] = p['norm2_g'][l][None]
    w['w1'] = p['moe_w1'][l].astype(BF16)
    w['w3'] = p['moe_w3'][l].astype(BF16)
    w['w2e'] = p['moe_w2'][l].astype(BF16)
    return w


def _layer(xs, mods, w, tabs, rw, rb, fg, *, bn, lb, n_ctx, last):
    m, d = xs.shape
    n_tiles = m // TM
    tpb = lb // TM
    row = lambda i: (i, 0)
    seg = lambda i: (jnp.where(i % tpb == 0, 0, 1 + i // tpb), 0, 0)
    x_spec = pl.BlockSpec((TM, d), row)
    mod_spec = pl.BlockSpec((1, 6, d), seg)
    hp_spec = pl.BlockSpec((TM, HP), row)
    hp2_spec = pl.BlockSpec((2, TM, HP), lambda i: (0, i, 0))
    hp_f32 = jax.ShapeDtypeStruct((m, HP), F32)
    hp_bf16 = jax.ShapeDtypeStruct((m, HP), BF16)
    hp2_f32 = jax.ShapeDtypeStruct((2, m, HP), F32)
    cq, sq, ck, sk = tabs
    tab_spec = pl.BlockSpec((TM, LANE), lambda i: (i % tpb, 0))

    q, k, v = pl.pallas_call(
        _mla_kernel,
        grid=(n_tiles,),
        in_specs=[x_spec, _full((1, d)), mod_spec, _full(w['win_b'].shape), _full((1, B_Q_LORA)),
                  _full((1, B_KV_LORA)), _full(w['wq'].shape), _full(w['wqs'].shape),
                  _full(w['wk'].shape), _full(w['wv'].shape), _full(w['place'].shape),
                  tab_spec, tab_spec, tab_spec, tab_spec],
        out_specs=[hp_spec, hp_spec, hp_spec],
        out_shape=[hp_bf16, hp_bf16, hp_bf16],
        compiler_params=_params(1),
        name="mla_proj",
    )(xs, w['g1'], mods, w['win_b'], w['qg'], w['kvg'], w['wq'], w['wqs'], w['wk'], w['wv'],
      w['place'], cq, sq, ck, sk)

    ya, gate = pl.pallas_call(
        _gmlp_kernel,
        grid=(n_tiles,),
        in_specs=[x_spec, _full((1, d)), mod_spec, _full(w['wa'].shape), _full(w['wg'].shape),
                  _full((1, A_WIDTH)), _full(w['ws'].shape), _full(w['bsb'].shape)],
        out_specs=[pl.BlockSpec((TM, A_WIDTH), row), pl.BlockSpec((TM, G_COLS), row)],
        out_shape=[jax.ShapeDtypeStruct((m, A_WIDTH), BF16), jax.ShapeDtypeStruct((m, G_COLS), BF16)],
        compiler_params=_params(1),
        name="gmlp_gates",
    )(xs, w['g1'], mods, w['wa'], w['wg'], w['vg'], w['ws'], w['bsb'])

    sub = TM // 8
    r, vv, nkk, logw, beta, kd, bonus, gg = pl.pallas_call(
        functools.partial(_feat_kernel, tiles_per_batch=tpb),
        grid=(n_tiles,),
        in_specs=[x_spec,
                  pl.BlockSpec((8, d), lambda i: (jnp.maximum(i * sub - 1, 0), 0)),
                  pl.BlockSpec((8, d), lambda i: (jnp.minimum((i + 1) * sub, m // 8 - 1), 0)),
                  _full((1, d)), mod_spec, _full(w['wc'].shape), _full(w['mu'].shape),
                  _full((2, HP)), _full(w['w2'].shape), _full((2, HP)), _full(w['a2'].shape),
                  _full((1, HP)), _full((1, HP)), _full((1, HP)), _full(w['g2'].shape)],
        out_specs=[hp_spec, hp_spec, hp_spec, hp2_spec, hp2_spec, hp2_spec, hp_spec, hp_spec],
        out_shape=[hp_f32, hp_f32, hp_f32, hp2_f32, hp2_f32, hp2_f32, hp_f32, hp_f32],
        compiler_params=_params(1),
        name="rwkv_features",
    )(xs, xs, xs, w['g1'], mods, w['wc'], w['mu'], w['w0'], w['w2'], w['a0'], w['a2'],
      w['kk'], w['ka'], w['rk'], w['g2'])

    nc = lb // SCAN_C
    nctx_c = n_ctx // SCAN_C

    def chunk_of(b, dd, j):
        bwd = jnp.where(j < nctx_c, nctx_c - 1 - j, nc - 1 + nctx_c - j)
        return b * nc + jnp.where(dd == 0, j, bwd)

    tok_spec = pl.BlockSpec((SCAN_C, HP), lambda b, dd, j: (chunk_of(b, dd, j), 0))
    dir_spec = pl.BlockSpec((1, SCAN_C, HP), lambda b, dd, j: (dd, chunk_of(b, dd, j), 0))
    o_scan = pl.pallas_call(
        _scan_kernel,
        grid=(bn, 2, nc),
        in_specs=[tok_spec, tok_spec, tok_spec, dir_spec, dir_spec, dir_spec],
        out_specs=dir_spec,
        out_shape=hp2_f32,
        scratch_shapes=[pltpu.VMEM((C_HEADS, LANE, LANE), F32)],
        compiler_params=_params(3),
        name="rwkv_scan",
    )(r, vv, nkk, logw, beta, kd)

    q3, k3, v3 = (t.reshape(bn, lb, HP) for t in (q, k, v))
    k_chunk = lb // 3
    yb = pl.pallas_call(
        functools.partial(_attn_kernel, n_ctx=n_ctx, k_chunk=k_chunk),
        grid=(bn, B_HEADS, tpb),
        in_specs=[pl.BlockSpec((1, TM, LANE), lambda b, hh, qi: (b, qi, hh)),
                  pl.BlockSpec((1, lb, LANE), lambda b, hh, qi: (b, 0, hh)),
                  pl.BlockSpec((1, lb, LANE), lambda b, hh, qi: (b, 0, hh))],
        out_specs=pl.BlockSpec((1, TM, LANE), lambda b, hh, qi: (b, qi, hh)),
        out_shape=jax.ShapeDtypeStruct((bn, lb, HP), BF16),
        compiler_params=_params(3),
        name="mla_attention",
    )(q3, k3, v3).reshape(m, HP)

    o_spec = lambda dd: pl.BlockSpec((1, TM, HP), lambda i: (dd, i, 0))
    xn, h2, comb = pl.pallas_call(
        _merge_kernel,
        grid=(n_tiles,),
        in_specs=[o_spec(0), o_spec(1), hp_spec, hp_spec, pl.BlockSpec((TM, A_WIDTH), row), hp_spec,
                  pl.BlockSpec((TM, G_COLS), row), x_spec, mod_spec, _full((1, HP)), _full((1, HP)),
                  _full(w['wup_a'].shape), _full(w['wup_b'].shape), _full(w['wup_c'].shape),
                  _full(w['wo'].shape), _full((1, d)), _full(rw.shape), _full(rb.shape)],
        out_specs=[x_spec, x_spec, pl.BlockSpec((TM, LANE), row)],
        out_shape=[jax.ShapeDtypeStruct((m, d), F32), jax.ShapeDtypeStruct((m, d), BF16),
                   jax.ShapeDtypeStruct((m, LANE), F32)],
        compiler_params=_params(1),
        name="merge_router",
    )(o_scan, o_scan, bonus, gg, ya, yb, gate, xs, mods, w['gng'], w['gnb'], w['wup_a'], w['wup_b'],
      w['wup_c'], w['wo'], w['n2'], rw, rb)

    tm_moe = MOE_SUB * TM
    mrow = lambda i, e: (i, 0)
    mseg = lambda s: pl.BlockSpec((1, 6, d), lambda i, e: seg(i * MOE_SUB + s))
    wexp = lambda shape: pl.BlockSpec((1,) + shape, lambda i, e: (e, 0, 0))
    return pl.pallas_call(
        functools.partial(_moe_kernel, last=last),
        grid=(m // tm_moe, N_EXPERTS),
        in_specs=[pl.BlockSpec((tm_moe, d), mrow), pl.BlockSpec((tm_moe, LANE), mrow),
                  pl.BlockSpec((tm_moe, d), mrow), mseg(0), mseg(1), mseg(2),
                  wexp((d, D_EXPERT)), wexp((d, D_EXPERT)), wexp((D_EXPERT, d)),
                  pl.BlockSpec((1, d), lambda i, e: (0, 0))],
        out_specs=pl.BlockSpec((tm_moe, d), mrow),
        out_shape=jax.ShapeDtypeStruct((m, d), F32),
        scratch_shapes=[pltpu.VMEM((tm_moe, d), F32)],
        compiler_params=_params(2),
        name="moe",
    )(h2, comb, xn, mods, mods, mods, w['w1'], w['w3'], w['w2e'], fg)


def kernel(x, c, ctx, c_ctx, mod_w, mod_b, norm1_g, norm2_g, w_in, a_v_gain, a_ws, a_bs, b_q_norm, b_w_uq, b_kv_norm, b_w_ukv, c_mu, c_w0, c_w2, c_a0, c_a2, c_g2, c_k_k, c_k_a, c_r_k, c_gn_g, c_gn_b, w_up_a, w_up_b, w_up_c, w_o, router_w, router_b, moe_w1, moe_w3, moe_w2, final_g):
    p = dict(w_in=w_in, a_v_gain=a_v_gain, a_ws=a_ws, a_bs=a_bs, b_q_norm=b_q_norm, b_w_uq=b_w_uq,
             b_kv_norm=b_kv_norm, b_w_ukv=b_w_ukv, c_mu=c_mu, c_w0=c_w0, c_w2=c_w2, c_a0=c_a0,
             c_a2=c_a2, c_g2=c_g2, c_k_k=c_k_k, c_k_a=c_k_a, c_r_k=c_r_k, c_gn_g=c_gn_g,
             c_gn_b=c_gn_b, w_up_a=w_up_a, w_up_b=w_up_b, w_up_c=w_up_c, w_o=w_o,
             norm1_g=norm1_g, norm2_g=norm2_g, moe_w1=moe_w1, moe_w3=moe_w3, moe_w2=moe_w2)
    bn, t, d = x.shape
    n_ctx = ctx.shape[1]
    lb = n_ctx + t
    depth = mod_w.shape[0]
    assert bn == 2 and n_ctx == TM and lb % (MOE_SUB * TM) == 0 and lb % SCAN_C == 0

    cvec = jnp.concatenate([c_ctx[None], c, jnp.zeros((8 - 1 - bn, d), F32)], axis=0)
    mods = _modulation(cvec, mod_w, mod_b).reshape(depth, 8, 6, d)

    tabs = _rope_tables(n_ctx, t)
    rw = jnp.pad(router_w, ((0, 0), (0, LANE - N_EXPERTS))).astype(BF16)
    rb = jnp.pad(router_b, (0, LANE - N_EXPERTS))[None]
    fg = final_g[None]

    xs = jnp.concatenate([ctx, x], axis=1).reshape(bn * lb, d)
    for l in range(depth):
        xs = _layer(xs, mods[l], _layer_weights(p, l), tabs, rw, rb, fg,
                    bn=bn, lb=lb, n_ctx=n_ctx, last=(l == depth - 1))
    return xs.reshape(bn, lb, d)[:, n_ctx:]
```

```python
import functools
import math

import jax
import jax.numpy as jnp
from jax import lax
from jax.experimental import pallas as pl
from jax.experimental.pallas import tpu as pltpu

F32 = jnp.float32
BF16 = jnp.bfloat16
HIGHEST = lax.Precision.HIGHEST

D_MODEL = 1024
GRID_W = 64
EPS = 1e-6

A_WIDTH = 512
A_GROUPS = 4
A_CHUNK = 128

B_HEADS = 8
B_Q_LORA = 384
B_KV_LORA = 256
B_NOPE = 64
B_ROPE = 32
B_VDIM = 64
ROPE_BASE = 10000.0
ROPE_FREQS = B_ROPE // 4
MLA_SCALE = (B_NOPE + B_ROPE) ** -0.5

C_HEADS = 8
C_HEAD = 64
C_WIDTH = C_HEADS * C_HEAD
C_DECAY_LORA = 64
C_AAA_LORA = 64
C_GATE_LORA = 128
C_GN_EPS = 64e-5
DECAY_SCALE = 0.6065306597126334

B_COLS = B_Q_LORA + B_KV_LORA + B_ROPE
C_COLS = 3 * C_WIDTH + 2 * C_DECAY_LORA + 2 * C_AAA_LORA + C_GATE_LORA
A_COLS = 2 * A_WIDTH
G_COLS = 3 * D_MODEL

N_EXPERTS = 16
N_GROUPS = 4
EXPERTS_PER_GROUP = 4
D_EXPERT = 512

LANE = 128
HP = C_HEADS * LANE
TM = 256
SCAN_C = 64
MOE_SUB = 3
VMEM_LIMIT = 52 * 1024 * 1024
SCAN_EXACT = dict(gram=False, inv=False, apply=False, state=True)


def _dot(a, b, precision=None):
    return jnp.dot(a, b, preferred_element_type=F32, precision=precision)


def _dot_nt(a, b, precision=None):
    return lax.dot_general(a, b, (((1,), (1,)), ((), ())),
                           preferred_element_type=F32, precision=precision)


def _dot_tn(a, b, precision=None):
    return lax.dot_general(a, b, (((0,), (0,)), ((), ())),
                           preferred_element_type=F32, precision=precision)


def _rms(x, eps=EPS):
    return x * lax.rsqrt(jnp.mean(x * x, axis=-1, keepdims=True) + eps)


def _normmod(x, g, mod, first):
    return _rms(x) * g * (1.0 + mod[first + 1:first + 2]) + mod[first:first + 1]


def _sigmoid(x):
    return 1.0 / (1.0 + jnp.exp(-x))


def _params(n_grid):
    return pltpu.CompilerParams(
        dimension_semantics=("arbitrary",) * n_grid, vmem_limit_bytes=VMEM_LIMIT)


def _full(shape):
    n = len(shape)
    return pl.BlockSpec(shape, lambda *_: (0,) * n)


def _mod_kernel(c_ref, w_ref, b_ref, o_ref):
    c = c_ref[...]
    act = (c * _sigmoid(c)).astype(BF16)
    o_ref[0] = _dot(act, w_ref[0].astype(BF16)) + b_ref[0]


def _modulation(cvec, mod_w, mod_b):
    depth, d, n = mod_w.shape
    tn = 1536
    return pl.pallas_call(
        _mod_kernel,
        grid=(depth, n // tn),
        in_specs=[
            pl.BlockSpec((8, d), lambda l, j: (0, 0)),
            pl.BlockSpec((1, d, tn), lambda l, j: (l, 0, j)),
            pl.BlockSpec((1, 1, tn), lambda l, j: (l, 0, j)),
        ],
        out_specs=pl.BlockSpec((1, 8, tn), lambda l, j: (l, 0, j)),
        out_shape=jax.ShapeDtypeStruct((depth, 8, n), F32),
        compiler_params=_params(2),
        name="modulation",
    )(cvec, mod_w, mod_b.reshape(depth, 1, n))


def _mla_kernel(x_ref, g1_ref, mod_ref, win_ref, qg_ref, kvg_ref, wq_ref, wqs_ref,
                wk_ref, wv_ref, place_ref, cq_ref, sq_ref, ck_ref, sk_ref,
                q_out, k_out, v_out):
    h = _normmod(x_ref[...], g1_ref[...], mod_ref[0], 0).astype(BF16)
    z = _dot(h, win_ref[...])
    zq = z[:, :B_Q_LORA]
    zkv = z[:, B_Q_LORA:B_Q_LORA + B_KV_LORA]
    kr_a = z[:, B_Q_LORA + B_KV_LORA:B_Q_LORA + B_KV_LORA + LANE]
    kr_b = z[:, B_Q_LORA + B_KV_LORA + LANE:]
    qn = (_rms(zq) * qg_ref[...]).astype(BF16)
    kvn = (_rms(zkv) * kvg_ref[...]).astype(BF16)
    q1 = _dot(qn, wq_ref[...])
    q2 = _dot(qn, wqs_ref[...])
    kr = (kr_a * ck_ref[...] + kr_b * sk_ref[...]).astype(BF16)
    k = _dot(kvn, wk_ref[...]) + _dot(kr, place_ref[...])
    cq = cq_ref[...]
    sq = sq_ref[...]
    for hh in range(B_HEADS):
        sl = slice(hh * LANE, (hh + 1) * LANE)
        q_out[:, sl] = (q1[:, sl] * cq + q2[:, sl] * sq).astype(BF16)
    k_out[...] = k.astype(BF16)
    v_out[...] = _dot(kvn, wv_ref[...]).astype(BF16)


def _gelu_tanh(x):
    return 0.5 * x * (1.0 + jnp.tanh(math.sqrt(2.0 / math.pi) * (x + 0.044715 * (x * x * x))))


def _gmlp_kernel(x_ref, g1_ref, mod_ref, wa_ref, wg_ref, vg_ref, ws_ref, bsb_ref,
                 ya_out, gate_out):
    h = _normmod(x_ref[...], g1_ref[...], mod_ref[0], 0).astype(BF16)
    gate_out[...] = _sigmoid(_dot(h, wg_ref[...])).astype(BF16)
    zg = _gelu_tanh(_dot(h, wa_ref[...]))
    u = zg[:, :A_WIDTH]
    v = (_rms(zg[:, A_WIDTH:]) * vg_ref[...]).astype(BF16)
    for c in range(TM // A_CHUNK):
        rows = slice(c * A_CHUNK, (c + 1) * A_CHUNK)
        for g in range(A_GROUPS):
            cols = slice(g * LANE, (g + 1) * LANE)
            mixed = _dot(ws_ref[g], v[rows, cols]) + bsb_ref[:, cols]
            ya_out[rows, cols] = (u[rows, cols] * mixed).astype(BF16)


def _feat_kernel(x_ref, xp_ref, xn_ref, g1_ref, mod_ref, wc_ref, mu_ref,
                 w0_ref, w2_ref, a0_ref, a2_ref, kk_ref, ka_ref, rk_ref, g2_ref,
                 v_out, ab_out, bh_out, kh_out, rb_out, pe_out, bonus_out, g_out,
                 *, tiles_per_batch):
    i = pl.program_id(0)
    t = i % tiles_per_batch
    prev_on = jnp.where((t == 0) | (t == 1), 0.0, 1.0)
    next_on = jnp.where((t == 0) | (t == tiles_per_batch - 1), 0.0, 1.0)
    mod = mod_ref[0]
    g1 = g1_ref[...]
    wc = wc_ref[...]
    z = _dot(_normmod(x_ref[...], g1, mod, 0).astype(BF16), wc)
    zp = _dot(_normmod(xp_ref[...], g1, mod, 0).astype(BF16), wc)[7:8] * prev_on
    zn = _dot(_normmod(xn_ref[...], g1, mod, 0).astype(BF16), wc)[0:1] * next_on
    row = lax.broadcasted_iota(jnp.int32, (TM, 1), 0)
    z_up = jnp.where(row == 0, zp, pltpu.roll(z, 1, 0))
    z_dn = jnp.where(row == TM - 1, zn, pltpu.roll(z, TM - 1, 0))
    z = z + mu_ref[...] * (0.5 * (z_up + z_dn) - z)

    r = z[:, 0:HP]
    k = z[:, HP:2 * HP]
    v = z[:, 2 * HP:3 * HP]
    zw = jnp.tanh(z[:, 3 * HP:3 * HP + LANE]).astype(BF16)
    za = z[:, 3 * HP + LANE:3 * HP + 2 * LANE].astype(BF16)
    zg = _sigmoid(z[:, 3 * HP + 2 * LANE:]).astype(BF16)

    kk = k * kk_ref[...]
    ka = ka_ref[...]
    rk = rk_ref[...]
    a = [_sigmoid(a0_ref[d:d + 1] + _dot(za, a2_ref[d])) for d in range(2)]
    kd = [k * (1.0 + (a[d] - 1.0) * ka) for d in range(2)]
    v_out[...] = v
    g_out[...] = _dot(zg, g2_ref[...])

    ti = lax.broadcasted_iota(jnp.int32, (TM, TM), 0)
    tj = lax.broadcasted_iota(jnp.int32, (TM, TM), 1)
    shift = SCAN_C.bit_length() - 1
    same = lax.shift_right_logical(ti, shift) == lax.shift_right_logical(tj, shift)
    p_in, p_prev, p_inv = [], [], []
    for d in range(2):
        lw = -DECAY_SCALE * _sigmoid(w0_ref[d:d + 1] + _dot(zw, w2_ref[d]))
        tri = jnp.where(same & ((tj <= ti) if d == 0 else (tj >= ti)), 1.0, 0.0).astype(F32)
        cs = _dot(tri, lw, HIGHEST)
        p_in.append(jnp.exp(cs))
        p_prev.append(jnp.exp(cs - lw))
        p_inv.append(jnp.exp(-cs))
        kh_out[d] = kd[d] * p_inv[d]
        rb_out[d] = r * p_in[d]
        for cc in range(TM // SCAN_C):
            last = cc * SCAN_C + (SCAN_C - 1 if d == 0 else 0)
            pe_out[d, cc * 8:(cc + 1) * 8, :] = jnp.broadcast_to(p_in[d][last:last + 1], (8, HP))

    for hh in range(C_HEADS):
        sl = slice(hh * LANE, (hh + 1) * LANE)
        kh = kk[:, sl]
        kh = kh * lax.rsqrt(jnp.maximum(jnp.sum(kh * kh, axis=-1, keepdims=True), 1e-12))
        rh = r[:, sl] * rk[:, sl]
        bon = jnp.zeros((TM, 1), F32)
        for d in range(2):
            ab_out[d, :, sl] = -kh * p_prev[d][:, sl]
            bh_out[d, :, sl] = kh * a[d][:, sl] * p_inv[d][:, sl]
            bon = bon + jnp.sum(rh * kd[d][:, sl], axis=-1, keepdims=True)
        bonus_out[:, sl] = bon * v[:, sl]


def _mm(a, b, exact, dims=((1,), (0,))):
    if exact:
        return lax.dot_general(a, b, (dims, ((), ())), preferred_element_type=F32, precision=HIGHEST)
    return lax.dot_general(a.astype(BF16), b.astype(BF16), (dims, ((), ())),
                           preferred_element_type=F32)


_NT = ((1,), (1,))
_TN = ((0,), (0,))


def _scan_kernel(vf_ref, vb_ref, abf_ref, abb_ref, bhf_ref, bhb_ref, khf_ref, khb_ref,
                 rbf_ref, rbb_ref, pef_ref, peb_ref, of_ref, ob_ref, s_ref):
    @pl.when(pl.program_id(1) == 0)
    def _():
        s_ref[...] = jnp.zeros_like(s_ref)

    c = SCAN_C
    row = lax.broadcasted_iota(jnp.int32, (c, c), 0)
    col = lax.broadcasted_iota(jnp.int32, (c, c), 1)
    eye = jnp.where(row == col, 1.0, 0.0).astype(F32)
    r128 = lax.broadcasted_iota(jnp.int32, (LANE, LANE), 0)
    c128 = lax.broadcasted_iota(jnp.int32, (LANE, LANE), 1)
    eye128 = jnp.where(r128 == c128, 1.0, 0.0).astype(F32)
    refs = ((vf_ref, abf_ref, bhf_ref, khf_ref, rbf_ref, pef_ref, of_ref),
            (vb_ref, abb_ref, bhb_ref, khb_ref, rbb_ref, peb_ref, ob_ref))

    chains = [(d, hh) for d in range(2) for hh in range(C_HEADS)]
    ex = SCAN_EXACT

    def sl(hh):
        return slice(hh * LANE, (hh + 1) * LANE)

    al = lambda ch: refs[ch[0]][1][0, :, sl(ch[1])]
    bh = lambda ch: refs[ch[0]][2][0, :, sl(ch[1])]
    kh = lambda ch: refs[ch[0]][3][0, :, sl(ch[1])]
    rb = lambda ch: refs[ch[0]][4][0, :, sl(ch[1])]
    vv = lambda ch: refs[ch[0]][0][:, sl(ch[1])]
    pe = lambda ch: refs[ch[0]][5][0, 0:1, sl(ch[1])]
    strict = lambda ch: (col < row) if ch[0] == 0 else (col > row)
    incl = lambda ch: (col <= row) if ch[0] == 0 else (col >= row)

    xb = {ch: _mm(jnp.concatenate([al(ch), rb(ch)], axis=0), bh(ch), ex['gram'], _NT) for ch in chains}
    xk = {ch: _mm(jnp.concatenate([al(ch), rb(ch)], axis=0), kh(ch), ex['gram'], _NT) for ch in chains}
    l_pow = {ch: jnp.where(strict(ch), xb[ch][:c], 0.0) for ch in chains}
    um = {ch: jnp.concatenate([_mm(jnp.where(strict(ch), xk[ch][:c], 0.0), vv(ch), ex['apply']), al(ch)],
                              axis=1) for ch in chains}
    for it in range(6):
        um = {ch: um[ch] + _mm(l_pow[ch], um[ch], ex['inv']) for ch in chains}
        if it < 5:
            l_pow = {ch: _mm(l_pow[ch], l_pow[ch], ex['inv']) for ch in chains}
    om = {ch: _mm(jnp.where(incl(ch), xb[ch][c:], 0.0), um[ch], ex['apply']) for ch in chains}
    ok = {ch: _mm(jnp.where(incl(ch), xk[ch][c:], 0.0), vv(ch), ex['apply']) for ch in chains}
    for ch in chains:
        o_ref = refs[ch[0]][6]
        o_ref[:, sl(ch[1])] = (_mm(rb(ch) + om[ch][:, LANE:], s_ref[ch], ex['state'], _NT)
                               + om[ch][:, :LANE] + ok[ch])
    g_mat = {ch: (eye128 + _mm(um[ch][:, LANE:], bh(ch), ex['apply'], _TN)) * pe(ch) for ch in chains}
    h_mat = {ch: _mm(jnp.concatenate([um[ch][:, :LANE], vv(ch)], axis=0),
                     jnp.concatenate([bh(ch), kh(ch)], axis=0), ex['apply'], _TN) * pe(ch)
             for ch in chains}
    for ch in chains:
        s_ref[ch] = _mm(s_ref[ch], g_mat[ch], ex['state']) + h_mat[ch]


def _attn_kernel(q_ref, k_ref, v_ref, o_ref, *, n_ctx, k_chunk):
    qi = pl.program_id(2)
    q = q_ref[0]

    @pl.when(qi == 0)
    def _():
        s = _dot_nt(q, k_ref[0, :n_ctx, :])
        p = jnp.exp(s - jnp.max(s, axis=-1, keepdims=True))
        l = jnp.sum(p, axis=-1, keepdims=True)
        o_ref[0] = (_dot(p.astype(BF16), v_ref[0, :n_ctx, :]) / l).astype(BF16)

    @pl.when(qi > 0)
    def _():
        n_k = k_ref.shape[1]
        m = jnp.full((q.shape[0], 1), -jnp.inf, F32)
        l = jnp.zeros((q.shape[0], 1), F32)
        acc = jnp.zeros((q.shape[0], LANE), F32)
        for c0 in range(0, n_k, k_chunk):
            s = _dot_nt(q, k_ref[0, c0:c0 + k_chunk, :])
            m_new = jnp.maximum(m, jnp.max(s, axis=-1, keepdims=True))
            p = jnp.exp(s - m_new)
            corr = jnp.exp(m - m_new)
            l = l * corr + jnp.sum(p, axis=-1, keepdims=True)
            acc = acc * corr + _dot(p.astype(BF16), v_ref[0, c0:c0 + k_chunk, :])
            m = m_new
        o_ref[0] = (acc / l).astype(BF16)


def _merge_kernel(of_ref, ob_ref, bonus_ref, g_ref, ya_ref, yb_ref, gate_ref, x_ref,
                  mod_ref, gng_ref, gnb_ref, wa_ref, wb_ref, wc_ref, wo_ref, n2_ref,
                  rw_ref, rb_ref, xn_out, h2_out, comb_out):
    mod = mod_ref[0]
    o = of_ref[...] + ob_ref[...]
    lane = lax.broadcasted_iota(jnp.int32, (1, LANE), 1)
    real = lane < C_HEAD
    ycs = []
    for hh in range(C_HEADS):
        sl = slice(hh * LANE, (hh + 1) * LANE)
        oh = o[:, sl]
        mean = jnp.sum(oh, axis=-1, keepdims=True) * (1.0 / C_HEAD)
        dev = jnp.where(real, oh - mean, 0.0)
        var = jnp.sum(dev * dev, axis=-1, keepdims=True) * (1.0 / C_HEAD)
        y = dev * lax.rsqrt(var + C_GN_EPS) * gng_ref[:, sl] + gnb_ref[:, sl]
        ycs.append(((y + bonus_ref[:, sl]) * g_ref[:, sl]).astype(BF16))
    yc = jnp.concatenate(ycs, axis=1)
    gate = gate_ref[...].astype(F32)
    merged = (gate[:, :D_MODEL] * _dot(ya_ref[...], wa_ref[...])
              + gate[:, D_MODEL:2 * D_MODEL] * _dot(yb_ref[...], wb_ref[...])
              + gate[:, 2 * D_MODEL:] * _dot(yc, wc_ref[...]))
    xn = x_ref[...] + mod[2:3] * _dot(merged.astype(BF16), wo_ref[...])
    xn_out[...] = xn
    h2 = _normmod(xn, n2_ref[...], mod, 3).astype(BF16)
    h2_out[...] = h2

    scores = _sigmoid(_dot(h2, rw_ref[...]))
    neg = -jnp.inf
    sel = jnp.where(lane < N_EXPERTS, scores + rb_ref[...], neg)

    lane_f = lane.astype(F32)

    def top1(s):
        mx = jnp.max(s, axis=-1, keepdims=True)
        idx = jnp.min(jnp.where(s == mx, lane_f, float(LANE)), axis=-1, keepdims=True)
        return mx, idx.astype(jnp.int32)

    best = None
    for g in range(N_GROUPS):
        in_g = (lane >= g * EXPERTS_PER_GROUP) & (lane < (g + 1) * EXPERTS_PER_GROUP)
        s = jnp.where(in_g, sel, neg)
        m1, i1 = top1(s)
        m2, _ = top1(jnp.where(lane == i1, neg, s))
        score = m1 + m2
        if best is None:
            best, gidx = score, jnp.zeros_like(i1)
        else:
            better = score > best
            gidx = jnp.where(better, g, gidx)
            best = jnp.where(better, score, best)
    lo = gidx * EXPERTS_PER_GROUP
    s = jnp.where((lane >= lo) & (lane < lo + EXPERTS_PER_GROUP), sel, neg)
    _, i1 = top1(s)
    _, i2 = top1(jnp.where(lane == i1, neg, s))
    picked = jnp.where((lane == i1) | (lane == i2), scores, 0.0)
    comb_out[...] = picked / jnp.sum(picked, axis=-1, keepdims=True)


def _moe_kernel(h2_ref, comb_ref, xn_ref, m0_ref, m1_ref, m2_ref, w1_ref, w3_ref, w2_ref,
                fg_ref, out_ref, acc_ref, *, last):
    e = pl.program_id(1)

    @pl.when(e == 0)
    def _():
        acc_ref[...] = jnp.zeros_like(acc_ref)

    t = h2_ref[...]
    a = _dot(t, w1_ref[0])
    he = (a * _sigmoid(a) * _dot(t, w3_ref[0])).astype(BF16)
    lane = lax.broadcasted_iota(jnp.int32, (1, LANE), 1)
    ce = jnp.sum(jnp.where(lane == e, comb_ref[...], 0.0), axis=-1, keepdims=True)
    acc_ref[...] += ce * _dot(he, w2_ref[0])

    @pl.when(e == N_EXPERTS - 1)
    def _():
        for s, m_ref in enumerate((m0_ref, m1_ref, m2_ref)):
            rows = slice(s * TM, (s + 1) * TM)
            y = xn_ref[rows] + m_ref[0][5:6] * acc_ref[rows]
            if last:
                y = _rms(y) * fg_ref[...]
            out_ref[rows] = y


def _head_pad_cols(w, width):
    lead = w.shape[:-1]
    w = w.reshape(lead + (C_HEADS, width))
    w = jnp.pad(w, [(0, 0)] * len(lead) + [(0, 0), (0, LANE - width)])
    return w.reshape(lead + (HP,))


def _head_pad_rows(w, width):
    n = w.shape[-1]
    w = w.reshape(C_HEADS, width, n)
    w = jnp.pad(w, ((0, 0), (0, LANE - width), (0, 0)))
    return w.reshape(HP, n)


def _split_rope(w):
    lead = w.shape[:-1]
    w = w.reshape(lead + (2, 2, ROPE_FREQS))
    x1 = w[..., 0, :].reshape(lead + (2 * ROPE_FREQS,))
    x2 = w[..., 1, :].reshape(lead + (2 * ROPE_FREQS,))
    return x1, x2


def _rope_tables(n_ctx, t):
    rows = t // GRID_W
    row = jnp.repeat(jnp.arange(rows, dtype=F32), GRID_W)
    col = jnp.tile(jnp.arange(GRID_W, dtype=F32), rows)
    inv = jnp.power(ROPE_BASE, -jnp.arange(ROPE_FREQS, dtype=F32) / ROPE_FREQS)
    ang = jnp.concatenate([row[:, None] * inv, col[:, None] * inv], axis=1)
    cos = jnp.concatenate([jnp.ones((n_ctx, 2 * ROPE_FREQS), F32), jnp.cos(ang)], axis=0)
    sin = jnp.concatenate([jnp.zeros((n_ctx, 2 * ROPE_FREQS), F32), jnp.sin(ang)], axis=0)
    n = n_ctx + t
    one = jnp.ones((n, B_NOPE), F32)
    zero = lambda w: jnp.zeros((n, w), F32)
    cq = jnp.concatenate([one, cos, cos, zero(LANE - B_NOPE - B_ROPE)], axis=1) * MLA_SCALE
    sq = jnp.concatenate([zero(B_NOPE), -sin, sin, zero(LANE - B_NOPE - B_ROPE)], axis=1) * MLA_SCALE
    ck = jnp.concatenate([cos, cos, zero(LANE - B_ROPE)], axis=1)
    sk = jnp.concatenate([-sin, sin, zero(LANE - B_ROPE)], axis=1)
    return cq, sq, ck, sk


def _layer_weights(p, l):
    w = {}
    w_in = p['w_in'][l]
    d = w_in.shape[0]
    o = 0
    w_q = w_in[:, o:o + B_Q_LORA]; o += B_Q_LORA
    w_kv = w_in[:, o:o + B_KV_LORA]; o += B_KV_LORA
    w_kr = w_in[:, o:o + B_ROPE]; o += B_ROPE
    w_c = w_in[:, o:o + C_COLS]; o += C_COLS
    w_a = w_in[:, o:o + A_COLS]; o += A_COLS
    w_g = w_in[:, o:o + G_COLS]

    k1, k2 = _split_rope(w_kr)
    zpad = jnp.zeros((d, LANE - B_ROPE), F32)
    w['win_b'] = jnp.concatenate([w_q, w_kv, k1, k2, zpad, k2, k1, zpad], axis=1).astype(BF16)

    wuq = p['b_w_uq'][l].reshape(B_Q_LORA, B_HEADS, B_NOPE + B_ROPE)
    q1, q2 = _split_rope(wuq[..., B_NOPE:])
    qz = jnp.zeros((B_Q_LORA, B_HEADS, LANE - B_NOPE - B_ROPE), F32)
    w['wq'] = jnp.concatenate([wuq[..., :B_NOPE], q1, q2, qz], axis=-1).reshape(B_Q_LORA, HP).astype(BF16)
    w['wqs'] = jnp.concatenate([wuq[..., :B_NOPE], q2, q1, qz], axis=-1).reshape(B_Q_LORA, HP).astype(BF16)
    wukv = p['b_w_ukv'][l].reshape(B_KV_LORA, B_HEADS, B_NOPE + B_VDIM)
    w['wk'] = _head_pad_cols(wukv[..., :B_NOPE].reshape(B_KV_LORA, -1), B_NOPE).astype(BF16)
    w['wv'] = _head_pad_cols(wukv[..., B_NOPE:].reshape(B_KV_LORA, -1), B_VDIM).astype(BF16)
    place = jnp.zeros((LANE, B_HEADS, LANE), F32)
    idx = jnp.arange(B_ROPE)
    place = place.at[idx, :, B_NOPE + idx].set(1.0)
    w['place'] = place.reshape(LANE, HP).astype(BF16)
    w['qg'] = p['b_q_norm'][l][None]
    w['kvg'] = p['b_kv_norm'][l][None]

    w['wa'] = w_a.astype(BF16)
    w['wg'] = w_g.astype(BF16)
    w['vg'] = p['a_v_gain'][l][None]
    w['ws'] = p['a_ws'][l].astype(BF16)
    w['bsb'] = jnp.broadcast_to(p['a_bs'][l].T[:, :, None], (A_CHUNK, A_GROUPS, LANE)).reshape(A_CHUNK, A_WIDTH)

    cw = C_WIDTH
    pieces = [_head_pad_cols(w_c[:, i * cw:(i + 1) * cw], C_HEAD) for i in range(3)]
    w['wc'] = jnp.concatenate(pieces + [w_c[:, 3 * cw:]], axis=1).astype(BF16)
    mu = p['c_mu'][l]
    w['mu'] = jnp.concatenate([_head_pad_cols(mu[i * cw:(i + 1) * cw], C_HEAD) for i in range(3)]
                              + [mu[3 * cw:]])[None]

    def lora_block(m, rank):
        m = _head_pad_cols(m, C_HEAD)
        z = jnp.zeros_like(m[0])
        return jnp.stack([jnp.concatenate([m[0], z], axis=0), jnp.concatenate([z, m[1]], axis=0)]).astype(BF16)

    w['w2'] = lora_block(p['c_w2'][l], C_DECAY_LORA)
    w['a2'] = lora_block(p['c_a2'][l], C_AAA_LORA)
    w['w0'] = _head_pad_cols(p['c_w0'][l], C_HEAD)
    w['a0'] = _head_pad_cols(p['c_a0'][l], C_HEAD)
    w['kk'] = _head_pad_cols(p['c_k_k'][l], C_HEAD)[None]
    w['ka'] = _head_pad_cols(p['c_k_a'][l], C_HEAD)[None]
    w['rk'] = _head_pad_cols(p['c_r_k'][l].reshape(-1), C_HEAD)[None]
    w['g2'] = _head_pad_cols(p['c_g2'][l], C_HEAD).astype(BF16)
    w['gng'] = _head_pad_cols(p['c_gn_g'][l], C_HEAD)[None]
    w['gnb'] = _head_pad_cols(p['c_gn_b'][l], C_HEAD)[None]

    w['wup_a'] = p['w_up_a'][l].astype(BF16)
    w['wup_b'] = _head_pad_rows(p['w_up_b'][l], B_VDIM).astype(BF16)
    w['wup_c'] = _head_pad_rows(p['w_up_c'][l], C_HEAD).astype(BF16)
    w['wo'] = p['w_o'][l].astype(BF16)
    w['g1'] = p['norm1_g'][l][None]
    w['n2'] = p['norm2_g'][l][None]
    w['w1'] = p['moe_w1'][l].astype(BF16)
    w['w3'] = p['moe_w3'][l].astype(BF16)
    w['w2e'] = p['moe_w2'][l].astype(BF16)
    return w


def _layer(xs, mods, w, tabs, rw, rb, fg, *, bn, lb, n_ctx, last):
    m, d = xs.shape
    n_tiles = m // TM
    tpb = lb // TM
    row = lambda i: (i, 0)
    seg = lambda i: (jnp.where(i % tpb == 0, 0, 1 + i // tpb), 0, 0)
    x_spec = pl.BlockSpec((TM, d), row)
    mod_spec = pl.BlockSpec((1, 6, d), seg)
    hp_spec = pl.BlockSpec((TM, HP), row)
    hp2_spec = pl.BlockSpec((2, TM, HP), lambda i: (0, i, 0))
    hp_f32 = jax.ShapeDtypeStruct((m, HP), F32)
    hp_bf16 = jax.ShapeDtypeStruct((m, HP), BF16)
    hp2_f32 = jax.ShapeDtypeStruct((2, m, HP), F32)
    cq, sq, ck, sk = tabs
    tab_spec = pl.BlockSpec((TM, LANE), lambda i: (i % tpb, 0))

    q, k, v = pl.pallas_call(
        _mla_kernel,
        grid=(n_tiles,),
        in_specs=[x_spec, _full((1, d)), mod_spec, _full(w['win_b'].shape), _full((1, B_Q_LORA)),
                  _full((1, B_KV_LORA)), _full(w['wq'].shape), _full(w['wqs'].shape),
                  _full(w['wk'].shape), _full(w['wv'].shape), _full(w['place'].shape),
                  tab_spec, tab_spec, tab_spec, tab_spec],
        out_specs=[hp_spec, hp_spec, hp_spec],
        out_shape=[hp_bf16, hp_bf16, hp_bf16],
        compiler_params=_params(1),
        name="mla_proj",
    )(xs, w['g1'], mods, w['win_b'], w['qg'], w['kvg'], w['wq'], w['wqs'], w['wk'], w['wv'],
      w['place'], cq, sq, ck, sk)

    ya, gate = pl.pallas_call(
        _gmlp_kernel,
        grid=(n_tiles,),
        in_specs=[x_spec, _full((1, d)), mod_spec, _full(w['wa'].shape), _full(w['wg'].shape),
                  _full((1, A_WIDTH)), _full(w['ws'].shape), _full(w['bsb'].shape)],
        out_specs=[pl.BlockSpec((TM, A_WIDTH), row), pl.BlockSpec((TM, G_COLS), row)],
        out_shape=[jax.ShapeDtypeStruct((m, A_WIDTH), BF16), jax.ShapeDtypeStruct((m, G_COLS), BF16)],
        compiler_params=_params(1),
        name="gmlp_gates",
    )(xs, w['g1'], mods, w['wa'], w['wg'], w['vg'], w['ws'], w['bsb'])

    sub = TM // 8
    pe_spec = pl.BlockSpec((2, 8 * (TM // SCAN_C), HP), lambda i: (0, i, 0))
    pe_shape = jax.ShapeDtypeStruct((2, 8 * (m // SCAN_C), HP), F32)
    vv, ab, bh, kh, rbar, pe, bonus, gg = pl.pallas_call(
        functools.partial(_feat_kernel, tiles_per_batch=tpb),
        grid=(n_tiles,),
        in_specs=[x_spec,
                  pl.BlockSpec((8, d), lambda i: (jnp.maximum(i * sub - 1, 0), 0)),
                  pl.BlockSpec((8, d), lambda i: (jnp.minimum((i + 1) * sub, m // 8 - 1), 0)),
                  _full((1, d)), mod_spec, _full(w['wc'].shape), _full(w['mu'].shape),
                  _full((2, HP)), _full(w['w2'].shape), _full((2, HP)), _full(w['a2'].shape),
                  _full((1, HP)), _full((1, HP)), _full((1, HP)), _full(w['g2'].shape)],
        out_specs=[hp_spec, hp2_spec, hp2_spec, hp2_spec, hp2_spec, pe_spec, hp_spec, hp_spec],
        out_shape=[hp_f32, hp2_f32, hp2_f32, hp2_f32, hp2_f32, pe_shape, hp_f32, hp_f32],
        compiler_params=_params(1),
        name="rwkv_features",
    )(xs, xs, xs, w['g1'], mods, w['wc'], w['mu'], w['w0'], w['w2'], w['a0'], w['a2'],
      w['kk'], w['ka'], w['rk'], w['g2'])

    nc = lb // SCAN_C
    nctx_c = n_ctx // SCAN_C

    def chunk_of(dd, b, j):
        if dd == 0:
            return b * nc + j
        return b * nc + jnp.where(j < nctx_c, nctx_c - 1 - j, nc - 1 + nctx_c - j)

    def tok_spec(dd):
        return pl.BlockSpec((SCAN_C, HP), lambda b, j: (chunk_of(dd, b, j), 0))

    def dir_spec(dd, rows=SCAN_C):
        return pl.BlockSpec((1, rows, HP), lambda b, j: (dd, chunk_of(dd, b, j), 0))

    o_f, o_b = pl.pallas_call(
        _scan_kernel,
        grid=(bn, nc),
        in_specs=[tok_spec(0), tok_spec(1)] + [dir_spec(dd) for _ in range(4) for dd in range(2)]
                 + [dir_spec(0, 8), dir_spec(1, 8)],
        out_specs=[tok_spec(0), tok_spec(1)],
        out_shape=[hp_f32, hp_f32],
        scratch_shapes=[pltpu.VMEM((2, C_HEADS, LANE, LANE), F32)],
        compiler_params=_params(2),
        name="rwkv_scan",
    )(vv, vv, ab, ab, bh, bh, kh, kh, rbar, rbar, pe, pe)

    q3, k3, v3 = (t.reshape(bn, lb, HP) for t in (q, k, v))
    k_chunk = lb // 3
    yb = pl.pallas_call(
        functools.partial(_attn_kernel, n_ctx=n_ctx, k_chunk=k_chunk),
        grid=(bn, B_HEADS, tpb),
        in_specs=[pl.BlockSpec((1, TM, LANE), lambda b, hh, qi: (b, qi, hh)),
                  pl.BlockSpec((1, lb, LANE), lambda b, hh, qi: (b, 0, hh)),
                  pl.BlockSpec((1, lb, LANE), lambda b, hh, qi: (b, 0, hh))],
        out_specs=pl.BlockSpec((1, TM, LANE), lambda b, hh, qi: (b, qi, hh)),
        out_shape=jax.ShapeDtypeStruct((bn, lb, HP), BF16),
        compiler_params=_params(3),
        name="mla_attention",
    )(q3, k3, v3).reshape(m, HP)

    xn, h2, comb = pl.pallas_call(
        _merge_kernel,
        grid=(n_tiles,),
        in_specs=[hp_spec, hp_spec, hp_spec, hp_spec, pl.BlockSpec((TM, A_WIDTH), row), hp_spec,
                  pl.BlockSpec((TM, G_COLS), row), x_spec, mod_spec, _full((1, HP)), _full((1, HP)),
                  _full(w['wup_a'].shape), _full(w['wup_b'].shape), _full(w['wup_c'].shape),
                  _full(w['wo'].shape), _full((1, d)), _full(rw.shape), _full(rb.shape)],
        out_specs=[x_spec, x_spec, pl.BlockSpec((TM, LANE), row)],
        out_shape=[jax.ShapeDtypeStruct((m, d), F32), jax.ShapeDtypeStruct((m, d), BF16),
                   jax.ShapeDtypeStruct((m, LANE), F32)],
        compiler_params=_params(1),
        name="merge_router",
    )(o_f, o_b, bonus, gg, ya, yb, gate, xs, mods, w['gng'], w['gnb'], w['wup_a'], w['wup_b'],
      w['wup_c'], w['wo'], w['n2'], rw, rb)

    tm_moe = MOE_SUB * TM
    mrow = lambda i, e: (i, 0)
    mseg = lambda s: pl.BlockSpec((1, 6, d), lambda i, e: seg(i * MOE_SUB + s))
    wexp = lambda shape: pl.BlockSpec((1,) + shape, lambda i, e: (e, 0, 0))
    return pl.pallas_call(
        functools.partial(_moe_kernel, last=last),
        grid=(m // tm_moe, N_EXPERTS),
        in_specs=[pl.BlockSpec((tm_moe, d), mrow), pl.BlockSpec((tm_moe, LANE), mrow),
                  pl.BlockSpec((tm_moe, d), mrow), mseg(0), mseg(1), mseg(2),
                  wexp((d, D_EXPERT)), wexp((d, D_EXPERT)), wexp((D_EXPERT, d)),
                  pl.BlockSpec((1, d), lambda i, e: (0, 0))],
        out_specs=pl.BlockSpec((tm_moe, d), mrow),
        out_shape=jax.ShapeDtypeStruct((m, d), F32),
        scratch_shapes=[pltpu.VMEM((tm_moe, d), F32)],
        compiler_params=_params(2),
        name="moe",
    )(h2, comb, xn, mods, mods, mods, w['w1'], w['w3'], w['w2e'], fg)


def kernel(x, c, ctx, c_ctx, mod_w, mod_b, norm1_g, norm2_g, w_in, a_v_gain, a_ws, a_bs, b_q_norm, b_w_uq, b_kv_norm, b_w_ukv, c_mu, c_w0, c_w2, c_a0, c_a2, c_g2, c_k_k, c_k_a, c_r_k, c_gn_g, c_gn_b, w_up_a, w_up_b, w_up_c, w_o, router_w, router_b, moe_w1, moe_w3, moe_w2, final_g):
    p = dict(w_in=w_in, a_v_gain=a_v_gain, a_ws=a_ws, a_bs=a_bs, b_q_norm=b_q_norm, b_w_uq=b_w_uq,
             b_kv_norm=b_kv_norm, b_w_ukv=b_w_ukv, c_mu=c_mu, c_w0=c_w0, c_w2=c_w2, c_a0=c_a0,
             c_a2=c_a2, c_g2=c_g2, c_k_k=c_k_k, c_k_a=c_k_a, c_r_k=c_r_k, c_gn_g=c_gn_g,
             c_gn_b=c_gn_b, w_up_a=w_up_a, w_up_b=w_up_b, w_up_c=w_up_c, w_o=w_o,
             norm1_g=norm1_g, norm2_g=norm2_g, moe_w1=moe_w1, moe_w3=moe_w3, moe_w2=moe_w2)
    bn, t, d = x.shape
    n_ctx = ctx.shape[1]
    lb = n_ctx + t
    depth = mod_w.shape[0]
    assert bn == 2 and n_ctx == TM and lb % (MOE_SUB * TM) == 0 and lb % SCAN_C == 0

    cvec = jnp.concatenate([c_ctx[None], c, jnp.zeros((8 - 1 - bn, d), F32)], axis=0)
    mods = _modulation(cvec, mod_w, mod_b).reshape(depth, 8, 6, d)

    tabs = _rope_tables(n_ctx, t)
    rw = jnp.pad(router_w, ((0, 0), (0, LANE - N_EXPERTS))).astype(BF16)
    rb = jnp.pad(router_b, (0, LANE - N_EXPERTS))[None]
    fg = final_g[None]

    xs = jnp.concatenate([ctx, x], axis=1).reshape(bn * lb, d)
    for l in range(depth):
        xs = _layer(xs, mods[l], _layer_weights(p, l), tabs, rw, rb, fg,
                    bn=bn, lb=lb, n_ctx=n_ctx, last=(l == depth - 1))
    return xs.reshape(bn, lb, d)[:, n_ctx:]
```

```python
import functools
import math

import jax
import jax.numpy as jnp
from jax import lax
from jax.experimental import pallas as pl
from jax.experimental.pallas import tpu as pltpu

F32 = jnp.float32
BF16 = jnp.bfloat16
HIGHEST = lax.Precision.HIGHEST

D_MODEL = 1024
GRID_W = 64
EPS = 1e-6

A_WIDTH = 512
A_GROUPS = 4
A_CHUNK = 128

B_HEADS = 8
B_Q_LORA = 384
B_KV_LORA = 256
B_NOPE = 64
B_ROPE = 32
B_VDIM = 64
ROPE_BASE = 10000.0
ROPE_FREQS = B_ROPE // 4
MLA_SCALE = (B_NOPE + B_ROPE) ** -0.5

C_HEADS = 8
C_HEAD = 64
C_WIDTH = C_HEADS * C_HEAD
C_DECAY_LORA = 64
C_AAA_LORA = 64
C_GATE_LORA = 128
C_GN_EPS = 64e-5
DECAY_SCALE = 0.6065306597126334

B_COLS = B_Q_LORA + B_KV_LORA + B_ROPE
C_COLS = 3 * C_WIDTH + 2 * C_DECAY_LORA + 2 * C_AAA_LORA + C_GATE_LORA
A_COLS = 2 * A_WIDTH
G_COLS = 3 * D_MODEL

N_EXPERTS = 16
N_GROUPS = 4
EXPERTS_PER_GROUP = 4
D_EXPERT = 512

LANE = 128
HP = C_HEADS * LANE
TM = 256
SCAN_C = 64
MOE_SUB = 3
ATTN_KC = 768
VMEM_LIMIT = 52 * 1024 * 1024


def _dot(a, b, precision=None):
    return jnp.dot(a, b, preferred_element_type=F32, precision=precision)


def _dot_nt(a, b, precision=None):
    return lax.dot_general(a, b, (((1,), (1,)), ((), ())),
                           preferred_element_type=F32, precision=precision)


def _dot_tn(a, b, precision=None):
    return lax.dot_general(a, b, (((0,), (0,)), ((), ())),
                           preferred_element_type=F32, precision=precision)


def _rms(x, eps=EPS):
    return x * lax.rsqrt(jnp.mean(x * x, axis=-1, keepdims=True) + eps)


def _normmod(x, g, mod, first):
    return _rms(x) * g * (1.0 + mod[first + 1:first + 2]) + mod[first:first + 1]


def _sigmoid(x):
    return 1.0 / (1.0 + jnp.exp(-x))


def _params(n_grid):
    return pltpu.CompilerParams(
        dimension_semantics=("arbitrary",) * n_grid, vmem_limit_bytes=VMEM_LIMIT)


def _full(shape):
    n = len(shape)
    return pl.BlockSpec(shape, lambda *_: (0,) * n)


def _mod_kernel(c_ref, w_ref, b_ref, o_ref):
    c = c_ref[...]
    act = (c * _sigmoid(c)).astype(BF16)
    o_ref[0] = _dot(act, w_ref[0].astype(BF16)) + b_ref[0]


def _modulation(cvec, mod_w, mod_b):
    depth, d, n = mod_w.shape
    tn = 1536
    return pl.pallas_call(
        _mod_kernel,
        grid=(depth, n // tn),
        in_specs=[
            pl.BlockSpec((8, d), lambda l, j: (0, 0)),
            pl.BlockSpec((1, d, tn), lambda l, j: (l, 0, j)),
            pl.BlockSpec((1, 1, tn), lambda l, j: (l, 0, j)),
        ],
        out_specs=pl.BlockSpec((1, 8, tn), lambda l, j: (l, 0, j)),
        out_shape=jax.ShapeDtypeStruct((depth, 8, n), F32),
        compiler_params=_params(2),
        name="modulation",
    )(cvec, mod_w, mod_b.reshape(depth, 1, n))


def _mla_kernel(x_ref, g1_ref, mod_ref, win_ref, qg_ref, kvg_ref, wq_ref, wqs_ref,
                wk_ref, wv_ref, place_ref, cq_ref, sq_ref, ck_ref, sk_ref,
                q_out, k_out, v_out):
    h = _normmod(x_ref[...], g1_ref[...], mod_ref[0], 0).astype(BF16)
    z = _dot(h, win_ref[...])
    zq = z[:, :B_Q_LORA]
    zkv = z[:, B_Q_LORA:B_Q_LORA + B_KV_LORA]
    kr_a = z[:, B_Q_LORA + B_KV_LORA:B_Q_LORA + B_KV_LORA + LANE]
    kr_b = z[:, B_Q_LORA + B_KV_LORA + LANE:]
    qn = (_rms(zq) * qg_ref[...]).astype(BF16)
    kvn = (_rms(zkv) * kvg_ref[...]).astype(BF16)
    q1 = _dot(qn, wq_ref[...])
    q2 = _dot(qn, wqs_ref[...])
    kr = (kr_a * ck_ref[...] + kr_b * sk_ref[...]).astype(BF16)
    k = _dot(kvn, wk_ref[...]) + _dot(kr, place_ref[...])
    cq = cq_ref[...]
    sq = sq_ref[...]
    for hh in range(B_HEADS):
        sl = slice(hh * LANE, (hh + 1) * LANE)
        q_out[:, sl] = (q1[:, sl] * cq + q2[:, sl] * sq).astype(BF16)
    k_out[...] = k.astype(BF16)
    v_out[...] = _dot(kvn, wv_ref[...]).astype(BF16)


def _gelu_tanh(x):
    return 0.5 * x * (1.0 + jnp.tanh(math.sqrt(2.0 / math.pi) * (x + 0.044715 * (x * x * x))))


def _gmlp_kernel(x_ref, g1_ref, mod_ref, wa_ref, wg_ref, vg_ref, ws_ref, bsb_ref,
                 ya_out, gate_out):
    h = _normmod(x_ref[...], g1_ref[...], mod_ref[0], 0).astype(BF16)
    gate_out[...] = _sigmoid(_dot(h, wg_ref[...])).astype(BF16)
    zg = _gelu_tanh(_dot(h, wa_ref[...]))
    u = zg[:, :A_WIDTH]
    v = (_rms(zg[:, A_WIDTH:]) * vg_ref[...]).astype(BF16)
    for c in range(TM // A_CHUNK):
        rows = slice(c * A_CHUNK, (c + 1) * A_CHUNK)
        for g in range(A_GROUPS):
            cols = slice(g * LANE, (g + 1) * LANE)
            mixed = _dot(ws_ref[g], v[rows, cols]) + bsb_ref[:, cols]
            ya_out[rows, cols] = (u[rows, cols] * mixed).astype(BF16)


def _feat_kernel(x_ref, xp_ref, xn_ref, g1_ref, mod_ref, wc_ref, mu_ref,
                 w0_ref, w2_ref, a0_ref, a2_ref, kk_ref, ka_ref, rk_ref, g2_ref,
                 v_out, ab_out, bh_out, kh_out, rb_out, pe_out, bonus_out, g_out,
                 *, tiles_per_batch):
    i = pl.program_id(0)
    t = i % tiles_per_batch
    prev_on = jnp.where((t == 0) | (t == 1), 0.0, 1.0)
    next_on = jnp.where((t == 0) | (t == tiles_per_batch - 1), 0.0, 1.0)
    mod = mod_ref[0]
    g1 = g1_ref[...]
    wc = wc_ref[...]
    z = _dot(_normmod(x_ref[...], g1, mod, 0).astype(BF16), wc)
    zp = _dot(_normmod(xp_ref[...], g1, mod, 0).astype(BF16), wc)[7:8] * prev_on
    zn = _dot(_normmod(xn_ref[...], g1, mod, 0).astype(BF16), wc)[0:1] * next_on
    row = lax.broadcasted_iota(jnp.int32, (TM, 1), 0)
    z_up = jnp.where(row == 0, zp, pltpu.roll(z, 1, 0))
    z_dn = jnp.where(row == TM - 1, zn, pltpu.roll(z, TM - 1, 0))
    z = z + mu_ref[...] * (0.5 * (z_up + z_dn) - z)

    r = z[:, 0:HP]
    k = z[:, HP:2 * HP]
    v = z[:, 2 * HP:3 * HP]
    zw = jnp.tanh(z[:, 3 * HP:3 * HP + LANE]).astype(BF16)
    za = z[:, 3 * HP + LANE:3 * HP + 2 * LANE].astype(BF16)
    zg = _sigmoid(z[:, 3 * HP + 2 * LANE:]).astype(BF16)

    kk = k * kk_ref[...]
    ka = ka_ref[...]
    rk = rk_ref[...]
    a = [_sigmoid(a0_ref[d:d + 1] + _dot(za, a2_ref[d])) for d in range(2)]
    kd = [k * (1.0 + (a[d] - 1.0) * ka) for d in range(2)]
    v_out[...] = v.astype(BF16)
    g_out[...] = _dot(zg, g2_ref[...])

    ti = lax.broadcasted_iota(jnp.int32, (TM, TM), 0)
    tj = lax.broadcasted_iota(jnp.int32, (TM, TM), 1)
    shift = SCAN_C.bit_length() - 1
    same = lax.shift_right_logical(ti, shift) == lax.shift_right_logical(tj, shift)
    p_in, p_prev, p_inv = [], [], []
    for d in range(2):
        lw = -DECAY_SCALE * _sigmoid(w0_ref[d:d + 1] + _dot(zw, w2_ref[d]))
        tri = jnp.where(same & ((tj <= ti) if d == 0 else (tj >= ti)), 1.0, 0.0).astype(F32)
        cs = _dot(tri, lw, HIGHEST)
        p_in.append(jnp.exp(cs))
        p_prev.append(jnp.exp(cs - lw))
        p_inv.append(jnp.exp(-cs))
        kh_out[d] = (kd[d] * p_inv[d]).astype(BF16)
        rb_out[d] = (r * p_in[d]).astype(BF16)
        for cc in range(TM // SCAN_C):
            last = cc * SCAN_C + (SCAN_C - 1 if d == 0 else 0)
            pe_out[d, cc * 8:(cc + 1) * 8, :] = jnp.broadcast_to(p_in[d][last:last + 1], (8, HP))

    for hh in range(C_HEADS):
        sl = slice(hh * LANE, (hh + 1) * LANE)
        kh = kk[:, sl]
        kh = kh * lax.rsqrt(jnp.maximum(jnp.sum(kh * kh, axis=-1, keepdims=True), 1e-12))
        rh = r[:, sl] * rk[:, sl]
        bon = jnp.zeros((TM, 1), F32)
        for d in range(2):
            ab_out[d, :, sl] = (-kh * p_prev[d][:, sl]).astype(BF16)
            bh_out[d, :, sl] = (kh * a[d][:, sl] * p_inv[d][:, sl]).astype(BF16)
            bon = bon + jnp.sum(rh * kd[d][:, sl], axis=-1, keepdims=True)
        bonus_out[:, sl] = bon * v[:, sl]


def _mm(a, b, dims=((1,), (0,))):
    return lax.dot_general(a.astype(BF16), b.astype(BF16), (dims, ((), ())),
                           preferred_element_type=F32)


_NT = ((1,), (1,))
_TN = ((0,), (0,))


def _scan_kernel(vf_ref, vb_ref, abf_ref, abb_ref, bhf_ref, bhb_ref, khf_ref, khb_ref,
                 rbf_ref, rbb_ref, pef_ref, peb_ref, of_ref, ob_ref, s_ref):
    @pl.when(pl.program_id(1) == 0)
    def _():
        s_ref[...] = jnp.zeros_like(s_ref)

    c = SCAN_C
    row = lax.broadcasted_iota(jnp.int32, (c, c), 0)
    col = lax.broadcasted_iota(jnp.int32, (c, c), 1)
    refs = ((vf_ref, abf_ref, bhf_ref, khf_ref, rbf_ref, pef_ref, of_ref),
            (vb_ref, abb_ref, bhb_ref, khb_ref, rbb_ref, peb_ref, ob_ref))

    chains = [(d, hh) for d in range(2) for hh in range(C_HEADS)]

    def sl(hh):
        return slice(hh * LANE, (hh + 1) * LANE)

    al = lambda ch: refs[ch[0]][1][0, :, sl(ch[1])]
    bh = lambda ch: refs[ch[0]][2][0, :, sl(ch[1])]
    kh = lambda ch: refs[ch[0]][3][0, :, sl(ch[1])]
    rb = lambda ch: refs[ch[0]][4][0, :, sl(ch[1])]
    vv = lambda ch: refs[ch[0]][0][:, sl(ch[1])]
    pe = lambda ch: refs[ch[0]][5][0, 0:1, sl(ch[1])]
    strict = lambda ch: (col < row) if ch[0] == 0 else (col > row)
    incl = lambda ch: (col <= row) if ch[0] == 0 else (col >= row)

    xb = {ch: _mm(jnp.concatenate([al(ch), rb(ch)], axis=0), bh(ch), _NT) for ch in chains}
    xk = {ch: _mm(jnp.concatenate([al(ch), rb(ch)], axis=0), kh(ch), _NT) for ch in chains}
    l_pow = {ch: jnp.where(strict(ch), xb[ch][:c], 0.0) for ch in chains}
    u = {ch: _mm(jnp.where(strict(ch), xk[ch][:c], 0.0), vv(ch)) + _mm(al(ch), s_ref[ch], _NT)
         for ch in chains}
    for it in range(6):
        u = {ch: u[ch] + _mm(l_pow[ch], u[ch]) for ch in chains}
        if it < 5:
            l_pow = {ch: _mm(l_pow[ch], l_pow[ch]) for ch in chains}
    for ch in chains:
        o_ref = refs[ch[0]][6]
        o_ref[:, sl(ch[1])] = (_mm(rb(ch), s_ref[ch], _NT)
                               + _mm(jnp.where(incl(ch), xb[ch][c:], 0.0), u[ch])
                               + _mm(jnp.where(incl(ch), xk[ch][c:], 0.0), vv(ch)))
    for ch in chains:
        upd = _mm(jnp.concatenate([u[ch].astype(BF16), vv(ch)], axis=0),
                  jnp.concatenate([bh(ch), kh(ch)], axis=0), _TN)
        s_ref[ch] = (s_ref[ch] + upd) * pe(ch)


def _attn_kernel(q_ref, k_ref, v_ref, o_ref, *, n_ctx, k_chunk):
    qi = pl.program_id(2)
    q = q_ref[0]

    @pl.when(qi == 0)
    def _():
        s = _dot_nt(q, k_ref[0, :n_ctx, :])
        p = jnp.exp2(s - jnp.max(s, axis=-1, keepdims=True))
        l = jnp.sum(p, axis=-1, keepdims=True)
        o_ref[0] = (_dot(p.astype(BF16), v_ref[0, :n_ctx, :]) / l).astype(BF16)

    @pl.when(qi > 0)
    def _():
        starts = list(range(0, k_ref.shape[1], k_chunk))
        scores = lambda c0: _dot_nt(q, k_ref[0, c0:c0 + k_chunk, :])
        m = jnp.full((q.shape[0], 1), -jnp.inf, F32)
        l = jnp.zeros((q.shape[0], 1), F32)
        acc = jnp.zeros((q.shape[0], LANE), F32)
        s_next = scores(starts[0])
        for n, c0 in enumerate(starts):
            s = s_next
            if n + 1 < len(starts):
                s_next = scores(starts[n + 1])
            m_new = jnp.maximum(m, jnp.max(s, axis=-1, keepdims=True))
            p = jnp.exp2(s - m_new)
            corr = jnp.exp2(m - m_new)
            l = l * corr + jnp.sum(p, axis=-1, keepdims=True)
            acc = acc * corr + _dot(p.astype(BF16), v_ref[0, c0:c0 + k_chunk, :])
            m = m_new
        o_ref[0] = (acc / l).astype(BF16)


def _merge_kernel(of_ref, ob_ref, bonus_ref, g_ref, ya_ref, yb_ref, gate_ref, x_ref,
                  mod_ref, gng_ref, gnb_ref, wa_ref, wb_ref, wc_ref, wo_ref, n2_ref,
                  rw_ref, rb_ref, xn_out, h2_out, comb_out):
    mod = mod_ref[0]
    o = of_ref[...] + ob_ref[...]
    lane = lax.broadcasted_iota(jnp.int32, (1, LANE), 1)
    real = lane < C_HEAD
    ycs = []
    for hh in range(C_HEADS):
        sl = slice(hh * LANE, (hh + 1) * LANE)
        oh = o[:, sl]
        mean = jnp.sum(oh, axis=-1, keepdims=True) * (1.0 / C_HEAD)
        dev = jnp.where(real, oh - mean, 0.0)
        var = jnp.sum(dev * dev, axis=-1, keepdims=True) * (1.0 / C_HEAD)
        y = dev * lax.rsqrt(var + C_GN_EPS) * gng_ref[:, sl] + gnb_ref[:, sl]
        ycs.append(((y + bonus_ref[:, sl]) * g_ref[:, sl]).astype(BF16))
    yc = jnp.concatenate(ycs, axis=1)
    gate = gate_ref[...].astype(F32)
    merged = (gate[:, :D_MODEL] * _dot(ya_ref[...], wa_ref[...])
              + gate[:, D_MODEL:2 * D_MODEL] * _dot(yb_ref[...], wb_ref[...])
              + gate[:, 2 * D_MODEL:] * _dot(yc, wc_ref[...]))
    xn = x_ref[...] + mod[2:3] * _dot(merged.astype(BF16), wo_ref[...])
    xn_out[...] = xn
    h2 = _normmod(xn, n2_ref[...], mod, 3).astype(BF16)
    h2_out[...] = h2

    scores = _sigmoid(_dot(h2, rw_ref[...]))
    neg = -jnp.inf
    sel = jnp.where(lane < N_EXPERTS, scores + rb_ref[...], neg)

    lane_f = lane.astype(F32)

    def top1(s):
        mx = jnp.max(s, axis=-1, keepdims=True)
        idx = jnp.min(jnp.where(s == mx, lane_f, float(LANE)), axis=-1, keepdims=True)
        return mx, idx.astype(jnp.int32)

    best = None
    for g in range(N_GROUPS):
        in_g = (lane >= g * EXPERTS_PER_GROUP) & (lane < (g + 1) * EXPERTS_PER_GROUP)
        s = jnp.where(in_g, sel, neg)
        m1, i1 = top1(s)
        m2, _ = top1(jnp.where(lane == i1, neg, s))
        score = m1 + m2
        if best is None:
            best, gidx = score, jnp.zeros_like(i1)
        else:
            better = score > best
            gidx = jnp.where(better, g, gidx)
            best = jnp.where(better, score, best)
    lo = gidx * EXPERTS_PER_GROUP
    s = jnp.where((lane >= lo) & (lane < lo + EXPERTS_PER_GROUP), sel, neg)
    _, i1 = top1(s)
    _, i2 = top1(jnp.where(lane == i1, neg, s))
    picked = jnp.where((lane == i1) | (lane == i2), scores, 0.0)
    comb_out[...] = picked / jnp.sum(picked, axis=-1, keepdims=True)


def _moe_kernel(h2_ref, comb_ref, xn_ref, m0_ref, m1_ref, m2_ref, w1_ref, w3_ref, w2_ref,
                fg_ref, out_ref, acc_ref, *, last):
    e = pl.program_id(1)

    @pl.when(e == 0)
    def _():
        acc_ref[...] = jnp.zeros_like(acc_ref)

    t = h2_ref[...]
    a = _dot(t, w1_ref[0])
    he = (a * _sigmoid(a) * _dot(t, w3_ref[0])).astype(BF16)
    lane = lax.broadcasted_iota(jnp.int32, (1, LANE), 1)
    ce = jnp.sum(jnp.where(lane == e, comb_ref[...], 0.0), axis=-1, keepdims=True)
    acc_ref[...] += ce * _dot(he, w2_ref[0])

    @pl.when(e == N_EXPERTS - 1)
    def _():
        for s, m_ref in enumerate((m0_ref, m1_ref, m2_ref)):
            rows = slice(s * TM, (s + 1) * TM)
            y = xn_ref[rows] + m_ref[0][5:6] * acc_ref[rows]
            if last:
                y = _rms(y) * fg_ref[...]
            out_ref[rows] = y


def _head_pad_cols(w, width):
    lead = w.shape[:-1]
    w = w.reshape(lead + (C_HEADS, width))
    w = jnp.pad(w, [(0, 0)] * len(lead) + [(0, 0), (0, LANE - width)])
    return w.reshape(lead + (HP,))


def _head_pad_rows(w, width):
    n = w.shape[-1]
    w = w.reshape(C_HEADS, width, n)
    w = jnp.pad(w, ((0, 0), (0, LANE - width), (0, 0)))
    return w.reshape(HP, n)


def _split_rope(w):
    lead = w.shape[:-1]
    w = w.reshape(lead + (2, 2, ROPE_FREQS))
    x1 = w[..., 0, :].reshape(lead + (2 * ROPE_FREQS,))
    x2 = w[..., 1, :].reshape(lead + (2 * ROPE_FREQS,))
    return x1, x2


def _rope_tables(n_ctx, t):
    rows = t // GRID_W
    row = jnp.repeat(jnp.arange(rows, dtype=F32), GRID_W)
    col = jnp.tile(jnp.arange(GRID_W, dtype=F32), rows)
    inv = jnp.power(ROPE_BASE, -jnp.arange(ROPE_FREQS, dtype=F32) / ROPE_FREQS)
    ang = jnp.concatenate([row[:, None] * inv, col[:, None] * inv], axis=1)
    cos = jnp.concatenate([jnp.ones((n_ctx, 2 * ROPE_FREQS), F32), jnp.cos(ang)], axis=0)
    sin = jnp.concatenate([jnp.zeros((n_ctx, 2 * ROPE_FREQS), F32), jnp.sin(ang)], axis=0)
    n = n_ctx + t
    one = jnp.ones((n, B_NOPE), F32)
    zero = lambda w: jnp.zeros((n, w), F32)
    qs = MLA_SCALE * math.log2(math.e)
    cq = jnp.concatenate([one, cos, cos, zero(LANE - B_NOPE - B_ROPE)], axis=1) * qs
    sq = jnp.concatenate([zero(B_NOPE), -sin, sin, zero(LANE - B_NOPE - B_ROPE)], axis=1) * qs
    ck = jnp.concatenate([cos, cos, zero(LANE - B_ROPE)], axis=1)
    sk = jnp.concatenate([-sin, sin, zero(LANE - B_ROPE)], axis=1)
    return cq, sq, ck, sk


def _layer_weights(p, l):
    w = {}
    w_in = p['w_in'][l]
    d = w_in.shape[0]
    o = 0
    w_q = w_in[:, o:o + B_Q_LORA]; o += B_Q_LORA
    w_kv = w_in[:, o:o + B_KV_LORA]; o += B_KV_LORA
    w_kr = w_in[:, o:o + B_ROPE]; o += B_ROPE
    w_c = w_in[:, o:o + C_COLS]; o += C_COLS
    w_a = w_in[:, o:o + A_COLS]; o += A_COLS
    w_g = w_in[:, o:o + G_COLS]

    k1, k2 = _split_rope(w_kr)
    zpad = jnp.zeros((d, LANE - B_ROPE), F32)
    w['win_b'] = jnp.concatenate([w_q, w_kv, k1, k2, zpad, k2, k1, zpad], axis=1).astype(BF16)

    wuq = p['b_w_uq'][l].reshape(B_Q_LORA, B_HEADS, B_NOPE + B_ROPE)
    q1, q2 = _split_rope(wuq[..., B_NOPE:])
    qz = jnp.zeros((B_Q_LORA, B_HEADS, LANE - B_NOPE - B_ROPE), F32)
    w['wq'] = jnp.concatenate([wuq[..., :B_NOPE], q1, q2, qz], axis=-1).reshape(B_Q_LORA, HP).astype(BF16)
    w['wqs'] = jnp.concatenate([wuq[..., :B_NOPE], q2, q1, qz], axis=-1).reshape(B_Q_LORA, HP).astype(BF16)
    wukv = p['b_w_ukv'][l].reshape(B_KV_LORA, B_HEADS, B_NOPE + B_VDIM)
    w['wk'] = _head_pad_cols(wukv[..., :B_NOPE].reshape(B_KV_LORA, -1), B_NOPE).astype(BF16)
    w['wv'] = _head_pad_cols(wukv[..., B_NOPE:].reshape(B_KV_LORA, -1), B_VDIM).astype(BF16)
    place = jnp.zeros((LANE, B_HEADS, LANE), F32)
    idx = jnp.arange(B_ROPE)
    place = place.at[idx, :, B_NOPE + idx].set(1.0)
    w['place'] = place.reshape(LANE, HP).astype(BF16)
    w['qg'] = p['b_q_norm'][l][None]
    w['kvg'] = p['b_kv_norm'][l][None]

    w['wa'] = w_a.astype(BF16)
    w['wg'] = w_g.astype(BF16)
    w['vg'] = p['a_v_gain'][l][None]
    w['ws'] = p['a_ws'][l].astype(BF16)
    w['bsb'] = jnp.broadcast_to(p['a_bs'][l].T[:, :, None], (A_CHUNK, A_GROUPS, LANE)).reshape(A_CHUNK, A_WIDTH)

    cw = C_WIDTH
    pieces = [_head_pad_cols(w_c[:, i * cw:(i + 1) * cw], C_HEAD) for i in range(3)]
    w['wc'] = jnp.concatenate(pieces + [w_c[:, 3 * cw:]], axis=1).astype(BF16)
    mu = p['c_mu'][l]
    w['mu'] = jnp.concatenate([_head_pad_cols(mu[i * cw:(i + 1) * cw], C_HEAD) for i in range(3)]
                              + [mu[3 * cw:]])[None]

    def lora_block(m, rank):
        m = _head_pad_cols(m, C_HEAD)
        z = jnp.zeros_like(m[0])
        return jnp.stack([jnp.concatenate([m[0], z], axis=0), jnp.concatenate([z, m[1]], axis=0)]).astype(BF16)

    w['w2'] = lora_block(p['c_w2'][l], C_DECAY_LORA)
    w['a2'] = lora_block(p['c_a2'][l], C_AAA_LORA)
    w['w0'] = _head_pad_cols(p['c_w0'][l], C_HEAD)
    w['a0'] = _head_pad_cols(p['c_a0'][l], C_HEAD)
    w['kk'] = _head_pad_cols(p['c_k_k'][l], C_HEAD)[None]
    w['ka'] = _head_pad_cols(p['c_k_a'][l], C_HEAD)[None]
    w['rk'] = _head_pad_cols(p['c_r_k'][l].reshape(-1), C_HEAD)[None]
    w['g2'] = _head_pad_cols(p['c_g2'][l], C_HEAD).astype(BF16)
    w['gng'] = _head_pad_cols(p['c_gn_g'][l], C_HEAD)[None]
    w['gnb'] = _head_pad_cols(p['c_gn_b'][l], C_HEAD)[None]

    w['wup_a'] = p['w_up_a'][l].astype(BF16)
    w['wup_b'] = _head_pad_rows(p['w_up_b'][l], B_VDIM).astype(BF16)
    w['wup_c'] = _head_pad_rows(p['w_up_c'][l], C_HEAD).astype(BF16)
    w['wo'] = p['w_o'][l].astype(BF16)
    w['g1'] = p['norm1_g'][l][None]
    w['n2'] = p['norm2_g'][l][None]
    w['w1'] = p['moe_w1'][l].astype(BF16)
    w['w3'] = p['moe_w3'][l].astype(BF16)
    w['w2e'] = p['moe_w2'][l].astype(BF16)
    return w


def _layer(xs, mods, w, tabs, rw, rb, fg, *, bn, lb, n_ctx, last):
    m, d = xs.shape
    n_tiles = m // TM
    tpb = lb // TM
    row = lambda i: (i, 0)
    seg = lambda i: (jnp.where(i % tpb == 0, 0, 1 + i // tpb), 0, 0)
    x_spec = pl.BlockSpec((TM, d), row)
    mod_spec = pl.BlockSpec((1, 6, d), seg)
    hp_spec = pl.BlockSpec((TM, HP), row)
    hp2_spec = pl.BlockSpec((2, TM, HP), lambda i: (0, i, 0))
    hp_f32 = jax.ShapeDtypeStruct((m, HP), F32)
    hp_bf16 = jax.ShapeDtypeStruct((m, HP), BF16)
    hp2_bf16 = jax.ShapeDtypeStruct((2, m, HP), BF16)
    cq, sq, ck, sk = tabs
    tab_spec = pl.BlockSpec((TM, LANE), lambda i: (i % tpb, 0))

    q, k, v = pl.pallas_call(
        _mla_kernel,
        grid=(n_tiles,),
        in_specs=[x_spec, _full((1, d)), mod_spec, _full(w['win_b'].shape), _full((1, B_Q_LORA)),
                  _full((1, B_KV_LORA)), _full(w['wq'].shape), _full(w['wqs'].shape),
                  _full(w['wk'].shape), _full(w['wv'].shape), _full(w['place'].shape),
                  tab_spec, tab_spec, tab_spec, tab_spec],
        out_specs=[hp_spec, hp_spec, hp_spec],
        out_shape=[hp_bf16, hp_bf16, hp_bf16],
        compiler_params=_params(1),
        name="mla_proj",
    )(xs, w['g1'], mods, w['win_b'], w['qg'], w['kvg'], w['wq'], w['wqs'], w['wk'], w['wv'],
      w['place'], cq, sq, ck, sk)

    ya, gate = pl.pallas_call(
        _gmlp_kernel,
        grid=(n_tiles,),
        in_specs=[x_spec, _full((1, d)), mod_spec, _full(w['wa'].shape), _full(w['wg'].shape),
                  _full((1, A_WIDTH)), _full(w['ws'].shape), _full(w['bsb'].shape)],
        out_specs=[pl.BlockSpec((TM, A_WIDTH), row), pl.BlockSpec((TM, G_COLS), row)],
        out_shape=[jax.ShapeDtypeStruct((m, A_WIDTH), BF16), jax.ShapeDtypeStruct((m, G_COLS), BF16)],
        compiler_params=_params(1),
        name="gmlp_gates",
    )(xs, w['g1'], mods, w['wa'], w['wg'], w['vg'], w['ws'], w['bsb'])

    sub = TM // 8
    pe_spec = pl.BlockSpec((2, 8 * (TM // SCAN_C), HP), lambda i: (0, i, 0))
    pe_shape = jax.ShapeDtypeStruct((2, 8 * (m // SCAN_C), HP), F32)
    vv, ab, bh, kh, rbar, pe, bonus, gg = pl.pallas_call(
        functools.partial(_feat_kernel, tiles_per_batch=tpb),
        grid=(n_tiles,),
        in_specs=[x_spec,
                  pl.BlockSpec((8, d), lambda i: (jnp.maximum(i * sub - 1, 0), 0)),
                  pl.BlockSpec((8, d), lambda i: (jnp.minimum((i + 1) * sub, m // 8 - 1), 0)),
                  _full((1, d)), mod_spec, _full(w['wc'].shape), _full(w['mu'].shape),
                  _full((2, HP)), _full(w['w2'].shape), _full((2, HP)), _full(w['a2'].shape),
                  _full((1, HP)), _full((1, HP)), _full((1, HP)), _full(w['g2'].shape)],
        out_specs=[hp_spec, hp2_spec, hp2_spec, hp2_spec, hp2_spec, pe_spec, hp_spec, hp_spec],
        out_shape=[hp_bf16, hp2_bf16, hp2_bf16, hp2_bf16, hp2_bf16, pe_shape, hp_f32, hp_f32],
        compiler_params=_params(1),
        name="rwkv_features",
    )(xs, xs, xs, w['g1'], mods, w['wc'], w['mu'], w['w0'], w['w2'], w['a0'], w['a2'],
      w['kk'], w['ka'], w['rk'], w['g2'])

    nc = lb // SCAN_C
    nctx_c = n_ctx // SCAN_C

    def chunk_of(dd, b, j):
        if dd == 0:
            return b * nc + j
        return b * nc + jnp.where(j < nctx_c, nctx_c - 1 - j, nc - 1 + nctx_c - j)

    def tok_spec(dd):
        return pl.BlockSpec((SCAN_C, HP), lambda b, j: (chunk_of(dd, b, j), 0))

    def dir_spec(dd, rows=SCAN_C):
        return pl.BlockSpec((1, rows, HP), lambda b, j: (dd, chunk_of(dd, b, j), 0))

    o_f, o_b = pl.pallas_call(
        _scan_kernel,
        grid=(bn, nc),
        in_specs=[tok_spec(0), tok_spec(1)] + [dir_spec(dd) for _ in range(4) for dd in range(2)]
                 + [dir_spec(0, 8), dir_spec(1, 8)],
        out_specs=[tok_spec(0), tok_spec(1)],
        out_shape=[hp_f32, hp_f32],
        scratch_shapes=[pltpu.VMEM((2, C_HEADS, LANE, LANE), F32)],
        compiler_params=_params(2),
        name="rwkv_scan",
    )(vv, vv, ab, ab, bh, bh, kh, kh, rbar, rbar, pe, pe)

    q3, k3, v3 = (t.reshape(bn, lb, HP) for t in (q, k, v))
    yb = pl.pallas_call(
        functools.partial(_attn_kernel, n_ctx=n_ctx, k_chunk=ATTN_KC),
        grid=(bn, B_HEADS, tpb),
        in_specs=[pl.BlockSpec((1, TM, LANE), lambda b, hh, qi: (b, qi, hh)),
                  pl.BlockSpec((1, lb, LANE), lambda b, hh, qi: (b, 0, hh)),
                  pl.BlockSpec((1, lb, LANE), lambda b, hh, qi: (b, 0, hh))],
        out_specs=pl.BlockSpec((1, TM, LANE), lambda b, hh, qi: (b, qi, hh)),
        out_shape=jax.ShapeDtypeStruct((bn, lb, HP), BF16),
        compiler_params=_params(3),
        name="mla_attention",
    )(q3, k3, v3).reshape(m, HP)

    xn, h2, comb = pl.pallas_call(
        _merge_kernel,
        grid=(n_tiles,),
        in_specs=[hp_spec, hp_spec, hp_spec, hp_spec, pl.BlockSpec((TM, A_WIDTH), row), hp_spec,
                  pl.BlockSpec((TM, G_COLS), row), x_spec, mod_spec, _full((1, HP)), _full((1, HP)),
                  _full(w['wup_a'].shape), _full(w['wup_b'].shape), _full(w['wup_c'].shape),
                  _full(w['wo'].shape), _full((1, d)), _full(rw.shape), _full(rb.shape)],
        out_specs=[x_spec, x_spec, pl.BlockSpec((TM, LANE), row)],
        out_shape=[jax.ShapeDtypeStruct((m, d), F32), jax.ShapeDtypeStruct((m, d), BF16),
                   jax.ShapeDtypeStruct((m, LANE), F32)],
        compiler_params=_params(1),
        name="merge_router",
    )(o_f, o_b, bonus, gg, ya, yb, gate, xs, mods, w['gng'], w['gnb'], w['wup_a'], w['wup_b'],
      w['wup_c'], w['wo'], w['n2'], rw, rb)

    tm_moe = MOE_SUB * TM
    mrow = lambda i, e: (i, 0)
    mseg = lambda s: pl.BlockSpec((1, 6, d), lambda i, e: seg(i * MOE_SUB + s))
    wexp = lambda shape: pl.BlockSpec((1,) + shape, lambda i, e: (e, 0, 0))
    return pl.pallas_call(
        functools.partial(_moe_kernel, last=last),
        grid=(m // tm_moe, N_EXPERTS),
        in_specs=[pl.BlockSpec((tm_moe, d), mrow), pl.BlockSpec((tm_moe, LANE), mrow),
                  pl.BlockSpec((tm_moe, d), mrow), mseg(0), mseg(1), mseg(2),
                  wexp((d, D_EXPERT)), wexp((d, D_EXPERT)), wexp((D_EXPERT, d)),
                  pl.BlockSpec((1, d), lambda i, e: (0, 0))],
        out_specs=pl.BlockSpec((tm_moe, d), mrow),
        out_shape=jax.ShapeDtypeStruct((m, d), F32),
        scratch_shapes=[pltpu.VMEM((tm_moe, d), F32)],
        compiler_params=_params(2),
        name="moe",
    )(h2, comb, xn, mods, mods, mods, w['w1'], w['w3'], w['w2e'], fg)


def kernel(x, c, ctx, c_ctx, mod_w, mod_b, norm1_g, norm2_g, w_in, a_v_gain, a_ws, a_bs, b_q_norm, b_w_uq, b_kv_norm, b_w_ukv, c_mu, c_w0, c_w2, c_a0, c_a2, c_g2, c_k_k, c_k_a, c_r_k, c_gn_g, c_gn_b, w_up_a, w_up_b, w_up_c, w_o, router_w, router_b, moe_w1, moe_w3, moe_w2, final_g):
    p = dict(w_in=w_in, a_v_gain=a_v_gain, a_ws=a_ws, a_bs=a_bs, b_q_norm=b_q_norm, b_w_uq=b_w_uq,
             b_kv_norm=b_kv_norm, b_w_ukv=b_w_ukv, c_mu=c_mu, c_w0=c_w0, c_w2=c_w2, c_a0=c_a0,
             c_a2=c_a2, c_g2=c_g2, c_k_k=c_k_k, c_k_a=c_k_a, c_r_k=c_r_k, c_gn_g=c_gn_g,
             c_gn_b=c_gn_b, w_up_a=w_up_a, w_up_b=w_up_b, w_up_c=w_up_c, w_o=w_o,
             norm1_g=norm1_g, norm2_g=norm2_g, moe_w1=moe_w1, moe_w3=moe_w3, moe_w2=moe_w2)
    bn, t, d = x.shape
    n_ctx = ctx.shape[1]
    lb = n_ctx + t
    depth = mod_w.shape[0]
    assert bn == 2 and n_ctx == TM and lb % (MOE_SUB * TM) == 0 and lb % SCAN_C == 0 and lb % ATTN_KC == 0

    cvec = jnp.concatenate([c_ctx[None], c, jnp.zeros((8 - 1 - bn, d), F32)], axis=0)
    mods = _modulation(cvec, mod_w, mod_b).reshape(depth, 8, 6, d)

    tabs = _rope_tables(n_ctx, t)
    rw = jnp.pad(router_w, ((0, 0), (0, LANE - N_EXPERTS))).astype(BF16)
    rb = jnp.pad(router_b, (0, LANE - N_EXPERTS))[None]
    fg = final_g[None]

    xs = jnp.concatenate([ctx, x], axis=1).reshape(bn * lb, d)
    for l in range(depth):
        xs = _layer(xs, mods[l], _layer_weights(p, l), tabs, rw, rb, fg,
                    bn=bn, lb=lb, n_ctx=n_ctx, last=(l == depth - 1))
    return xs.reshape(bn, lb, d)[:, n_ctx:]
```

```python
import functools
import math

import jax
import jax.numpy as jnp
from jax import lax
from jax.experimental import pallas as pl
from jax.experimental.pallas import tpu as pltpu

F32 = jnp.float32
BF16 = jnp.bfloat16
HIGHEST = lax.Precision.HIGHEST

D_MODEL = 1024
GRID_W = 64
EPS = 1e-6

A_WIDTH = 512
A_GROUPS = 4
A_CHUNK = 128

B_HEADS = 8
B_Q_LORA = 384
B_KV_LORA = 256
B_NOPE = 64
B_ROPE = 32
B_VDIM = 64
ROPE_BASE = 10000.0
ROPE_FREQS = B_ROPE // 4
MLA_SCALE = (B_NOPE + B_ROPE) ** -0.5

C_HEADS = 8
C_HEAD = 64
C_WIDTH = C_HEADS * C_HEAD
C_DECAY_LORA = 64
C_AAA_LORA = 64
C_GATE_LORA = 128
C_GN_EPS = 64e-5
DECAY_SCALE = 0.6065306597126334

B_COLS = B_Q_LORA + B_KV_LORA + B_ROPE
C_COLS = 3 * C_WIDTH + 2 * C_DECAY_LORA + 2 * C_AAA_LORA + C_GATE_LORA
A_COLS = 2 * A_WIDTH
G_COLS = 3 * D_MODEL

N_EXPERTS = 16
N_GROUPS = 4
EXPERTS_PER_GROUP = 4
D_EXPERT = 512

LANE = 128
HP = C_HEADS * LANE
TM = 512
TM_FEAT = 256
SEG_ROWS = 256
SCAN_C = 64
CUM_ROWS = 256
MOE_SUB = 3
ATTN_TQ = 512
ATTN_KC = 1024
VMEM_LIMIT = 52 * 1024 * 1024


def _dot(a, b, precision=None):
    return jnp.dot(a, b, preferred_element_type=F32, precision=precision)


def _dot_nt(a, b, precision=None):
    return lax.dot_general(a, b, (((1,), (1,)), ((), ())),
                           preferred_element_type=F32, precision=precision)


def _dot_tn(a, b, precision=None):
    return lax.dot_general(a, b, (((0,), (0,)), ((), ())),
                           preferred_element_type=F32, precision=precision)


def _rms(x, eps=EPS):
    return x * lax.rsqrt(jnp.mean(x * x, axis=-1, keepdims=True) + eps)


def _normmod(x, g, mod, first):
    return _rms(x) * g * (1.0 + mod[first + 1:first + 2]) + mod[first:first + 1]


def _sigmoid(x):
    return 1.0 / (1.0 + jnp.exp(-x))


def _params(n_grid):
    return pltpu.CompilerParams(
        dimension_semantics=("arbitrary",) * n_grid, vmem_limit_bytes=VMEM_LIMIT)


def _full(shape):
    n = len(shape)
    return pl.BlockSpec(shape, lambda *_: (0,) * n)


def _mod_kernel(c_ref, w_ref, b_ref, o_ref):
    c = c_ref[...]
    act = (c * _sigmoid(c)).astype(BF16)
    o_ref[0] = _dot(act, w_ref[0].astype(BF16)) + b_ref[0]


def _modulation(cvec, mod_w, mod_b):
    depth, d, n = mod_w.shape
    tn = 1536
    return pl.pallas_call(
        _mod_kernel,
        grid=(depth, n // tn),
        in_specs=[
            pl.BlockSpec((8, d), lambda l, j: (0, 0)),
            pl.BlockSpec((1, d, tn), lambda l, j: (l, 0, j)),
            pl.BlockSpec((1, 1, tn), lambda l, j: (l, 0, j)),
        ],
        out_specs=pl.BlockSpec((1, 8, tn), lambda l, j: (l, 0, j)),
        out_shape=jax.ShapeDtypeStruct((depth, 8, n), F32),
        compiler_params=_params(2),
        name="modulation",
    )(cvec, mod_w, mod_b.reshape(depth, 1, n))


def _mla_kernel(x_ref, g1_ref, mod_ref, win_ref, qg_ref, kvg_ref, wq_ref, wqs_ref,
                wk_ref, wv_ref, place_ref, cq_ref, sq_ref, ck_ref, sk_ref,
                q_out, k_out, v_out):
    h = _normmod(x_ref[...], g1_ref[...], mod_ref[0], 0).astype(BF16)
    z = _dot(h, win_ref[...])
    zq = z[:, :B_Q_LORA]
    zkv = z[:, B_Q_LORA:B_Q_LORA + B_KV_LORA]
    kr_a = z[:, B_Q_LORA + B_KV_LORA:B_Q_LORA + B_KV_LORA + LANE]
    kr_b = z[:, B_Q_LORA + B_KV_LORA + LANE:]
    qn = (_rms(zq) * qg_ref[...]).astype(BF16)
    kvn = (_rms(zkv) * kvg_ref[...]).astype(BF16)
    q1 = _dot(qn, wq_ref[...])
    q2 = _dot(qn, wqs_ref[...])
    kr = (kr_a * ck_ref[...] + kr_b * sk_ref[...]).astype(BF16)
    k = _dot(kvn, wk_ref[...]) + _dot(kr, place_ref[...])
    cq = cq_ref[...]
    sq = sq_ref[...]
    for hh in range(B_HEADS):
        sl = slice(hh * LANE, (hh + 1) * LANE)
        q_out[:, sl] = (q1[:, sl] * cq + q2[:, sl] * sq).astype(BF16)
    k_out[...] = k.astype(BF16)
    v_out[...] = _dot(kvn, wv_ref[...]).astype(BF16)


def _gelu_tanh(x):
    return 0.5 * x * (1.0 + jnp.tanh(math.sqrt(2.0 / math.pi) * (x + 0.044715 * (x * x * x))))


def _gmlp_kernel(x_ref, g1_ref, mod_ref, wa_ref, wg_ref, vg_ref, ws_ref, bsb_ref,
                 ya_out, gate_out):
    h = _normmod(x_ref[...], g1_ref[...], mod_ref[0], 0).astype(BF16)
    gate_out[...] = _sigmoid(_dot(h, wg_ref[...])).astype(BF16)
    zg = _gelu_tanh(_dot(h, wa_ref[...]))
    u = zg[:, :A_WIDTH]
    v = (_rms(zg[:, A_WIDTH:]) * vg_ref[...]).astype(BF16)
    for c in range(TM // A_CHUNK):
        rows = slice(c * A_CHUNK, (c + 1) * A_CHUNK)
        for g in range(A_GROUPS):
            cols = slice(g * LANE, (g + 1) * LANE)
            mixed = _dot(ws_ref[g], v[rows, cols]) + bsb_ref[:, cols]
            ya_out[rows, cols] = (u[rows, cols] * mixed).astype(BF16)


def _feat_kernel(x_ref, xp_ref, xn_ref, g1_ref, mod_ref, wc_ref, mu_ref,
                 w0_ref, w2_ref, a0_ref, a2_ref, kk_ref, ka_ref, rk_ref, g2_ref,
                 v_out, ab_out, bh_out, kh_out, rb_out, pe_out, bonus_out, g_out,
                 *, seq_starts, seq_ends):
    mod = mod_ref[0]
    g1 = g1_ref[...]
    h = _normmod(x_ref[...], g1, mod, 0).astype(BF16)
    hp = _normmod(xp_ref[...], g1, mod, 0).astype(BF16)
    hn = _normmod(xn_ref[...], g1, mod, 0).astype(BF16)
    tm = x_ref.shape[0]
    row = lax.broadcasted_iota(jnp.int32, (tm, 1), 0)
    grow = row + pl.program_id(0) * tm
    no_prev = functools.reduce(jnp.logical_or, [grow == s for s in seq_starts])
    no_next = functools.reduce(jnp.logical_or, [grow == e for e in seq_ends])

    def shifted_proj(cols):
        w = wc_ref[:, cols]
        z = _dot(h, w)
        z_up = jnp.where(row == 0, _dot(hp, w)[7:8], pltpu.roll(z, 1, 0))
        z_dn = jnp.where(row == tm - 1, _dot(hn, w)[0:1], pltpu.roll(z, tm - 1, 0))
        z_up = jnp.where(no_prev, 0.0, z_up)
        z_dn = jnp.where(no_next, 0.0, z_dn)
        return z + mu_ref[:, cols] * (0.5 * (z_up + z_dn) - z)

    r = shifted_proj(slice(0, HP))
    k = shifted_proj(slice(HP, 2 * HP))
    v = shifted_proj(slice(2 * HP, 3 * HP))
    zl = shifted_proj(slice(3 * HP, 3 * HP + 3 * LANE))
    zw = jnp.tanh(zl[:, :LANE]).astype(BF16)
    za = zl[:, LANE:2 * LANE].astype(BF16)
    zg = _sigmoid(zl[:, 2 * LANE:]).astype(BF16)

    kk = k * kk_ref[...]
    ka = ka_ref[...]
    rk = rk_ref[...]
    a = [_sigmoid(a0_ref[d:d + 1] + _dot(za, a2_ref[d])) for d in range(2)]
    kd = [k * (1.0 + (a[d] - 1.0) * ka) for d in range(2)]
    v_out[...] = v.astype(BF16)
    g_out[...] = _dot(zg, g2_ref[...])

    ti = lax.broadcasted_iota(jnp.int32, (CUM_ROWS, CUM_ROWS), 0)
    tj = lax.broadcasted_iota(jnp.int32, (CUM_ROWS, CUM_ROWS), 1)
    shift = SCAN_C.bit_length() - 1
    same = lax.shift_right_logical(ti, shift) == lax.shift_right_logical(tj, shift)
    p_in, p_prev, p_inv = [], [], []
    for d in range(2):
        lw = -DECAY_SCALE * _sigmoid(w0_ref[d:d + 1] + _dot(zw, w2_ref[d]))
        tri = jnp.where(same & ((tj <= ti) if d == 0 else (tj >= ti)), 1.0, 0.0).astype(BF16)
        hi = lw.astype(BF16)
        rest = lw - hi.astype(F32)
        mid = rest.astype(BF16)
        lo = (rest - mid.astype(F32)).astype(BF16)
        cs = jnp.concatenate(
            [_dot(tri, hi[rows]) + _dot(tri, mid[rows]) + _dot(tri, lo[rows])
             for rows in (slice(c0, c0 + CUM_ROWS) for c0 in range(0, tm, CUM_ROWS))], axis=0)
        p_in.append(jnp.exp(cs))
        p_prev.append(jnp.exp(cs - lw))
        p_inv.append(jnp.exp(-cs))
        kh_out[d] = (kd[d] * p_inv[d]).astype(BF16)
        rb_out[d] = (r * p_in[d]).astype(BF16)
        for cc in range(tm // SCAN_C):
            last = cc * SCAN_C + (SCAN_C - 1 if d == 0 else 0)
            pe_out[d, cc * 8:(cc + 1) * 8, :] = jnp.broadcast_to(p_in[d][last:last + 1], (8, HP))

    for hh in range(C_HEADS):
        sl = slice(hh * LANE, (hh + 1) * LANE)
        kh = kk[:, sl]
        kh = kh * lax.rsqrt(jnp.maximum(jnp.sum(kh * kh, axis=-1, keepdims=True), 1e-12))
        rh = r[:, sl] * rk[:, sl]
        bon = jnp.zeros((tm, 1), F32)
        for d in range(2):
            ab_out[d, :, sl] = (-kh * p_prev[d][:, sl]).astype(BF16)
            bh_out[d, :, sl] = (kh * a[d][:, sl] * p_inv[d][:, sl]).astype(BF16)
            bon = bon + jnp.sum(rh * kd[d][:, sl], axis=-1, keepdims=True)
        bonus_out[:, sl] = bon * v[:, sl]


def _mm(a, b, dims=((1,), (0,))):
    return lax.dot_general(a.astype(BF16), b.astype(BF16), (dims, ((), ())),
                           preferred_element_type=F32)


_NT = ((1,), (1,))
_TN = ((0,), (0,))


def _scan_kernel(vf_ref, vb_ref, abf_ref, abb_ref, bhf_ref, bhb_ref, khf_ref, khb_ref,
                 rbf_ref, rbb_ref, pef_ref, peb_ref, of_ref, ob_ref, s_ref):
    @pl.when(pl.program_id(1) == 0)
    def _():
        s_ref[...] = jnp.zeros_like(s_ref)

    c = SCAN_C
    row = lax.broadcasted_iota(jnp.int32, (c, c), 0)
    col = lax.broadcasted_iota(jnp.int32, (c, c), 1)
    refs = ((vf_ref, abf_ref, bhf_ref, khf_ref, rbf_ref, pef_ref, of_ref),
            (vb_ref, abb_ref, bhb_ref, khb_ref, rbb_ref, peb_ref, ob_ref))

    chains = [(d, hh) for d in range(2) for hh in range(C_HEADS)]

    def sl(hh):
        return slice(hh * LANE, (hh + 1) * LANE)

    al = lambda ch: refs[ch[0]][1][0, :, sl(ch[1])]
    bh = lambda ch: refs[ch[0]][2][0, :, sl(ch[1])]
    kh = lambda ch: refs[ch[0]][3][0, :, sl(ch[1])]
    rb = lambda ch: refs[ch[0]][4][0, :, sl(ch[1])]
    vv = lambda ch: refs[ch[0]][0][:, sl(ch[1])]
    pe = lambda ch: refs[ch[0]][5][0, 0:1, sl(ch[1])]
    strict = lambda ch: (col < row) if ch[0] == 0 else (col > row)
    incl = lambda ch: (col <= row) if ch[0] == 0 else (col >= row)

    xb = {ch: _mm(jnp.concatenate([al(ch), rb(ch)], axis=0), bh(ch), _NT) for ch in chains}
    xk = {ch: _mm(jnp.concatenate([al(ch), rb(ch)], axis=0), kh(ch), _NT) for ch in chains}
    l_pow = {ch: jnp.where(strict(ch), xb[ch][:c], 0.0) for ch in chains}
    u = {ch: _mm(jnp.where(strict(ch), xk[ch][:c], 0.0), vv(ch)) + _mm(al(ch), s_ref[ch], _NT)
         for ch in chains}
    for it in range(6):
        u = {ch: u[ch] + _mm(l_pow[ch], u[ch]) for ch in chains}
        if it < 5:
            l_pow = {ch: _mm(l_pow[ch], l_pow[ch]) for ch in chains}
    for ch in chains:
        o_ref = refs[ch[0]][6]
        o_ref[:, sl(ch[1])] = (_mm(rb(ch), s_ref[ch], _NT)
                               + _mm(jnp.where(incl(ch), xb[ch][c:], 0.0), u[ch])
                               + _mm(jnp.where(incl(ch), xk[ch][c:], 0.0), vv(ch)))
    for ch in chains:
        upd = _mm(jnp.concatenate([u[ch].astype(BF16), vv(ch)], axis=0),
                  jnp.concatenate([bh(ch), kh(ch)], axis=0), _TN)
        s_ref[ch] = (s_ref[ch] + upd) * pe(ch)


def _attn_ctx_kernel(q_ref, k_ref, v_ref, o_ref):
    s = _dot_nt(q_ref[...], k_ref[...])
    p = jnp.exp2(s - jnp.max(s, axis=-1, keepdims=True))
    l = jnp.sum(p, axis=-1, keepdims=True)
    o_ref[...] = (_dot(p.astype(BF16), v_ref[...]) / l).astype(BF16)


def _attn_kernel(q_ref, kc_ref, vc_ref, kl_ref, vl_ref, o_ref, *, k_chunk):
    q = q_ref[...]
    parts = [(kc_ref, vc_ref, 0, kc_ref.shape[0])]
    parts += [(kl_ref, vl_ref, c0, k_chunk) for c0 in range(0, kl_ref.shape[0], k_chunk)]
    scores = lambda part: _dot_nt(q, part[0][part[2]:part[2] + part[3], :])
    m = jnp.full((q.shape[0], 1), -jnp.inf, F32)
    l = jnp.zeros((q.shape[0], 1), F32)
    acc = jnp.zeros((q.shape[0], LANE), F32)
    s_next = scores(parts[0])
    for n, (_, v_ref, c0, size) in enumerate(parts):
        s = s_next
        if n + 1 < len(parts):
            s_next = scores(parts[n + 1])
        m_new = jnp.maximum(m, jnp.max(s, axis=-1, keepdims=True))
        p = jnp.exp2(s - m_new)
        corr = jnp.exp2(m - m_new)
        l = l * corr + jnp.sum(p, axis=-1, keepdims=True)
        acc = acc * corr + _dot(p.astype(BF16), v_ref[c0:c0 + size, :])
        m = m_new
    o_ref[...] = (acc / l).astype(BF16)


def _merge_kernel(of_ref, ob_ref, bonus_ref, g_ref, ya_ref, ybl_ref, ybc_ref, gate_ref, x_ref,
                  mod_ref, gng_ref, gnb_ref, wa_ref, wb_ref, wc_ref, wo_ref, n2_ref,
                  rw_ref, rb_ref, xn_out, h2_out, comb_out, *, n_lat_tiles):
    mod = mod_ref[0]
    yb = jnp.where(pl.program_id(0) < n_lat_tiles, ybl_ref[...], ybc_ref[...])
    o = of_ref[...] + ob_ref[...]
    lane = lax.broadcasted_iota(jnp.int32, (1, LANE), 1)
    real = lane < C_HEAD
    ycs = []
    for hh in range(C_HEADS):
        sl = slice(hh * LANE, (hh + 1) * LANE)
        oh = o[:, sl]
        mean = jnp.sum(oh, axis=-1, keepdims=True) * (1.0 / C_HEAD)
        dev = jnp.where(real, oh - mean, 0.0)
        var = jnp.sum(dev * dev, axis=-1, keepdims=True) * (1.0 / C_HEAD)
        y = dev * lax.rsqrt(var + C_GN_EPS) * gng_ref[:, sl] + gnb_ref[:, sl]
        ycs.append(((y + bonus_ref[:, sl]) * g_ref[:, sl]).astype(BF16))
    yc = jnp.concatenate(ycs, axis=1)
    gate = gate_ref[...].astype(F32)
    merged = (gate[:, :D_MODEL] * _dot(ya_ref[...], wa_ref[...])
              + gate[:, D_MODEL:2 * D_MODEL] * _dot(yb, wb_ref[...])
              + gate[:, 2 * D_MODEL:] * _dot(yc, wc_ref[...]))
    xn = x_ref[...] + mod[2:3] * _dot(merged.astype(BF16), wo_ref[...])
    xn_out[...] = xn
    h2 = _normmod(xn, n2_ref[...], mod, 3).astype(BF16)
    h2_out[...] = h2

    scores = _sigmoid(_dot(h2, rw_ref[...]))
    neg = -jnp.inf
    sel = jnp.where(lane < N_EXPERTS, scores + rb_ref[...], neg)

    lane_f = lane.astype(F32)

    def top1(s):
        mx = jnp.max(s, axis=-1, keepdims=True)
        idx = jnp.min(jnp.where(s == mx, lane_f, float(LANE)), axis=-1, keepdims=True)
        return mx, idx.astype(jnp.int32)

    best = None
    for g in range(N_GROUPS):
        in_g = (lane >= g * EXPERTS_PER_GROUP) & (lane < (g + 1) * EXPERTS_PER_GROUP)
        s = jnp.where(in_g, sel, neg)
        m1, i1 = top1(s)
        m2, _ = top1(jnp.where(lane == i1, neg, s))
        score = m1 + m2
        if best is None:
            best, gidx = score, jnp.zeros_like(i1)
        else:
            better = score > best
            gidx = jnp.where(better, g, gidx)
            best = jnp.where(better, score, best)
    lo = gidx * EXPERTS_PER_GROUP
    s = jnp.where((lane >= lo) & (lane < lo + EXPERTS_PER_GROUP), sel, neg)
    _, i1 = top1(s)
    _, i2 = top1(jnp.where(lane == i1, neg, s))
    picked = jnp.where((lane == i1) | (lane == i2), scores, 0.0)
    comb_out[...] = picked / jnp.sum(picked, axis=-1, keepdims=True)


def _moe_kernel(h2_ref, comb_ref, xn_ref, m0_ref, m1_ref, m2_ref, w1_ref, w3_ref, w2_ref,
                fg_ref, out_ref, acc_ref, *, last):
    e = pl.program_id(1)

    @pl.when(e == 0)
    def _():
        acc_ref[...] = jnp.zeros_like(acc_ref)

    t = h2_ref[...]
    a = _dot(t, w1_ref[0])
    he = (a * _sigmoid(a) * _dot(t, w3_ref[0])).astype(BF16)
    lane = lax.broadcasted_iota(jnp.int32, (1, LANE), 1)
    ce = jnp.sum(jnp.where(lane == e, comb_ref[...], 0.0), axis=-1, keepdims=True)
    acc_ref[...] += ce * _dot(he, w2_ref[0])

    @pl.when(e == N_EXPERTS - 1)
    def _():
        for s, m_ref in enumerate((m0_ref, m1_ref, m2_ref)):
            rows = slice(s * SEG_ROWS, (s + 1) * SEG_ROWS)
            y = xn_ref[rows] + m_ref[0][5:6] * acc_ref[rows]
            if last:
                y = _rms(y) * fg_ref[...]
            out_ref[rows] = y


def _head_pad_cols(w, width):
    lead = w.shape[:-1]
    w = w.reshape(lead + (C_HEADS, width))
    w = jnp.pad(w, [(0, 0)] * len(lead) + [(0, 0), (0, LANE - width)])
    return w.reshape(lead + (HP,))


def _head_pad_rows(w, width):
    n = w.shape[-1]
    w = w.reshape(C_HEADS, width, n)
    w = jnp.pad(w, ((0, 0), (0, LANE - width), (0, 0)))
    return w.reshape(HP, n)


def _split_rope(w):
    lead = w.shape[:-1]
    w = w.reshape(lead + (2, 2, ROPE_FREQS))
    x1 = w[..., 0, :].reshape(lead + (2 * ROPE_FREQS,))
    x2 = w[..., 1, :].reshape(lead + (2 * ROPE_FREQS,))
    return x1, x2


def _rope_tables(n_ctx, t):
    rows = t // GRID_W
    row = jnp.repeat(jnp.arange(rows, dtype=F32), GRID_W)
    col = jnp.tile(jnp.arange(GRID_W, dtype=F32), rows)
    inv = jnp.power(ROPE_BASE, -jnp.arange(ROPE_FREQS, dtype=F32) / ROPE_FREQS)
    ang = jnp.concatenate([row[:, None] * inv, col[:, None] * inv], axis=1)
    cos = jnp.concatenate([jnp.cos(ang), jnp.ones((n_ctx, 2 * ROPE_FREQS), F32)], axis=0)
    sin = jnp.concatenate([jnp.sin(ang), jnp.zeros((n_ctx, 2 * ROPE_FREQS), F32)], axis=0)
    n = n_ctx + t
    one = jnp.ones((n, B_NOPE), F32)
    zero = lambda w: jnp.zeros((n, w), F32)
    qs = MLA_SCALE * math.log2(math.e)
    cq = jnp.concatenate([one, cos, cos, zero(LANE - B_NOPE - B_ROPE)], axis=1) * qs
    sq = jnp.concatenate([zero(B_NOPE), -sin, sin, zero(LANE - B_NOPE - B_ROPE)], axis=1) * qs
    ck = jnp.concatenate([cos, cos, zero(LANE - B_ROPE)], axis=1)
    sk = jnp.concatenate([-sin, sin, zero(LANE - B_ROPE)], axis=1)
    return cq, sq, ck, sk


def _layer_weights(p, l):
    w = {}
    w_in = p['w_in'][l]
    d = w_in.shape[0]
    o = 0
    w_q = w_in[:, o:o + B_Q_LORA]; o += B_Q_LORA
    w_kv = w_in[:, o:o + B_KV_LORA]; o += B_KV_LORA
    w_kr = w_in[:, o:o + B_ROPE]; o += B_ROPE
    w_c = w_in[:, o:o + C_COLS]; o += C_COLS
    w_a = w_in[:, o:o + A_COLS]; o += A_COLS
    w_g = w_in[:, o:o + G_COLS]

    k1, k2 = _split_rope(w_kr)
    zpad = jnp.zeros((d, LANE - B_ROPE), F32)
    w['win_b'] = jnp.concatenate([w_q, w_kv, k1, k2, zpad, k2, k1, zpad], axis=1).astype(BF16)

    wuq = p['b_w_uq'][l].reshape(B_Q_LORA, B_HEADS, B_NOPE + B_ROPE)
    q1, q2 = _split_rope(wuq[..., B_NOPE:])
    qz = jnp.zeros((B_Q_LORA, B_HEADS, LANE - B_NOPE - B_ROPE), F32)
    w['wq'] = jnp.concatenate([wuq[..., :B_NOPE], q1, q2, qz], axis=-1).reshape(B_Q_LORA, HP).astype(BF16)
    w['wqs'] = jnp.concatenate([wuq[..., :B_NOPE], q2, q1, qz], axis=-1).reshape(B_Q_LORA, HP).astype(BF16)
    wukv = p['b_w_ukv'][l].reshape(B_KV_LORA, B_HEADS, B_NOPE + B_VDIM)
    w['wk'] = _head_pad_cols(wukv[..., :B_NOPE].reshape(B_KV_LORA, -1), B_NOPE).astype(BF16)
    w['wv'] = _head_pad_cols(wukv[..., B_NOPE:].reshape(B_KV_LORA, -1), B_VDIM).astype(BF16)
    place = jnp.zeros((LANE, B_HEADS, LANE), F32)
    idx = jnp.arange(B_ROPE)
    place = place.at[idx, :, B_NOPE + idx].set(1.0)
    w['place'] = place.reshape(LANE, HP).astype(BF16)
    w['qg'] = p['b_q_norm'][l][None]
    w['kvg'] = p['b_kv_norm'][l][None]

    w['wa'] = w_a.astype(BF16)
    w['wg'] = w_g.astype(BF16)
    w['vg'] = p['a_v_gain'][l][None]
    w['ws'] = p['a_ws'][l].astype(BF16)
    w['bsb'] = jnp.broadcast_to(p['a_bs'][l].T[:, :, None], (A_CHUNK, A_GROUPS, LANE)).reshape(A_CHUNK, A_WIDTH)

    cw = C_WIDTH
    pieces = [_head_pad_cols(w_c[:, i * cw:(i + 1) * cw], C_HEAD) for i in range(3)]
    w['wc'] = jnp.concatenate(pieces + [w_c[:, 3 * cw:]], axis=1).astype(BF16)
    mu = p['c_mu'][l]
    w['mu'] = jnp.concatenate([_head_pad_cols(mu[i * cw:(i + 1) * cw], C_HEAD) for i in range(3)]
                              + [mu[3 * cw:]])[None]

    def lora_block(m, rank):
        m = _head_pad_cols(m, C_HEAD)
        z = jnp.zeros_like(m[0])
        return jnp.stack([jnp.concatenate([m[0], z], axis=0), jnp.concatenate([z, m[1]], axis=0)]).astype(BF16)

    w['w2'] = lora_block(p['c_w2'][l], C_DECAY_LORA)
    w['a2'] = lora_block(p['c_a2'][l], C_AAA_LORA)
    w['w0'] = _head_pad_cols(p['c_w0'][l], C_HEAD)
    w['a0'] = _head_pad_cols(p['c_a0'][l], C_HEAD)
    w['kk'] = _head_pad_cols(p['c_k_k'][l], C_HEAD)[None]
    w['ka'] = _head_pad_cols(p['c_k_a'][l], C_HEAD)[None]
    w['rk'] = _head_pad_cols(p['c_r_k'][l].reshape(-1), C_HEAD)[None]
    w['g2'] = _head_pad_cols(p['c_g2'][l], C_HEAD).astype(BF16)
    w['gng'] = _head_pad_cols(p['c_gn_g'][l], C_HEAD)[None]
    w['gnb'] = _head_pad_cols(p['c_gn_b'][l], C_HEAD)[None]

    w['wup_a'] = p['w_up_a'][l].astype(BF16)
    w['wup_b'] = _head_pad_rows(p['w_up_b'][l], B_VDIM).astype(BF16)
    w['wup_c'] = _head_pad_rows(p['w_up_c'][l], C_HEAD).astype(BF16)
    w['wo'] = p['w_o'][l].astype(BF16)
    w['g1'] = p['norm1_g'][l][None]
    w['n2'] = p['norm2_g'][l][None]
    w['w1'] = p['moe_w1'][l].astype(BF16)
    w['w3'] = p['moe_w3'][l].astype(BF16)
    w['w2e'] = p['moe_w2'][l].astype(BF16)
    return w


def _layer(xs, mods, w, tabs, rw, rb, fg, *, bn, t, n_ctx, last):
    m, d = xs.shape
    n_tiles = m // TM
    n_lat = bn * t
    n_lat_tiles = n_lat // TM
    tpl = t // TM
    row = lambda i: (i, 0)
    seg_of = lambda blk: jnp.where(blk < n_lat // SEG_ROWS, 1 + blk // (t // SEG_ROWS), 0)
    seg = lambda i: (seg_of(i * (TM // SEG_ROWS)), 0, 0)
    x_spec = pl.BlockSpec((TM, d), row)
    mod_spec = pl.BlockSpec((1, 6, d), seg)
    hp_spec = pl.BlockSpec((TM, HP), row)
    hp2_spec = pl.BlockSpec((2, TM, HP), lambda i: (0, i, 0))
    hp_f32 = jax.ShapeDtypeStruct((m, HP), F32)
    hp_bf16 = jax.ShapeDtypeStruct((m, HP), BF16)
    hp2_bf16 = jax.ShapeDtypeStruct((2, m, HP), BF16)
    cq, sq, ck, sk = tabs
    tab_spec = pl.BlockSpec(
        (TM, LANE), lambda i: (jnp.where(i < n_lat_tiles, i % tpl, tpl + i - n_lat_tiles), 0))

    q, k, v = pl.pallas_call(
        _mla_kernel,
        grid=(n_tiles,),
        in_specs=[x_spec, _full((1, d)), mod_spec, _full(w['win_b'].shape), _full((1, B_Q_LORA)),
                  _full((1, B_KV_LORA)), _full(w['wq'].shape), _full(w['wqs'].shape),
                  _full(w['wk'].shape), _full(w['wv'].shape), _full(w['place'].shape),
                  tab_spec, tab_spec, tab_spec, tab_spec],
        out_specs=[hp_spec, hp_spec, hp_spec],
        out_shape=[hp_bf16, hp_bf16, hp_bf16],
        compiler_params=_params(1),
        name="mla_proj",
    )(xs, w['g1'], mods, w['win_b'], w['qg'], w['kvg'], w['wq'], w['wqs'], w['wk'], w['wv'],
      w['place'], cq, sq, ck, sk)

    ya, gate = pl.pallas_call(
        _gmlp_kernel,
        grid=(n_tiles,),
        in_specs=[x_spec, _full((1, d)), mod_spec, _full(w['wa'].shape), _full(w['wg'].shape),
                  _full((1, A_WIDTH)), _full(w['ws'].shape), _full(w['bsb'].shape)],
        out_specs=[pl.BlockSpec((TM, A_WIDTH), row), pl.BlockSpec((TM, G_COLS), row)],
        out_shape=[jax.ShapeDtypeStruct((m, A_WIDTH), BF16), jax.ShapeDtypeStruct((m, G_COLS), BF16)],
        compiler_params=_params(1),
        name="gmlp_gates",
    )(xs, w['g1'], mods, w['wa'], w['wg'], w['vg'], w['ws'], w['bsb'])

    tf = TM_FEAT
    sub = tf // 8
    seq_starts = tuple(b * t for b in range(bn)) + tuple(n_lat + b * n_ctx for b in range(bn))
    seq_ends = tuple(s - 1 for s in seq_starts[1:]) + (m - 1,)
    f_hp = pl.BlockSpec((tf, HP), row)
    f_hp2 = pl.BlockSpec((2, tf, HP), lambda i: (0, i, 0))
    pe_spec = pl.BlockSpec((2, 8 * (tf // SCAN_C), HP), lambda i: (0, i, 0))
    pe_shape = jax.ShapeDtypeStruct((2, 8 * (m // SCAN_C), HP), F32)
    vv, ab, bh, kh, rbar, pe, bonus, gg = pl.pallas_call(
        functools.partial(_feat_kernel, seq_starts=seq_starts, seq_ends=seq_ends),
        grid=(m // tf,),
        in_specs=[pl.BlockSpec((tf, d), row),
                  pl.BlockSpec((8, d), lambda i: (jnp.maximum(i * sub - 1, 0), 0)),
                  pl.BlockSpec((8, d), lambda i: (jnp.minimum((i + 1) * sub, m // 8 - 1), 0)),
                  _full((1, d)), pl.BlockSpec((1, 6, d), lambda i: (seg_of(i * (tf // SEG_ROWS)), 0, 0)),
                  _full(w['wc'].shape), _full(w['mu'].shape),
                  _full((2, HP)), _full(w['w2'].shape), _full((2, HP)), _full(w['a2'].shape),
                  _full((1, HP)), _full((1, HP)), _full((1, HP)), _full(w['g2'].shape)],
        out_specs=[f_hp, f_hp2, f_hp2, f_hp2, f_hp2, pe_spec, f_hp, f_hp],
        out_shape=[hp_bf16, hp2_bf16, hp2_bf16, hp2_bf16, hp2_bf16, pe_shape, hp_f32, hp_f32],
        compiler_params=_params(1),
        name="rwkv_features",
    )(xs, xs, xs, w['g1'], mods, w['wc'], w['mu'], w['w0'], w['w2'], w['a0'], w['a2'],
      w['kk'], w['ka'], w['rk'], w['g2'])

    nctx_c = n_ctx // SCAN_C
    nlat_c = t // SCAN_C
    nc = nctx_c + nlat_c

    def chunk_of(dd, b, j):
        ctx_base = n_lat // SCAN_C + b * nctx_c
        lat_base = b * nlat_c
        if dd == 0:
            return jnp.where(j < nctx_c, ctx_base + j, lat_base + j - nctx_c)
        return jnp.where(j < nctx_c, ctx_base + nctx_c - 1 - j, lat_base + nc - 1 - j)

    def tok_spec(dd):
        return pl.BlockSpec((SCAN_C, HP), lambda b, j: (chunk_of(dd, b, j), 0))

    def dir_spec(dd, rows=SCAN_C):
        return pl.BlockSpec((1, rows, HP), lambda b, j: (dd, chunk_of(dd, b, j), 0))

    o_f, o_b = pl.pallas_call(
        _scan_kernel,
        grid=(bn, nc),
        in_specs=[tok_spec(0), tok_spec(1)] + [dir_spec(dd) for _ in range(4) for dd in range(2)]
                 + [dir_spec(0, 8), dir_spec(1, 8)],
        out_specs=[tok_spec(0), tok_spec(1)],
        out_shape=[hp_f32, hp_f32],
        scratch_shapes=[pltpu.VMEM((2, C_HEADS, LANE, LANE), F32)],
        compiler_params=_params(2),
        name="rwkv_scan",
    )(vv, vv, ab, ab, bh, bh, kh, kh, rbar, rbar, pe, pe)

    ctx_blk = n_lat // n_ctx
    tq = ATTN_TQ
    ctx_kv = pl.BlockSpec((n_ctx, LANE), lambda b, hh, qi: (ctx_blk + b, hh))
    lat_kv = pl.BlockSpec((t, LANE), lambda b, hh, qi: (b, hh))
    q_lat = pl.BlockSpec((tq, LANE), lambda b, hh, qi: (b * (t // tq) + qi, hh))
    yb_lat = pl.pallas_call(
        functools.partial(_attn_kernel, k_chunk=ATTN_KC),
        grid=(bn, B_HEADS, t // tq),
        in_specs=[q_lat, ctx_kv, ctx_kv, lat_kv, lat_kv],
        out_specs=q_lat,
        out_shape=jax.ShapeDtypeStruct((n_lat, HP), BF16),
        compiler_params=_params(3),
        name="mla_attention",
    )(q, k, v, k, v)
    ctx_q = pl.BlockSpec((n_ctx, LANE), lambda b, hh: (ctx_blk + b, hh))
    yb_ctx = pl.pallas_call(
        _attn_ctx_kernel,
        grid=(bn, B_HEADS),
        in_specs=[ctx_q, ctx_q, ctx_q],
        out_specs=pl.BlockSpec((n_ctx, LANE), lambda b, hh: (b, hh)),
        out_shape=jax.ShapeDtypeStruct((bn * n_ctx, HP), BF16),
        compiler_params=_params(2),
        name="mla_attention_ctx",
    )(q, k, v)

    n_ctx_tiles = n_tiles - n_lat_tiles
    xn, h2, comb = pl.pallas_call(
        functools.partial(_merge_kernel, n_lat_tiles=n_lat_tiles),
        grid=(n_tiles,),
        in_specs=[hp_spec, hp_spec, hp_spec, hp_spec, pl.BlockSpec((TM, A_WIDTH), row),
                  pl.BlockSpec((TM, HP), lambda i: (jnp.minimum(i, n_lat_tiles - 1), 0)),
                  pl.BlockSpec((TM, HP), lambda i: (jnp.clip(i - n_lat_tiles, 0, n_ctx_tiles - 1), 0)),
                  pl.BlockSpec((TM, G_COLS), row), x_spec, mod_spec, _full((1, HP)), _full((1, HP)),
                  _full(w['wup_a'].shape), _full(w['wup_b'].shape), _full(w['wup_c'].shape),
                  _full(w['wo'].shape), _full((1, d)), _full(rw.shape), _full(rb.shape)],
        out_specs=[x_spec, x_spec, pl.BlockSpec((TM, LANE), row)],
        out_shape=[jax.ShapeDtypeStruct((m, d), F32), jax.ShapeDtypeStruct((m, d), BF16),
                   jax.ShapeDtypeStruct((m, LANE), F32)],
        compiler_params=_params(1),
        name="merge_router",
    )(o_f, o_b, bonus, gg, ya, yb_lat, yb_ctx, gate, xs, mods, w['gng'], w['gnb'], w['wup_a'], w['wup_b'],
      w['wup_c'], w['wo'], w['n2'], rw, rb)

    tm_moe = MOE_SUB * SEG_ROWS
    mrow = lambda i, e: (i, 0)
    mseg = lambda s: pl.BlockSpec((1, 6, d), lambda i, e: (seg_of(i * MOE_SUB + s), 0, 0))
    wexp = lambda shape: pl.BlockSpec((1,) + shape, lambda i, e: (e, 0, 0))
    return pl.pallas_call(
        functools.partial(_moe_kernel, last=last),
        grid=(m // tm_moe, N_EXPERTS),
        in_specs=[pl.BlockSpec((tm_moe, d), mrow), pl.BlockSpec((tm_moe, LANE), mrow),
                  pl.BlockSpec((tm_moe, d), mrow), mseg(0), mseg(1), mseg(2),
                  wexp((d, D_EXPERT)), wexp((d, D_EXPERT)), wexp((D_EXPERT, d)),
                  pl.BlockSpec((1, d), lambda i, e: (0, 0))],
        out_specs=pl.BlockSpec((tm_moe, d), mrow),
        out_shape=jax.ShapeDtypeStruct((m, d), F32),
        scratch_shapes=[pltpu.VMEM((tm_moe, d), F32)],
        compiler_params=_params(2),
        name="moe",
    )(h2, comb, xn, mods, mods, mods, w['w1'], w['w3'], w['w2e'], fg)


def kernel(x, c, ctx, c_ctx, mod_w, mod_b, norm1_g, norm2_g, w_in, a_v_gain, a_ws, a_bs, b_q_norm, b_w_uq, b_kv_norm, b_w_ukv, c_mu, c_w0, c_w2, c_a0, c_a2, c_g2, c_k_k, c_k_a, c_r_k, c_gn_g, c_gn_b, w_up_a, w_up_b, w_up_c, w_o, router_w, router_b, moe_w1, moe_w3, moe_w2, final_g):
    p = dict(w_in=w_in, a_v_gain=a_v_gain, a_ws=a_ws, a_bs=a_bs, b_q_norm=b_q_norm, b_w_uq=b_w_uq,
             b_kv_norm=b_kv_norm, b_w_ukv=b_w_ukv, c_mu=c_mu, c_w0=c_w0, c_w2=c_w2, c_a0=c_a0,
             c_a2=c_a2, c_g2=c_g2, c_k_k=c_k_k, c_k_a=c_k_a, c_r_k=c_r_k, c_gn_g=c_gn_g,
             c_gn_b=c_gn_b, w_up_a=w_up_a, w_up_b=w_up_b, w_up_c=w_up_c, w_o=w_o,
             norm1_g=norm1_g, norm2_g=norm2_g, moe_w1=moe_w1, moe_w3=moe_w3, moe_w2=moe_w2)
    bn, t, d = x.shape
    n_ctx = ctx.shape[1]
    depth = mod_w.shape[0]
    m = bn * (t + n_ctx)
    assert n_ctx == SEG_ROWS and (bn * n_ctx) % TM == 0 and t % TM == 0 and t % ATTN_TQ == 0
    assert t % ATTN_KC == 0 and m % (MOE_SUB * SEG_ROWS) == 0 and 1 + bn <= 8

    cvec = jnp.concatenate([c_ctx[None], c, jnp.zeros((8 - 1 - bn, d), F32)], axis=0)
    mods = _modulation(cvec, mod_w, mod_b).reshape(depth, 8, 6, d)

    tabs = _rope_tables(bn * n_ctx, t)
    rw = jnp.pad(router_w, ((0, 0), (0, LANE - N_EXPERTS))).astype(BF16)
    rb = jnp.pad(router_b, (0, LANE - N_EXPERTS))[None]
    fg = final_g[None]

    xs = jnp.concatenate([x.reshape(bn * t, d), ctx.reshape(bn * n_ctx, d)], axis=0)
    for l in range(depth):
        xs = _layer(xs, mods[l], _layer_weights(p, l), tabs, rw, rb, fg,
                    bn=bn, t=t, n_ctx=n_ctx, last=(l == depth - 1))
    return xs[:bn * t].reshape(bn, t, d)
```

```python
import functools
import math

import jax
import jax.numpy as jnp
from jax import lax
from jax.experimental import pallas as pl
from jax.experimental.pallas import tpu as pltpu

F32 = jnp.float32
BF16 = jnp.bfloat16
HIGHEST = lax.Precision.HIGHEST

D_MODEL = 1024
GRID_W = 64
EPS = 1e-6

A_WIDTH = 512
A_GROUPS = 4
A_CHUNK = 128

B_HEADS = 8
B_Q_LORA = 384
B_KV_LORA = 256
B_NOPE = 64
B_ROPE = 32
B_VDIM = 64
ROPE_BASE = 10000.0
ROPE_FREQS = B_ROPE // 4
MLA_SCALE = (B_NOPE + B_ROPE) ** -0.5

C_HEADS = 8
C_HEAD = 64
C_WIDTH = C_HEADS * C_HEAD
C_DECAY_LORA = 64
C_AAA_LORA = 64
C_GATE_LORA = 128
C_GN_EPS = 64e-5
DECAY_SCALE = 0.6065306597126334

B_COLS = B_Q_LORA + B_KV_LORA + B_ROPE
C_COLS = 3 * C_WIDTH + 2 * C_DECAY_LORA + 2 * C_AAA_LORA + C_GATE_LORA
A_COLS = 2 * A_WIDTH
G_COLS = 3 * D_MODEL

N_EXPERTS = 16
N_GROUPS = 4
EXPERTS_PER_GROUP = 4
D_EXPERT = 512

LANE = 128
HP = C_HEADS * LANE
TM = 512
TM_FEAT = 256
SEG_ROWS = 256
SCAN_C = 64
CUM_ROWS = 256
MOE_SUB = 6
MOE_CAP = 256
ATTN_TQ = 512
ATTN_KC = 1024
VMEM_LIMIT = 52 * 1024 * 1024


def _dot(a, b, precision=None):
    return jnp.dot(a, b, preferred_element_type=F32, precision=precision)


def _dot_nt(a, b, precision=None):
    return lax.dot_general(a, b, (((1,), (1,)), ((), ())),
                           preferred_element_type=F32, precision=precision)


def _dot_tn(a, b, precision=None):
    return lax.dot_general(a, b, (((0,), (0,)), ((), ())),
                           preferred_element_type=F32, precision=precision)


def _rms(x, eps=EPS):
    return x * lax.rsqrt(jnp.mean(x * x, axis=-1, keepdims=True) + eps)


def _normmod(x, g, mod, first):
    return _rms(x) * g * (1.0 + mod[first + 1:first + 2]) + mod[first:first + 1]


def _sigmoid(x):
    return 0.5 * jnp.tanh(0.5 * x) + 0.5


def _params(n_grid):
    return pltpu.CompilerParams(
        dimension_semantics=("arbitrary",) * n_grid, vmem_limit_bytes=VMEM_LIMIT)


def _full(shape):
    n = len(shape)
    return pl.BlockSpec(shape, lambda *_: (0,) * n)


def _mod_kernel(c_ref, w_ref, b_ref, o_ref):
    c = c_ref[...]
    act = (c * _sigmoid(c)).astype(BF16)
    o_ref[0] = _dot(act, w_ref[0].astype(BF16)) + b_ref[0]


def _modulation(cvec, mod_w, mod_b):
    depth, d, n = mod_w.shape
    tn = 1536
    return pl.pallas_call(
        _mod_kernel,
        grid=(depth, n // tn),
        in_specs=[
            pl.BlockSpec((8, d), lambda l, j: (0, 0)),
            pl.BlockSpec((1, d, tn), lambda l, j: (l, 0, j)),
            pl.BlockSpec((1, 1, tn), lambda l, j: (l, 0, j)),
        ],
        out_specs=pl.BlockSpec((1, 8, tn), lambda l, j: (l, 0, j)),
        out_shape=jax.ShapeDtypeStruct((depth, 8, n), F32),
        compiler_params=_params(2),
        name="modulation",
    )(cvec, mod_w, mod_b.reshape(depth, 1, n))


def _mla_kernel(x_ref, g1_ref, mod_ref, win_ref, qg_ref, kvg_ref, wq_ref, wqs_ref,
                wk_ref, wv_ref, place_ref, cq_ref, sq_ref, ck_ref, sk_ref,
                q_out, k_out, v_out):
    h = _normmod(x_ref[...], g1_ref[...], mod_ref[0], 0).astype(BF16)
    z = _dot(h, win_ref[...])
    zq = z[:, :B_Q_LORA]
    zkv = z[:, B_Q_LORA:B_Q_LORA + B_KV_LORA]
    kr_a = z[:, B_Q_LORA + B_KV_LORA:B_Q_LORA + B_KV_LORA + LANE]
    kr_b = z[:, B_Q_LORA + B_KV_LORA + LANE:]
    qn = (_rms(zq) * qg_ref[...]).astype(BF16)
    kvn = (_rms(zkv) * kvg_ref[...]).astype(BF16)
    q1 = _dot(qn, wq_ref[...])
    q2 = _dot(qn, wqs_ref[...])
    kr = (kr_a * ck_ref[...] + kr_b * sk_ref[...]).astype(BF16)
    k = _dot(kvn, wk_ref[...]) + _dot(kr, place_ref[...])
    cq = cq_ref[...]
    sq = sq_ref[...]
    for hh in range(B_HEADS):
        sl = slice(hh * LANE, (hh + 1) * LANE)
        q_out[:, sl] = (q1[:, sl] * cq + q2[:, sl] * sq).astype(BF16)
    k_out[...] = k.astype(BF16)
    v_out[...] = _dot(kvn, wv_ref[...]).astype(BF16)


def _gelu_tanh(x):
    return 0.5 * x * (1.0 + jnp.tanh(math.sqrt(2.0 / math.pi) * (x + 0.044715 * (x * x * x))))


def _gmlp_kernel(x_ref, g1_ref, mod_ref, wa_ref, wg_ref, vg_ref, ws_ref, bsb_ref,
                 ya_out, gate_out):
    h = _normmod(x_ref[...], g1_ref[...], mod_ref[0], 0).astype(BF16)
    gate_out[...] = _sigmoid(_dot(h, wg_ref[...])).astype(BF16)
    zg = _gelu_tanh(_dot(h, wa_ref[...]))
    u = zg[:, :A_WIDTH]
    v = (_rms(zg[:, A_WIDTH:]) * vg_ref[...]).astype(BF16)
    for c in range(TM // A_CHUNK):
        rows = slice(c * A_CHUNK, (c + 1) * A_CHUNK)
        for g in range(A_GROUPS):
            cols = slice(g * LANE, (g + 1) * LANE)
            mixed = _dot(ws_ref[g], v[rows, cols]) + bsb_ref[:, cols]
            ya_out[rows, cols] = (u[rows, cols] * mixed).astype(BF16)


def _feat_kernel(x_ref, xp_ref, xn_ref, g1_ref, mod_ref, wc_ref, mu_ref,
                 w0_ref, w2_ref, a0_ref, a2_ref, kk_ref, ka_ref, rk_ref, g2_ref,
                 v_out, ab_out, bh_out, kh_out, rb_out, pe_out, bonus_out, g_out,
                 *, seq_starts, seq_ends):
    mod = mod_ref[0]
    g1 = g1_ref[...]
    h = _normmod(x_ref[...], g1, mod, 0).astype(BF16)
    hp = _normmod(xp_ref[...], g1, mod, 0).astype(BF16)
    hn = _normmod(xn_ref[...], g1, mod, 0).astype(BF16)
    tm = x_ref.shape[0]
    row = lax.broadcasted_iota(jnp.int32, (tm, 1), 0)
    first = pl.program_id(0) * tm
    has_prev = 1.0 - functools.reduce(jnp.maximum, [jnp.where(first == s, 1.0, 0.0) for s in seq_starts])
    has_next = 1.0 - functools.reduce(jnp.maximum,
                                      [jnp.where(first + tm - 1 == e, 1.0, 0.0) for e in seq_ends])

    def shifted_proj(cols):
        w = wc_ref[:, cols]
        z = _dot(h, w)
        z_up = jnp.where(row == 0, _dot(hp, w)[7:8] * has_prev, pltpu.roll(z, 1, 0))
        z_dn = jnp.where(row == tm - 1, _dot(hn, w)[0:1] * has_next, pltpu.roll(z, tm - 1, 0))
        return z + mu_ref[:, cols] * (0.5 * (z_up + z_dn) - z)

    r = shifted_proj(slice(0, HP))
    k = shifted_proj(slice(HP, 2 * HP))
    v = shifted_proj(slice(2 * HP, 3 * HP))
    zl = shifted_proj(slice(3 * HP, 3 * HP + 3 * LANE))
    zw = jnp.tanh(zl[:, :LANE]).astype(BF16)
    za = zl[:, LANE:2 * LANE].astype(BF16)
    zg = _sigmoid(zl[:, 2 * LANE:]).astype(BF16)

    kk = k * kk_ref[...]
    ka = ka_ref[...]
    rk = rk_ref[...]
    a = [_sigmoid(a0_ref[d:d + 1] + _dot(za, a2_ref[d])) for d in range(2)]
    kd = [k * (1.0 + (a[d] - 1.0) * ka) for d in range(2)]
    v_out[...] = v.astype(BF16)
    g_out[...] = _dot(zg, g2_ref[...])

    ti = lax.broadcasted_iota(jnp.int32, (CUM_ROWS, CUM_ROWS), 0)
    tj = lax.broadcasted_iota(jnp.int32, (CUM_ROWS, CUM_ROWS), 1)
    shift = SCAN_C.bit_length() - 1
    same = lax.shift_right_logical(ti, shift) == lax.shift_right_logical(tj, shift)
    p_in, p_prev, p_inv = [], [], []
    for d in range(2):
        lw = -DECAY_SCALE * _sigmoid(w0_ref[d:d + 1] + _dot(zw, w2_ref[d]))
        tri = jnp.where(same & ((tj <= ti) if d == 0 else (tj >= ti)), 1.0, 0.0).astype(BF16)
        hi = lw.astype(BF16)
        rest = lw - hi.astype(F32)
        mid = rest.astype(BF16)
        lo = (rest - mid.astype(F32)).astype(BF16)
        cs = jnp.concatenate(
            [_dot(tri, hi[rows]) + _dot(tri, mid[rows]) + _dot(tri, lo[rows])
             for rows in (slice(c0, c0 + CUM_ROWS) for c0 in range(0, tm, CUM_ROWS))], axis=0)
        p_in.append(jnp.exp(cs))
        p_prev.append(jnp.exp(cs - lw))
        p_inv.append(jnp.exp(-cs))
        kh_out[d] = (kd[d] * p_inv[d]).astype(BF16)
        rb_out[d] = (r * p_in[d]).astype(BF16)
        for cc in range(tm // SCAN_C):
            last = cc * SCAN_C + (SCAN_C - 1 if d == 0 else 0)
            pe_out[d, cc * 8:(cc + 1) * 8, :] = jnp.broadcast_to(p_in[d][last:last + 1], (8, HP))

    for hh in range(C_HEADS):
        sl = slice(hh * LANE, (hh + 1) * LANE)
        kh = kk[:, sl]
        kh = kh * lax.rsqrt(jnp.maximum(jnp.sum(kh * kh, axis=-1, keepdims=True), 1e-12))
        rh = r[:, sl] * rk[:, sl]
        bon = jnp.zeros((tm, 1), F32)
        for d in range(2):
            ab_out[d, :, sl] = (-kh * p_prev[d][:, sl]).astype(BF16)
            bh_out[d, :, sl] = (kh * a[d][:, sl] * p_inv[d][:, sl]).astype(BF16)
            bon = bon + jnp.sum(rh * kd[d][:, sl], axis=-1, keepdims=True)
        bonus_out[:, sl] = bon * v[:, sl]


def _mm(a, b, dims=((1,), (0,))):
    return lax.dot_general(a.astype(BF16), b.astype(BF16), (dims, ((), ())),
                           preferred_element_type=F32)


_NT = ((1,), (1,))
_TN = ((0,), (0,))


def _scan_kernel(vf_ref, vb_ref, abf_ref, abb_ref, bhf_ref, bhb_ref, khf_ref, khb_ref,
                 rbf_ref, rbb_ref, pef_ref, peb_ref, of_ref, ob_ref, s_ref):
    @pl.when(pl.program_id(1) == 0)
    def _():
        s_ref[...] = jnp.zeros_like(s_ref)

    c = SCAN_C
    row = lax.broadcasted_iota(jnp.int32, (c, c), 0)
    col = lax.broadcasted_iota(jnp.int32, (c, c), 1)
    refs = ((vf_ref, abf_ref, bhf_ref, khf_ref, rbf_ref, pef_ref, of_ref),
            (vb_ref, abb_ref, bhb_ref, khb_ref, rbb_ref, peb_ref, ob_ref))

    chains = [(d, hh) for d in range(2) for hh in range(C_HEADS)]

    def sl(hh):
        return slice(hh * LANE, (hh + 1) * LANE)

    al = lambda ch: refs[ch[0]][1][0, :, sl(ch[1])]
    bh = lambda ch: refs[ch[0]][2][0, :, sl(ch[1])]
    kh = lambda ch: refs[ch[0]][3][0, :, sl(ch[1])]
    rb = lambda ch: refs[ch[0]][4][0, :, sl(ch[1])]
    vv = lambda ch: refs[ch[0]][0][:, sl(ch[1])]
    pe = lambda ch: refs[ch[0]][5][0, 0:1, sl(ch[1])]
    strict = lambda ch: (col < row) if ch[0] == 0 else (col > row)
    incl = lambda ch: (col <= row) if ch[0] == 0 else (col >= row)

    xb = {ch: _mm(jnp.concatenate([al(ch), rb(ch)], axis=0), bh(ch), _NT) for ch in chains}
    xk = {ch: _mm(jnp.concatenate([al(ch), rb(ch)], axis=0), kh(ch), _NT) for ch in chains}
    l_pow = {ch: jnp.where(strict(ch), xb[ch][:c], 0.0) for ch in chains}
    u = {ch: _mm(jnp.where(strict(ch), xk[ch][:c], 0.0), vv(ch)) + _mm(al(ch), s_ref[ch], _NT)
         for ch in chains}
    for it in range(6):
        u = {ch: u[ch] + _mm(l_pow[ch], u[ch]) for ch in chains}
        if it < 5:
            l_pow = {ch: _mm(l_pow[ch], l_pow[ch]) for ch in chains}
    for ch in chains:
        o_ref = refs[ch[0]][6]
        o_ref[:, sl(ch[1])] = (_mm(rb(ch), s_ref[ch], _NT)
                               + _mm(jnp.where(incl(ch), xb[ch][c:], 0.0), u[ch])
                               + _mm(jnp.where(incl(ch), xk[ch][c:], 0.0), vv(ch)))
    for ch in chains:
        upd = _mm(jnp.concatenate([u[ch].astype(BF16), vv(ch)], axis=0),
                  jnp.concatenate([bh(ch), kh(ch)], axis=0), _TN)
        s_ref[ch] = (s_ref[ch] + upd) * pe(ch)


def _attn_ctx_kernel(q_ref, k_ref, v_ref, o_ref):
    s = _dot_nt(q_ref[...], k_ref[...])
    p = jnp.exp2(s - jnp.max(s, axis=-1, keepdims=True))
    l = jnp.sum(p, axis=-1, keepdims=True)
    o_ref[...] = (_dot(p.astype(BF16), v_ref[...]) / l).astype(BF16)


def _attn_kernel(q_ref, kc_ref, vc_ref, kl_ref, vl_ref, o_ref, *, k_chunk):
    q = q_ref[...]
    parts = [(kc_ref, vc_ref, 0, kc_ref.shape[0])]
    parts += [(kl_ref, vl_ref, c0, k_chunk) for c0 in range(0, kl_ref.shape[0], k_chunk)]
    scores = lambda part: _dot_nt(q, part[0][part[2]:part[2] + part[3], :])
    m = jnp.full((q.shape[0], 1), -jnp.inf, F32)
    l = jnp.zeros((q.shape[0], 1), F32)
    acc = jnp.zeros((q.shape[0], LANE), F32)
    s_next = scores(parts[0])
    for n, (_, v_ref, c0, size) in enumerate(parts):
        s = s_next
        if n + 1 < len(parts):
            s_next = scores(parts[n + 1])
        m_new = jnp.maximum(m, jnp.max(s, axis=-1, keepdims=True))
        p = jnp.exp2(s - m_new)
        corr = jnp.exp2(m - m_new)
        l = l * corr + jnp.sum(p, axis=-1, keepdims=True)
        acc = acc * corr + _dot(p.astype(BF16), v_ref[c0:c0 + size, :])
        m = m_new
    o_ref[...] = (acc / l).astype(BF16)


def _merge_kernel(of_ref, ob_ref, bonus_ref, g_ref, ya_ref, ybl_ref, ybc_ref, gate_ref, x_ref,
                  mod_ref, gng_ref, gnb_ref, wa_ref, wb_ref, wc_ref, wo_ref, n2_ref,
                  rw_ref, rb_ref, xn_out, h2_out, comb_out, *, n_lat_tiles):
    mod = mod_ref[0]
    yb = jnp.where(pl.program_id(0) < n_lat_tiles, ybl_ref[...], ybc_ref[...])
    o = of_ref[...] + ob_ref[...]
    lane = lax.broadcasted_iota(jnp.int32, (1, LANE), 1)
    real = lane < C_HEAD
    ycs = []
    for hh in range(C_HEADS):
        sl = slice(hh * LANE, (hh + 1) * LANE)
        oh = o[:, sl]
        mean = jnp.sum(oh, axis=-1, keepdims=True) * (1.0 / C_HEAD)
        dev = jnp.where(real, oh - mean, 0.0)
        var = jnp.sum(dev * dev, axis=-1, keepdims=True) * (1.0 / C_HEAD)
        y = dev * lax.rsqrt(var + C_GN_EPS) * gng_ref[:, sl] + gnb_ref[:, sl]
        ycs.append(((y + bonus_ref[:, sl]) * g_ref[:, sl]).astype(BF16))
    yc = jnp.concatenate(ycs, axis=1)
    gate = gate_ref[...].astype(F32)
    merged = (gate[:, :D_MODEL] * _dot(ya_ref[...], wa_ref[...])
              + gate[:, D_MODEL:2 * D_MODEL] * _dot(yb, wb_ref[...])
              + gate[:, 2 * D_MODEL:] * _dot(yc, wc_ref[...]))
    xn = x_ref[...] + mod[2:3] * _dot(merged.astype(BF16), wo_ref[...])
    xn_out[...] = xn
    h2 = _normmod(xn, n2_ref[...], mod, 3).astype(BF16)
    h2_out[...] = h2

    scores = 1.0 / (1.0 + jnp.exp(-_dot(h2, rw_ref[...])))
    neg = -jnp.inf
    sel = jnp.where(lane < N_EXPERTS, scores + rb_ref[...], neg)

    lane_f = lane.astype(F32)

    def top1(s):
        mx = jnp.max(s, axis=-1, keepdims=True)
        idx = jnp.min(jnp.where(s == mx, lane_f, float(LANE)), axis=-1, keepdims=True)
        return mx, idx.astype(jnp.int32)

    best = None
    for g in range(N_GROUPS):
        in_g = (lane >= g * EXPERTS_PER_GROUP) & (lane < (g + 1) * EXPERTS_PER_GROUP)
        s = jnp.where(in_g, sel, neg)
        m1, i1 = top1(s)
        m2, _ = top1(jnp.where(lane == i1, neg, s))
        score = m1 + m2
        if best is None:
            best, gidx = score, jnp.zeros_like(i1)
        else:
            better = score > best
            gidx = jnp.where(better, g, gidx)
            best = jnp.where(better, score, best)
    lo = gidx * EXPERTS_PER_GROUP
    s = jnp.where((lane >= lo) & (lane < lo + EXPERTS_PER_GROUP), sel, neg)
    _, i1 = top1(s)
    _, i2 = top1(jnp.where(lane == i1, neg, s))
    picked = jnp.where((lane == i1) | (lane == i2), scores, 0.0)
    comb_out[...] = picked / jnp.sum(picked, axis=-1, keepdims=True)


def _moe_kernel(cnt_ref, h2_ref, comb_ref, xn_ref, *rest, last):
    mod_refs = rest[:MOE_SUB]
    w1_ref, w3_ref, w2_ref, tri_ref, fg_ref, out_ref, rank_ref, rank_t_ref, comb_t_ref = rest[MOE_SUB:]
    i = pl.program_id(0)
    e = pl.program_id(1)
    tm = h2_ref.shape[0]
    lane = lax.broadcasted_iota(jnp.int32, (1, LANE), 1)

    @pl.when(e == 0)
    def _():
        comb = comb_ref[...]
        picked = jnp.where(comb > 0.0, 1.0, 0.0).astype(BF16)
        rank = _dot(tri_ref[...], picked)
        rank_ref[...] = rank
        rank_t_ref[...] = rank.T
        comb_t_ref[...] = comb.T
        out_ref[...] = jnp.zeros_like(out_ref)

    comb_e = jnp.sum(jnp.where(lane == e, comb_ref[...], 0.0), axis=-1, keepdims=True)
    rank_col = jnp.sum(jnp.where(lane == e, rank_ref[...], 0.0), axis=-1, keepdims=True)
    rank_row = rank_t_ref[pl.ds(e, 1), :]
    picked_row = comb_t_ref[pl.ds(e, 1), :] > 0.0
    slot_col = lax.broadcasted_iota(jnp.int32, (MOE_CAP, 1), 0)
    slot_row = lax.broadcasted_iota(jnp.int32, (1, MOE_CAP), 1)
    n_blocks = lax.shift_right_logical(cnt_ref[i, e] + (MOE_CAP - 1), MOE_CAP.bit_length() - 1)

    def block(blk, carry):
        base = blk * MOE_CAP
        gather = jnp.where(picked_row & (rank_row == (slot_col + base).astype(F32)), 1.0, 0.0).astype(BF16)
        xg = _dot(gather, h2_ref[...]).astype(BF16)
        a = _dot(xg, w1_ref[0])
        he = (a * _sigmoid(a) * _dot(xg, w3_ref[0])).astype(BF16)
        y = _dot(he, w2_ref[0]).astype(BF16)
        scatter = jnp.where((comb_e > 0.0) & (rank_col == (slot_row + base).astype(F32)), 1.0, 0.0).astype(BF16)
        out_ref[...] += comb_e * _dot(scatter, y)
        return carry

    lax.fori_loop(0, n_blocks, block, 0)

    @pl.when(e == N_EXPERTS - 1)
    def _():
        for s, m_ref in enumerate(mod_refs):
            rows = slice(s * SEG_ROWS, (s + 1) * SEG_ROWS)
            y = xn_ref[rows] + m_ref[0][5:6] * out_ref[rows]
            if last:
                y = _rms(y) * fg_ref[...]
            out_ref[rows] = y


def _head_pad_cols(w, width):
    lead = w.shape[:-1]
    w = w.reshape(lead + (C_HEADS, width))
    w = jnp.pad(w, [(0, 0)] * len(lead) + [(0, 0), (0, LANE - width)])
    return w.reshape(lead + (HP,))


def _head_pad_rows(w, width):
    n = w.shape[-1]
    w = w.reshape(C_HEADS, width, n)
    w = jnp.pad(w, ((0, 0), (0, LANE - width), (0, 0)))
    return w.reshape(HP, n)


def _split_rope(w):
    lead = w.shape[:-1]
    w = w.reshape(lead + (2, 2, ROPE_FREQS))
    x1 = w[..., 0, :].reshape(lead + (2 * ROPE_FREQS,))
    x2 = w[..., 1, :].reshape(lead + (2 * ROPE_FREQS,))
    return x1, x2


def _rope_tables(n_ctx, t):
    rows = t // GRID_W
    row = jnp.repeat(jnp.arange(rows, dtype=F32), GRID_W)
    col = jnp.tile(jnp.arange(GRID_W, dtype=F32), rows)
    inv = jnp.power(ROPE_BASE, -jnp.arange(ROPE_FREQS, dtype=F32) / ROPE_FREQS)
    ang = jnp.concatenate([row[:, None] * inv, col[:, None] * inv], axis=1)
    cos = jnp.concatenate([jnp.cos(ang), jnp.ones((n_ctx, 2 * ROPE_FREQS), F32)], axis=0)
    sin = jnp.concatenate([jnp.sin(ang), jnp.zeros((n_ctx, 2 * ROPE_FREQS), F32)], axis=0)
    n = n_ctx + t
    one = jnp.ones((n, B_NOPE), F32)
    zero = lambda w: jnp.zeros((n, w), F32)
    qs = MLA_SCALE * math.log2(math.e)
    cq = jnp.concatenate([one, cos, cos, zero(LANE - B_NOPE - B_ROPE)], axis=1) * qs
    sq = jnp.concatenate([zero(B_NOPE), -sin, sin, zero(LANE - B_NOPE - B_ROPE)], axis=1) * qs
    ck = jnp.concatenate([cos, cos, zero(LANE - B_ROPE)], axis=1)
    sk = jnp.concatenate([-sin, sin, zero(LANE - B_ROPE)], axis=1)
    return cq, sq, ck, sk


def _layer_weights(p, l):
    w = {}
    w_in = p['w_in'][l]
    d = w_in.shape[0]
    o = 0
    w_q = w_in[:, o:o + B_Q_LORA]; o += B_Q_LORA
    w_kv = w_in[:, o:o + B_KV_LORA]; o += B_KV_LORA
    w_kr = w_in[:, o:o + B_ROPE]; o += B_ROPE
    w_c = w_in[:, o:o + C_COLS]; o += C_COLS
    w_a = w_in[:, o:o + A_COLS]; o += A_COLS
    w_g = w_in[:, o:o + G_COLS]

    k1, k2 = _split_rope(w_kr)
    zpad = jnp.zeros((d, LANE - B_ROPE), F32)
    w['win_b'] = jnp.concatenate([w_q, w_kv, k1, k2, zpad, k2, k1, zpad], axis=1).astype(BF16)

    wuq = p['b_w_uq'][l].reshape(B_Q_LORA, B_HEADS, B_NOPE + B_ROPE)
    q1, q2 = _split_rope(wuq[..., B_NOPE:])
    qz = jnp.zeros((B_Q_LORA, B_HEADS, LANE - B_NOPE - B_ROPE), F32)
    w['wq'] = jnp.concatenate([wuq[..., :B_NOPE], q1, q2, qz], axis=-1).reshape(B_Q_LORA, HP).astype(BF16)
    w['wqs'] = jnp.concatenate([wuq[..., :B_NOPE], q2, q1, qz], axis=-1).reshape(B_Q_LORA, HP).astype(BF16)
    wukv = p['b_w_ukv'][l].reshape(B_KV_LORA, B_HEADS, B_NOPE + B_VDIM)
    w['wk'] = _head_pad_cols(wukv[..., :B_NOPE].reshape(B_KV_LORA, -1), B_NOPE).astype(BF16)
    w['wv'] = _head_pad_cols(wukv[..., B_NOPE:].reshape(B_KV_LORA, -1), B_VDIM).astype(BF16)
    place = jnp.zeros((LANE, B_HEADS, LANE), F32)
    idx = jnp.arange(B_ROPE)
    place = place.at[idx, :, B_NOPE + idx].set(1.0)
    w['place'] = place.reshape(LANE, HP).astype(BF16)
    w['qg'] = p['b_q_norm'][l][None]
    w['kvg'] = p['b_kv_norm'][l][None]

    w['wa'] = w_a.astype(BF16)
    w['wg'] = w_g.astype(BF16)
    w['vg'] = p['a_v_gain'][l][None]
    w['ws'] = p['a_ws'][l].astype(BF16)
    w['bsb'] = jnp.broadcast_to(p['a_bs'][l].T[:, :, None], (A_CHUNK, A_GROUPS, LANE)).reshape(A_CHUNK, A_WIDTH)

    cw = C_WIDTH
    pieces = [_head_pad_cols(w_c[:, i * cw:(i + 1) * cw], C_HEAD) for i in range(3)]
    w['wc'] = jnp.concatenate(pieces + [w_c[:, 3 * cw:]], axis=1).astype(BF16)
    mu = p['c_mu'][l]
    w['mu'] = jnp.concatenate([_head_pad_cols(mu[i * cw:(i + 1) * cw], C_HEAD) for i in range(3)]
                              + [mu[3 * cw:]])[None]

    def lora_block(m, rank):
        m = _head_pad_cols(m, C_HEAD)
        z = jnp.zeros_like(m[0])
        return jnp.stack([jnp.concatenate([m[0], z], axis=0), jnp.concatenate([z, m[1]], axis=0)]).astype(BF16)

    w['w2'] = lora_block(p['c_w2'][l], C_DECAY_LORA)
    w['a2'] = lora_block(p['c_a2'][l], C_AAA_LORA)
    w['w0'] = _head_pad_cols(p['c_w0'][l], C_HEAD)
    w['a0'] = _head_pad_cols(p['c_a0'][l], C_HEAD)
    w['kk'] = _head_pad_cols(p['c_k_k'][l], C_HEAD)[None]
    w['ka'] = _head_pad_cols(p['c_k_a'][l], C_HEAD)[None]
    w['rk'] = _head_pad_cols(p['c_r_k'][l].reshape(-1), C_HEAD)[None]
    w['g2'] = _head_pad_cols(p['c_g2'][l], C_HEAD).astype(BF16)
    w['gng'] = _head_pad_cols(p['c_gn_g'][l], C_HEAD)[None]
    w['gnb'] = _head_pad_cols(p['c_gn_b'][l], C_HEAD)[None]

    w['wup_a'] = p['w_up_a'][l].astype(BF16)
    w['wup_b'] = _head_pad_rows(p['w_up_b'][l], B_VDIM).astype(BF16)
    w['wup_c'] = _head_pad_rows(p['w_up_c'][l], C_HEAD).astype(BF16)
    w['wo'] = p['w_o'][l].astype(BF16)
    w['g1'] = p['norm1_g'][l][None]
    w['n2'] = p['norm2_g'][l][None]
    w['w1'] = p['moe_w1'][l].astype(BF16)
    w['w3'] = p['moe_w3'][l].astype(BF16)
    w['w2e'] = p['moe_w2'][l].astype(BF16)
    return w


def _layer(xs, mods, w, tabs, rw, rb, fg, *, bn, t, n_ctx, last):
    m, d = xs.shape
    n_tiles = m // TM
    n_lat = bn * t
    n_lat_tiles = n_lat // TM
    tpl = t // TM
    row = lambda i: (i, 0)
    seg_of = lambda blk: jnp.where(blk < n_lat // SEG_ROWS, 1 + blk // (t // SEG_ROWS), 0)
    seg = lambda i: (seg_of(i * (TM // SEG_ROWS)), 0, 0)
    x_spec = pl.BlockSpec((TM, d), row)
    mod_spec = pl.BlockSpec((1, 6, d), seg)
    hp_spec = pl.BlockSpec((TM, HP), row)
    hp2_spec = pl.BlockSpec((2, TM, HP), lambda i: (0, i, 0))
    hp_f32 = jax.ShapeDtypeStruct((m, HP), F32)
    hp_bf16 = jax.ShapeDtypeStruct((m, HP), BF16)
    hp2_bf16 = jax.ShapeDtypeStruct((2, m, HP), BF16)
    cq, sq, ck, sk = tabs
    tab_spec = pl.BlockSpec(
        (TM, LANE), lambda i: (jnp.where(i < n_lat_tiles, i % tpl, tpl + i - n_lat_tiles), 0))

    q, k, v = pl.pallas_call(
        _mla_kernel,
        grid=(n_tiles,),
        in_specs=[x_spec, _full((1, d)), mod_spec, _full(w['win_b'].shape), _full((1, B_Q_LORA)),
                  _full((1, B_KV_LORA)), _full(w['wq'].shape), _full(w['wqs'].shape),
                  _full(w['wk'].shape), _full(w['wv'].shape), _full(w['place'].shape),
                  tab_spec, tab_spec, tab_spec, tab_spec],
        out_specs=[hp_spec, hp_spec, hp_spec],
        out_shape=[hp_bf16, hp_bf16, hp_bf16],
        compiler_params=_params(1),
        name="mla_proj",
    )(xs, w['g1'], mods, w['win_b'], w['qg'], w['kvg'], w['wq'], w['wqs'], w['wk'], w['wv'],
      w['place'], cq, sq, ck, sk)

    ya, gate = pl.pallas_call(
        _gmlp_kernel,
        grid=(n_tiles,),
        in_specs=[x_spec, _full((1, d)), mod_spec, _full(w['wa'].shape), _full(w['wg'].shape),
                  _full((1, A_WIDTH)), _full(w['ws'].shape), _full(w['bsb'].shape)],
        out_specs=[pl.BlockSpec((TM, A_WIDTH), row), pl.BlockSpec((TM, G_COLS), row)],
        out_shape=[jax.ShapeDtypeStruct((m, A_WIDTH), BF16), jax.ShapeDtypeStruct((m, G_COLS), BF16)],
        compiler_params=_params(1),
        name="gmlp_gates",
    )(xs, w['g1'], mods, w['wa'], w['wg'], w['vg'], w['ws'], w['bsb'])

    tf = TM_FEAT
    sub = tf // 8
    seq_starts = tuple(b * t for b in range(bn)) + tuple(n_lat + b * n_ctx for b in range(bn))
    seq_ends = tuple(s - 1 for s in seq_starts[1:]) + (m - 1,)
    assert all(s % tf == 0 for s in seq_starts)
    f_hp = pl.BlockSpec((tf, HP), row)
    f_hp2 = pl.BlockSpec((2, tf, HP), lambda i: (0, i, 0))
    pe_spec = pl.BlockSpec((2, 8 * (tf // SCAN_C), HP), lambda i: (0, i, 0))
    pe_shape = jax.ShapeDtypeStruct((2, 8 * (m // SCAN_C), HP), F32)
    vv, ab, bh, kh, rbar, pe, bonus, gg = pl.pallas_call(
        functools.partial(_feat_kernel, seq_starts=seq_starts, seq_ends=seq_ends),
        grid=(m // tf,),
        in_specs=[pl.BlockSpec((tf, d), row),
                  pl.BlockSpec((8, d), lambda i: (jnp.maximum(i * sub - 1, 0), 0)),
                  pl.BlockSpec((8, d), lambda i: (jnp.minimum((i + 1) * sub, m // 8 - 1), 0)),
                  _full((1, d)), pl.BlockSpec((1, 6, d), lambda i: (seg_of(i * (tf // SEG_ROWS)), 0, 0)),
                  _full(w['wc'].shape), _full(w['mu'].shape),
                  _full((2, HP)), _full(w['w2'].shape), _full((2, HP)), _full(w['a2'].shape),
                  _full((1, HP)), _full((1, HP)), _full((1, HP)), _full(w['g2'].shape)],
        out_specs=[f_hp, f_hp2, f_hp2, f_hp2, f_hp2, pe_spec, f_hp, f_hp],
        out_shape=[hp_bf16, hp2_bf16, hp2_bf16, hp2_bf16, hp2_bf16, pe_shape, hp_f32, hp_f32],
        compiler_params=_params(1),
        name="rwkv_features",
    )(xs, xs, xs, w['g1'], mods, w['wc'], w['mu'], w['w0'], w['w2'], w['a0'], w['a2'],
      w['kk'], w['ka'], w['rk'], w['g2'])

    nctx_c = n_ctx // SCAN_C
    nlat_c = t // SCAN_C
    nc = nctx_c + nlat_c

    def chunk_of(dd, b, j):
        ctx_base = n_lat // SCAN_C + b * nctx_c
        lat_base = b * nlat_c
        if dd == 0:
            return jnp.where(j < nctx_c, ctx_base + j, lat_base + j - nctx_c)
        return jnp.where(j < nctx_c, ctx_base + nctx_c - 1 - j, lat_base + nc - 1 - j)

    def tok_spec(dd):
        return pl.BlockSpec((SCAN_C, HP), lambda b, j: (chunk_of(dd, b, j), 0))

    def dir_spec(dd, rows=SCAN_C):
        return pl.BlockSpec((1, rows, HP), lambda b, j: (dd, chunk_of(dd, b, j), 0))

    o_f, o_b = pl.pallas_call(
        _scan_kernel,
        grid=(bn, nc),
        in_specs=[tok_spec(0), tok_spec(1)] + [dir_spec(dd) for _ in range(4) for dd in range(2)]
                 + [dir_spec(0, 8), dir_spec(1, 8)],
        out_specs=[tok_spec(0), tok_spec(1)],
        out_shape=[hp_f32, hp_f32],
        scratch_shapes=[pltpu.VMEM((2, C_HEADS, LANE, LANE), F32)],
        compiler_params=_params(2),
        name="rwkv_scan",
    )(vv, vv, ab, ab, bh, bh, kh, kh, rbar, rbar, pe, pe)

    ctx_blk = n_lat // n_ctx
    tq = ATTN_TQ
    ctx_kv = pl.BlockSpec((n_ctx, LANE), lambda b, hh, qi: (ctx_blk + b, hh))
    lat_kv = pl.BlockSpec((t, LANE), lambda b, hh, qi: (b, hh))
    q_lat = pl.BlockSpec((tq, LANE), lambda b, hh, qi: (b * (t // tq) + qi, hh))
    yb_lat = pl.pallas_call(
        functools.partial(_attn_kernel, k_chunk=ATTN_KC),
        grid=(bn, B_HEADS, t // tq),
        in_specs=[q_lat, ctx_kv, ctx_kv, lat_kv, lat_kv],
        out_specs=q_lat,
        out_shape=jax.ShapeDtypeStruct((n_lat, HP), BF16),
        compiler_params=_params(3),
        name="mla_attention",
    )(q, k, v, k, v)
    ctx_q = pl.BlockSpec((n_ctx, LANE), lambda b, hh: (ctx_blk + b, hh))
    yb_ctx = pl.pallas_call(
        _attn_ctx_kernel,
        grid=(bn, B_HEADS),
        in_specs=[ctx_q, ctx_q, ctx_q],
        out_specs=pl.BlockSpec((n_ctx, LANE), lambda b, hh: (b, hh)),
        out_shape=jax.ShapeDtypeStruct((bn * n_ctx, HP), BF16),
        compiler_params=_params(2),
        name="mla_attention_ctx",
    )(q, k, v)

    n_ctx_tiles = n_tiles - n_lat_tiles
    xn, h2, comb = pl.pallas_call(
        functools.partial(_merge_kernel, n_lat_tiles=n_lat_tiles),
        grid=(n_tiles,),
        in_specs=[hp_spec, hp_spec, hp_spec, hp_spec, pl.BlockSpec((TM, A_WIDTH), row),
                  pl.BlockSpec((TM, HP), lambda i: (jnp.minimum(i, n_lat_tiles - 1), 0)),
                  pl.BlockSpec((TM, HP), lambda i: (jnp.clip(i - n_lat_tiles, 0, n_ctx_tiles - 1), 0)),
                  pl.BlockSpec((TM, G_COLS), row), x_spec, mod_spec, _full((1, HP)), _full((1, HP)),
                  _full(w['wup_a'].shape), _full(w['wup_b'].shape), _full(w['wup_c'].shape),
                  _full(w['wo'].shape), _full((1, d)), _full(rw.shape), _full(rb.shape)],
        out_specs=[x_spec, x_spec, pl.BlockSpec((TM, LANE), row)],
        out_shape=[jax.ShapeDtypeStruct((m, d), F32), jax.ShapeDtypeStruct((m, d), BF16),
                   jax.ShapeDtypeStruct((m, LANE), F32)],
        compiler_params=_params(1),
        name="merge_router",
    )(o_f, o_b, bonus, gg, ya, yb_lat, yb_ctx, gate, xs, mods, w['gng'], w['gnb'], w['wup_a'], w['wup_b'],
      w['wup_c'], w['wo'], w['n2'], rw, rb)

    tm_moe = MOE_SUB * SEG_ROWS
    counts = jnp.sum((comb > 0.0).reshape(m // tm_moe, tm_moe, LANE), axis=1, dtype=jnp.int32)
    tri = jnp.tril(jnp.ones((tm_moe, tm_moe), BF16), -1)
    mrow = lambda i, e, cnt: (i, 0)
    once = pl.Buffered(1)
    mseg = lambda s: pl.BlockSpec((1, 6, d), lambda i, e, cnt: (seg_of(i * MOE_SUB + s), 0, 0))
    wexp = lambda shape: pl.BlockSpec((1,) + shape, lambda i, e, cnt: (e, 0, 0))
    grid_spec = pltpu.PrefetchScalarGridSpec(
        num_scalar_prefetch=1,
        grid=(m // tm_moe, N_EXPERTS),
        in_specs=[pl.BlockSpec((tm_moe, d), mrow), pl.BlockSpec((tm_moe, LANE), mrow),
                  pl.BlockSpec((tm_moe, d), mrow, pipeline_mode=once)]
                 + [mseg(s) for s in range(MOE_SUB)]
                 + [wexp((d, D_EXPERT)), wexp((d, D_EXPERT)), wexp((D_EXPERT, d)),
                    pl.BlockSpec((tm_moe, tm_moe), lambda i, e, cnt: (0, 0), pipeline_mode=once),
                    pl.BlockSpec((1, d), lambda i, e, cnt: (0, 0))],
        out_specs=pl.BlockSpec((tm_moe, d), mrow),
        scratch_shapes=[pltpu.VMEM((tm_moe, LANE), F32), pltpu.VMEM((LANE, tm_moe), F32),
                        pltpu.VMEM((LANE, tm_moe), F32)],
    )
    return pl.pallas_call(
        functools.partial(_moe_kernel, last=last),
        grid_spec=grid_spec,
        out_shape=jax.ShapeDtypeStruct((m, d), F32),
        compiler_params=_params(2),
        name="moe",
    )(counts, h2, comb, xn, *([mods] * MOE_SUB), w['w1'], w['w3'], w['w2e'], tri, fg)


def kernel(x, c, ctx, c_ctx, mod_w, mod_b, norm1_g, norm2_g, w_in, a_v_gain, a_ws, a_bs, b_q_norm, b_w_uq, b_kv_norm, b_w_ukv, c_mu, c_w0, c_w2, c_a0, c_a2, c_g2, c_k_k, c_k_a, c_r_k, c_gn_g, c_gn_b, w_up_a, w_up_b, w_up_c, w_o, router_w, router_b, moe_w1, moe_w3, moe_w2, final_g):
    p = dict(w_in=w_in, a_v_gain=a_v_gain, a_ws=a_ws, a_bs=a_bs, b_q_norm=b_q_norm, b_w_uq=b_w_uq,
             b_kv_norm=b_kv_norm, b_w_ukv=b_w_ukv, c_mu=c_mu, c_w0=c_w0, c_w2=c_w2, c_a0=c_a0,
             c_a2=c_a2, c_g2=c_g2, c_k_k=c_k_k, c_k_a=c_k_a, c_r_k=c_r_k, c_gn_g=c_gn_g,
             c_gn_b=c_gn_b, w_up_a=w_up_a, w_up_b=w_up_b, w_up_c=w_up_c, w_o=w_o,
             norm1_g=norm1_g, norm2_g=norm2_g, moe_w1=moe_w1, moe_w3=moe_w3, moe_w2=moe_w2)
    bn, t, d = x.shape
    n_ctx = ctx.shape[1]
    depth = mod_w.shape[0]
    m = bn * (t + n_ctx)
    assert n_ctx == SEG_ROWS and (bn * n_ctx) % TM == 0 and t % TM == 0 and t % ATTN_TQ == 0
    assert t % ATTN_KC == 0 and m % (MOE_SUB * SEG_ROWS) == 0 and 1 + bn <= 8

    cvec = jnp.concatenate([c_ctx[None], c, jnp.zeros((8 - 1 - bn, d), F32)], axis=0)
    mods = _modulation(cvec, mod_w, mod_b).reshape(depth, 8, 6, d)

    tabs = _rope_tables(bn * n_ctx, t)
    rw = jnp.pad(router_w, ((0, 0), (0, LANE - N_EXPERTS))).astype(BF16)
    rb = jnp.pad(router_b, (0, LANE - N_EXPERTS))[None]
    fg = final_g[None]

    xs = jnp.concatenate([x.reshape(bn * t, d), ctx.reshape(bn * n_ctx, d)], axis=0)
    for l in range(depth):
        xs = _layer(xs, mods[l], _layer_weights(p, l), tabs, rw, rb, fg,
                    bn=bn, t=t, n_ctx=n_ctx, last=(l == depth - 1))
    return xs[:bn * t].reshape(bn, t, d)
```

```python
import functools
import math

import jax
import jax.numpy as jnp
from jax import lax
from jax.experimental import pallas as pl
from jax.experimental.pallas import tpu as pltpu

F32 = jnp.float32
BF16 = jnp.bfloat16
HIGHEST = lax.Precision.HIGHEST

D_MODEL = 1024
GRID_W = 64
EPS = 1e-6

A_WIDTH = 512
A_GROUPS = 4
A_CHUNK = 128

B_HEADS = 8
B_Q_LORA = 384
B_KV_LORA = 256
B_NOPE = 64
B_ROPE = 32
B_VDIM = 64
ROPE_BASE = 10000.0
ROPE_FREQS = B_ROPE // 4
MLA_SCALE = (B_NOPE + B_ROPE) ** -0.5

C_HEADS = 8
C_HEAD = 64
C_WIDTH = C_HEADS * C_HEAD
C_DECAY_LORA = 64
C_AAA_LORA = 64
C_GATE_LORA = 128
C_GN_EPS = 64e-5
DECAY_SCALE = 0.6065306597126334

B_COLS = B_Q_LORA + B_KV_LORA + B_ROPE
C_COLS = 3 * C_WIDTH + 2 * C_DECAY_LORA + 2 * C_AAA_LORA + C_GATE_LORA
A_COLS = 2 * A_WIDTH
G_COLS = 3 * D_MODEL

N_EXPERTS = 16
N_GROUPS = 4
EXPERTS_PER_GROUP = 4
D_EXPERT = 512

LANE = 128
HP = C_HEADS * LANE
TM = 512
TM_FEAT = 256
SEG_ROWS = 256
SCAN_C = 64
SCAN_STEP = 2
CUM_ROWS = 256
MOE_SUB = 6
MOE_CAP = 256
ATTN_TQ = 512
ATTN_KC = 1024
VMEM_LIMIT = 52 * 1024 * 1024


def _dot(a, b, precision=None):
    return jnp.dot(a, b, preferred_element_type=F32, precision=precision)


def _dot_nt(a, b, precision=None):
    return lax.dot_general(a, b, (((1,), (1,)), ((), ())),
                           preferred_element_type=F32, precision=precision)


def _dot_tn(a, b, precision=None):
    return lax.dot_general(a, b, (((0,), (0,)), ((), ())),
                           preferred_element_type=F32, precision=precision)


def _rms(x, eps=EPS):
    return x * lax.rsqrt(jnp.mean(x * x, axis=-1, keepdims=True) + eps)


def _normmod(x, g, mod, first):
    return _rms(x) * g * (1.0 + mod[first + 1:first + 2]) + mod[first:first + 1]


def _sigmoid(x):
    return 0.5 * jnp.tanh(0.5 * x) + 0.5


def _params(n_grid):
    return pltpu.CompilerParams(
        dimension_semantics=("arbitrary",) * n_grid, vmem_limit_bytes=VMEM_LIMIT)


def _full(shape):
    n = len(shape)
    return pl.BlockSpec(shape, lambda *_: (0,) * n)


def _mod_kernel(c_ref, w_ref, b_ref, o_ref):
    c = c_ref[...]
    act = (c * _sigmoid(c)).astype(BF16)
    o_ref[0] = _dot(act, w_ref[0].astype(BF16)) + b_ref[0]


def _modulation(cvec, mod_w, mod_b):
    depth, d, n = mod_w.shape
    tn = 1536
    return pl.pallas_call(
        _mod_kernel,
        grid=(depth, n // tn),
        in_specs=[
            pl.BlockSpec((8, d), lambda l, j: (0, 0)),
            pl.BlockSpec((1, d, tn), lambda l, j: (l, 0, j)),
            pl.BlockSpec((1, 1, tn), lambda l, j: (l, 0, j)),
        ],
        out_specs=pl.BlockSpec((1, 8, tn), lambda l, j: (l, 0, j)),
        out_shape=jax.ShapeDtypeStruct((depth, 8, n), F32),
        compiler_params=_params(2),
        name="modulation",
    )(cvec, mod_w, mod_b.reshape(depth, 1, n))


def _mla_kernel(x_ref, g1_ref, mod_ref, win_ref, qg_ref, kvg_ref, wq_ref, wqs_ref,
                wk_ref, wv_ref, place_ref, cq_ref, sq_ref, ck_ref, sk_ref,
                q_out, k_out, v_out):
    h = _normmod(x_ref[...], g1_ref[...], mod_ref[0], 0).astype(BF16)
    z = _dot(h, win_ref[...])
    zq = z[:, :B_Q_LORA]
    zkv = z[:, B_Q_LORA:B_Q_LORA + B_KV_LORA]
    kr_a = z[:, B_Q_LORA + B_KV_LORA:B_Q_LORA + B_KV_LORA + LANE]
    kr_b = z[:, B_Q_LORA + B_KV_LORA + LANE:]
    qn = (_rms(zq) * qg_ref[...]).astype(BF16)
    kvn = (_rms(zkv) * kvg_ref[...]).astype(BF16)
    q1 = _dot(qn, wq_ref[...])
    q2 = _dot(qn, wqs_ref[...])
    kr = (kr_a * ck_ref[...] + kr_b * sk_ref[...]).astype(BF16)
    k = _dot(kvn, wk_ref[...]) + _dot(kr, place_ref[...])
    cq = cq_ref[...]
    sq = sq_ref[...]
    for hh in range(B_HEADS):
        sl = slice(hh * LANE, (hh + 1) * LANE)
        q_out[:, sl] = (q1[:, sl] * cq + q2[:, sl] * sq).astype(BF16)
    k_out[...] = k.astype(BF16)
    v_out[...] = _dot(kvn, wv_ref[...]).astype(BF16)


def _gelu_tanh(x):
    return 0.5 * x * (1.0 + jnp.tanh(math.sqrt(2.0 / math.pi) * (x + 0.044715 * (x * x * x))))


def _gmlp_kernel(x_ref, g1_ref, mod_ref, wa_ref, wg_ref, vg_ref, ws_ref, bsb_ref,
                 ya_out, gate_out):
    h = _normmod(x_ref[...], g1_ref[...], mod_ref[0], 0).astype(BF16)
    gate_out[...] = _sigmoid(_dot(h, wg_ref[...])).astype(BF16)
    zg = _gelu_tanh(_dot(h, wa_ref[...]))
    u = zg[:, :A_WIDTH]
    v = (_rms(zg[:, A_WIDTH:]) * vg_ref[...]).astype(BF16)
    for c in range(TM // A_CHUNK):
        rows = slice(c * A_CHUNK, (c + 1) * A_CHUNK)
        for g in range(A_GROUPS):
            cols = slice(g * LANE, (g + 1) * LANE)
            mixed = _dot(ws_ref[g], v[rows, cols]) + bsb_ref[:, cols]
            ya_out[rows, cols] = (u[rows, cols] * mixed).astype(BF16)


def _feat_kernel(x_ref, xp_ref, xn_ref, g1_ref, mod_ref, wc_ref, mu_ref,
                 w0_ref, w2_ref, a0_ref, a2_ref, kk_ref, ka_ref, rk_ref, g2_ref,
                 v_out, ab_out, bh_out, kh_out, rb_out, pe_out, bonus_out, g_out,
                 *, seq_starts, seq_ends):
    mod = mod_ref[0]
    g1 = g1_ref[...]
    h = _normmod(x_ref[...], g1, mod, 0).astype(BF16)
    hp = _normmod(xp_ref[...], g1, mod, 0).astype(BF16)
    hn = _normmod(xn_ref[...], g1, mod, 0).astype(BF16)
    tm = x_ref.shape[0]
    row = lax.broadcasted_iota(jnp.int32, (tm, 1), 0)
    first = pl.program_id(0) * tm
    has_prev = 1.0 - functools.reduce(jnp.maximum, [jnp.where(first == s, 1.0, 0.0) for s in seq_starts])
    has_next = 1.0 - functools.reduce(jnp.maximum,
                                      [jnp.where(first + tm - 1 == e, 1.0, 0.0) for e in seq_ends])

    def shifted_proj(cols):
        w = wc_ref[:, cols]
        z = _dot(h, w)
        z_up = jnp.where(row == 0, _dot(hp, w)[7:8] * has_prev, pltpu.roll(z, 1, 0))
        z_dn = jnp.where(row == tm - 1, _dot(hn, w)[0:1] * has_next, pltpu.roll(z, tm - 1, 0))
        return z + mu_ref[:, cols] * (0.5 * (z_up + z_dn) - z)

    r = shifted_proj(slice(0, HP))
    k = shifted_proj(slice(HP, 2 * HP))
    v = shifted_proj(slice(2 * HP, 3 * HP))
    zl = shifted_proj(slice(3 * HP, 3 * HP + 3 * LANE))
    zw = jnp.tanh(zl[:, :LANE]).astype(BF16)
    za = zl[:, LANE:2 * LANE].astype(BF16)
    zg = _sigmoid(zl[:, 2 * LANE:]).astype(BF16)

    kk = k * kk_ref[...]
    ka = ka_ref[...]
    rk = rk_ref[...]
    a = [_sigmoid(a0_ref[d:d + 1] + _dot(za, a2_ref[d])) for d in range(2)]
    kd = [k * (1.0 + (a[d] - 1.0) * ka) for d in range(2)]
    v_out[...] = v.astype(BF16)
    g_out[...] = _dot(zg, g2_ref[...])

    ti = lax.broadcasted_iota(jnp.int32, (CUM_ROWS, CUM_ROWS), 0)
    tj = lax.broadcasted_iota(jnp.int32, (CUM_ROWS, CUM_ROWS), 1)
    shift = SCAN_C.bit_length() - 1
    same = lax.shift_right_logical(ti, shift) == lax.shift_right_logical(tj, shift)
    p_in, p_prev, p_inv = [], [], []
    for d in range(2):
        lw = -DECAY_SCALE * _sigmoid(w0_ref[d:d + 1] + _dot(zw, w2_ref[d]))
        tri = jnp.where(same & ((tj <= ti) if d == 0 else (tj >= ti)), 1.0, 0.0).astype(BF16)
        hi = lw.astype(BF16)
        rest = lw - hi.astype(F32)
        mid = rest.astype(BF16)
        lo = (rest - mid.astype(F32)).astype(BF16)
        cs = jnp.concatenate(
            [_dot(tri, hi[rows]) + _dot(tri, mid[rows]) + _dot(tri, lo[rows])
             for rows in (slice(c0, c0 + CUM_ROWS) for c0 in range(0, tm, CUM_ROWS))], axis=0)
        p_in.append(jnp.exp(cs))
        p_prev.append(jnp.exp(cs - lw))
        p_inv.append(jnp.exp(-cs))
        kh_out[d] = (kd[d] * p_inv[d]).astype(BF16)
        rb_out[d] = (r * p_in[d]).astype(BF16)
        for cc in range(tm // SCAN_C):
            last = cc * SCAN_C + (SCAN_C - 1 if d == 0 else 0)
            pe_out[d, cc * 8:(cc + 1) * 8, :] = jnp.broadcast_to(p_in[d][last:last + 1], (8, HP))

    for hh in range(C_HEADS):
        sl = slice(hh * LANE, (hh + 1) * LANE)
        kh = kk[:, sl]
        kh = kh * lax.rsqrt(jnp.maximum(jnp.sum(kh * kh, axis=-1, keepdims=True), 1e-12))
        rh = r[:, sl] * rk[:, sl]
        bon = jnp.zeros((tm, 1), F32)
        for d in range(2):
            ab_out[d, :, sl] = (-kh * p_prev[d][:, sl]).astype(BF16)
            bh_out[d, :, sl] = (kh * a[d][:, sl] * p_inv[d][:, sl]).astype(BF16)
            bon = bon + jnp.sum(rh * kd[d][:, sl], axis=-1, keepdims=True)
        bonus_out[:, sl] = bon * v[:, sl]


def _mm(a, b, dims=((1,), (0,))):
    return lax.dot_general(a.astype(BF16), b.astype(BF16), (dims, ((), ())),
                           preferred_element_type=F32)


_NT = ((1,), (1,))
_TN = ((0,), (0,))


def _scan_kernel(vf_ref, vb_ref, abf_ref, abb_ref, bhf_ref, bhb_ref, khf_ref, khb_ref,
                 rbf_ref, rbb_ref, pef_ref, peb_ref, of_ref, ob_ref, s_ref):
    @pl.when(pl.program_id(1) == 0)
    def _():
        s_ref[...] = jnp.zeros_like(s_ref)

    c = SCAN_C
    row = lax.broadcasted_iota(jnp.int32, (c, c), 0)
    col = lax.broadcasted_iota(jnp.int32, (c, c), 1)
    refs = ((vf_ref, abf_ref, bhf_ref, khf_ref, rbf_ref, pef_ref, of_ref),
            (vb_ref, abb_ref, bhb_ref, khb_ref, rbb_ref, peb_ref, ob_ref))

    chains = [(d, hh) for d in range(2) for hh in range(C_HEADS)]

    def sl(hh):
        return slice(hh * LANE, (hh + 1) * LANE)

    def rows(key):
        sub, (d, _) = key
        k = sub if d == 0 else SCAN_STEP - 1 - sub
        return k, slice(k * c, (k + 1) * c)

    al = lambda key: refs[key[1][0]][1][0, rows(key)[1], sl(key[1][1])]
    bh = lambda key: refs[key[1][0]][2][0, rows(key)[1], sl(key[1][1])]
    kh = lambda key: refs[key[1][0]][3][0, rows(key)[1], sl(key[1][1])]
    rb = lambda key: refs[key[1][0]][4][0, rows(key)[1], sl(key[1][1])]
    vv = lambda key: refs[key[1][0]][0][rows(key)[1], sl(key[1][1])]
    pe = lambda key: refs[key[1][0]][5][0, 8 * rows(key)[0]:8 * rows(key)[0] + 1, sl(key[1][1])]
    strict = lambda key: (col < row) if key[1][0] == 0 else (col > row)
    incl = lambda key: (col <= row) if key[1][0] == 0 else (col >= row)

    keys = [(sub, ch) for sub in range(SCAN_STEP) for ch in chains]
    xb = {key: _mm(jnp.concatenate([al(key), rb(key)], axis=0), bh(key), _NT) for key in keys}
    xk = {key: _mm(jnp.concatenate([al(key), rb(key)], axis=0), kh(key), _NT) for key in keys}
    l_pows = {key: [jnp.where(strict(key), xb[key][:c], 0.0)] for key in keys}
    lakv = {key: _mm(jnp.where(strict(key), xk[key][:c], 0.0), vv(key)) for key in keys}
    for _ in range(5):
        for key in keys:
            l_pows[key].append(_mm(l_pows[key][-1], l_pows[key][-1]))

    for sub in range(SCAN_STEP):
        now = [(sub, ch) for ch in chains]
        u = {key: lakv[key] + _mm(al(key), s_ref[key[1]], _NT) for key in now}
        for it in range(6):
            u = {key: u[key] + _mm(l_pows[key][it], u[key]) for key in now}
        for key in now:
            o_ref = refs[key[1][0]][6]
            o_ref[rows(key)[1], sl(key[1][1])] = (
                _mm(rb(key), s_ref[key[1]], _NT)
                + _mm(jnp.where(incl(key), xb[key][c:], 0.0), u[key])
                + _mm(jnp.where(incl(key), xk[key][c:], 0.0), vv(key)))
        for key in now:
            upd = _mm(jnp.concatenate([u[key].astype(BF16), vv(key)], axis=0),
                      jnp.concatenate([bh(key), kh(key)], axis=0), _TN)
            s_ref[key[1]] = (s_ref[key[1]] + upd) * pe(key)


def _attn_ctx_kernel(q_ref, k_ref, v_ref, o_ref):
    s = _dot_nt(q_ref[...], k_ref[...])
    p = jnp.exp2(s - jnp.max(s, axis=-1, keepdims=True))
    l = jnp.sum(p, axis=-1, keepdims=True)
    o_ref[...] = (_dot(p.astype(BF16), v_ref[...]) / l).astype(BF16)


def _attn_kernel(q_ref, kc_ref, vtc_ref, kl_ref, vtl_ref, o_ref, *, k_chunk):
    q = q_ref[...]
    tq = q.shape[0]
    parts = [(kc_ref, vtc_ref, 0, kc_ref.shape[0])]
    parts += [(kl_ref, vtl_ref, c0, k_chunk) for c0 in range(0, kl_ref.shape[0], k_chunk)]
    scores = lambda part: _dot_nt(part[0][part[2]:part[2] + part[3], :], q)
    m = jnp.full((1, tq), -jnp.inf, F32)
    l = jnp.zeros((1, tq), F32)
    acc = jnp.zeros((o_ref.shape[0], tq), F32)
    s_next = scores(parts[0])
    for n, (_, vt_ref, c0, size) in enumerate(parts):
        s = s_next
        if n + 1 < len(parts):
            s_next = scores(parts[n + 1])
        m_new = jnp.maximum(m, jnp.max(s, axis=0, keepdims=True))
        p = jnp.exp2(s - m_new)
        corr = jnp.exp2(m - m_new)
        l = l * corr + jnp.sum(p, axis=0, keepdims=True)
        acc = acc * corr + _dot(vt_ref[:, c0:c0 + size], p.astype(BF16))
        m = m_new
    o_ref[...] = (acc / l).astype(BF16)


def _merge_kernel(of_ref, ob_ref, bonus_ref, g_ref, ya_ref, ybl_ref, ybc_ref, gate_ref, x_ref,
                  mod_ref, gng_ref, gnb_ref, wa_ref, wb_ref, wc_ref, wo_ref, n2_ref,
                  rw_ref, rb_ref, xn_out, h2_out, comb_out, *, n_lat_tiles):
    mod = mod_ref[0]
    yb = jnp.where(pl.program_id(0) < n_lat_tiles, ybl_ref[...], ybc_ref[...])
    o = of_ref[...] + ob_ref[...]
    lane = lax.broadcasted_iota(jnp.int32, (1, LANE), 1)
    real = lane < C_HEAD
    ycs = []
    for hh in range(C_HEADS):
        sl = slice(hh * LANE, (hh + 1) * LANE)
        oh = o[:, sl]
        mean = jnp.sum(oh, axis=-1, keepdims=True) * (1.0 / C_HEAD)
        dev = jnp.where(real, oh - mean, 0.0)
        var = jnp.sum(dev * dev, axis=-1, keepdims=True) * (1.0 / C_HEAD)
        y = dev * lax.rsqrt(var + C_GN_EPS) * gng_ref[:, sl] + gnb_ref[:, sl]
        ycs.append(((y + bonus_ref[:, sl]) * g_ref[:, sl]).astype(BF16))
    yc = jnp.concatenate(ycs, axis=1)
    gate = gate_ref[...].astype(F32)
    merged = (gate[:, :D_MODEL] * _dot(ya_ref[...], wa_ref[...])
              + gate[:, D_MODEL:2 * D_MODEL] * _dot(yb, wb_ref[...])
              + gate[:, 2 * D_MODEL:] * _dot(yc, wc_ref[...]))
    xn = x_ref[...] + mod[2:3] * _dot(merged.astype(BF16), wo_ref[...])
    xn_out[...] = xn
    h2 = _normmod(xn, n2_ref[...], mod, 3).astype(BF16)
    h2_out[...] = h2

    scores = 1.0 / (1.0 + jnp.exp(-_dot(h2, rw_ref[...])))
    neg = -jnp.inf
    sel = jnp.where(lane < N_EXPERTS, scores + rb_ref[...], neg)

    lane_f = lane.astype(F32)

    def top1(s):
        mx = jnp.max(s, axis=-1, keepdims=True)
        idx = jnp.min(jnp.where(s == mx, lane_f, float(LANE)), axis=-1, keepdims=True)
        return mx, idx.astype(jnp.int32)

    best = None
    for g in range(N_GROUPS):
        in_g = (lane >= g * EXPERTS_PER_GROUP) & (lane < (g + 1) * EXPERTS_PER_GROUP)
        s = jnp.where(in_g, sel, neg)
        m1, i1 = top1(s)
        m2, _ = top1(jnp.where(lane == i1, neg, s))
        score = m1 + m2
        if best is None:
            best, gidx = score, jnp.zeros_like(i1)
        else:
            better = score > best
            gidx = jnp.where(better, g, gidx)
            best = jnp.where(better, score, best)
    lo = gidx * EXPERTS_PER_GROUP
    s = jnp.where((lane >= lo) & (lane < lo + EXPERTS_PER_GROUP), sel, neg)
    _, i1 = top1(s)
    _, i2 = top1(jnp.where(lane == i1, neg, s))
    picked = jnp.where((lane == i1) | (lane == i2), scores, 0.0)
    comb_out[...] = picked / jnp.sum(picked, axis=-1, keepdims=True)


def _moe_kernel(cnt_ref, h2_ref, comb_ref, xn_ref, *rest, last):
    mod_refs = rest[:MOE_SUB]
    w1_ref, w3_ref, w2_ref, tri_ref, fg_ref, out_ref, rank_ref, rank_t_ref, comb_t_ref = rest[MOE_SUB:]
    i = pl.program_id(0)
    e = pl.program_id(1)
    tm = h2_ref.shape[0]
    lane = lax.broadcasted_iota(jnp.int32, (1, LANE), 1)

    @pl.when(e == 0)
    def _():
        comb = comb_ref[...]
        picked = jnp.where(comb > 0.0, 1.0, 0.0).astype(BF16)
        rank = _dot(tri_ref[...], picked)
        rank_ref[...] = rank
        rank_t_ref[...] = rank.T
        comb_t_ref[...] = comb.T
        out_ref[...] = jnp.zeros_like(out_ref)

    comb_e = jnp.sum(jnp.where(lane == e, comb_ref[...], 0.0), axis=-1, keepdims=True)
    rank_col = jnp.sum(jnp.where(lane == e, rank_ref[...], 0.0), axis=-1, keepdims=True)
    rank_row = rank_t_ref[pl.ds(e, 1), :]
    picked_row = comb_t_ref[pl.ds(e, 1), :] > 0.0
    slot_col = lax.broadcasted_iota(jnp.int32, (MOE_CAP, 1), 0)
    slot_row = lax.broadcasted_iota(jnp.int32, (1, MOE_CAP), 1)
    n_blocks = lax.shift_right_logical(cnt_ref[i, e] + (MOE_CAP - 1), MOE_CAP.bit_length() - 1)

    def block(blk, carry):
        base = blk * MOE_CAP
        gather = jnp.where(picked_row & (rank_row == (slot_col + base).astype(F32)), 1.0, 0.0).astype(BF16)
        xg = _dot(gather, h2_ref[...]).astype(BF16)
        a = _dot(xg, w1_ref[0])
        he = (a * _sigmoid(a) * _dot(xg, w3_ref[0])).astype(BF16)
        y = _dot(he, w2_ref[0]).astype(BF16)
        scatter = jnp.where((comb_e > 0.0) & (rank_col == (slot_row + base).astype(F32)), 1.0, 0.0).astype(BF16)
        out_ref[...] += comb_e * _dot(scatter, y)
        return carry

    lax.fori_loop(0, n_blocks, block, 0)

    @pl.when(e == N_EXPERTS - 1)
    def _():
        for s, m_ref in enumerate(mod_refs):
            rows = slice(s * SEG_ROWS, (s + 1) * SEG_ROWS)
            y = xn_ref[rows] + m_ref[0][5:6] * out_ref[rows]
            if last:
                y = _rms(y) * fg_ref[...]
            out_ref[rows] = y


def _head_pad_cols(w, width):
    lead = w.shape[:-1]
    w = w.reshape(lead + (C_HEADS, width))
    w = jnp.pad(w, [(0, 0)] * len(lead) + [(0, 0), (0, LANE - width)])
    return w.reshape(lead + (HP,))


def _head_pad_rows(w, width):
    n = w.shape[-1]
    w = w.reshape(C_HEADS, width, n)
    w = jnp.pad(w, ((0, 0), (0, LANE - width), (0, 0)))
    return w.reshape(HP, n)


def _split_rope(w):
    lead = w.shape[:-1]
    w = w.reshape(lead + (2, 2, ROPE_FREQS))
    x1 = w[..., 0, :].reshape(lead + (2 * ROPE_FREQS,))
    x2 = w[..., 1, :].reshape(lead + (2 * ROPE_FREQS,))
    return x1, x2


def _rope_tables(n_ctx, t):
    rows = t // GRID_W
    row = jnp.repeat(jnp.arange(rows, dtype=F32), GRID_W)
    col = jnp.tile(jnp.arange(GRID_W, dtype=F32), rows)
    inv = jnp.power(ROPE_BASE, -jnp.arange(ROPE_FREQS, dtype=F32) / ROPE_FREQS)
    ang = jnp.concatenate([row[:, None] * inv, col[:, None] * inv], axis=1)
    cos = jnp.concatenate([jnp.cos(ang), jnp.ones((n_ctx, 2 * ROPE_FREQS), F32)], axis=0)
    sin = jnp.concatenate([jnp.sin(ang), jnp.zeros((n_ctx, 2 * ROPE_FREQS), F32)], axis=0)
    n = n_ctx + t
    one = jnp.ones((n, B_NOPE), F32)
    zero = lambda w: jnp.zeros((n, w), F32)
    qs = MLA_SCALE * math.log2(math.e)
    cq = jnp.concatenate([one, cos, cos, zero(LANE - B_NOPE - B_ROPE)], axis=1) * qs
    sq = jnp.concatenate([zero(B_NOPE), -sin, sin, zero(LANE - B_NOPE - B_ROPE)], axis=1) * qs
    ck = jnp.concatenate([cos, cos, zero(LANE - B_ROPE)], axis=1)
    sk = jnp.concatenate([-sin, sin, zero(LANE - B_ROPE)], axis=1)
    return cq, sq, ck, sk


def _layer_weights(p, l):
    w = {}
    w_in = p['w_in'][l]
    d = w_in.shape[0]
    o = 0
    w_q = w_in[:, o:o + B_Q_LORA]; o += B_Q_LORA
    w_kv = w_in[:, o:o + B_KV_LORA]; o += B_KV_LORA
    w_kr = w_in[:, o:o + B_ROPE]; o += B_ROPE
    w_c = w_in[:, o:o + C_COLS]; o += C_COLS
    w_a = w_in[:, o:o + A_COLS]; o += A_COLS
    w_g = w_in[:, o:o + G_COLS]

    k1, k2 = _split_rope(w_kr)
    zpad = jnp.zeros((d, LANE - B_ROPE), F32)
    w['win_b'] = jnp.concatenate([w_q, w_kv, k1, k2, zpad, k2, k1, zpad], axis=1).astype(BF16)

    wuq = p['b_w_uq'][l].reshape(B_Q_LORA, B_HEADS, B_NOPE + B_ROPE)
    q1, q2 = _split_rope(wuq[..., B_NOPE:])
    qz = jnp.zeros((B_Q_LORA, B_HEADS, LANE - B_NOPE - B_ROPE), F32)
    w['wq'] = jnp.concatenate([wuq[..., :B_NOPE], q1, q2, qz], axis=-1).reshape(B_Q_LORA, HP).astype(BF16)
    w['wqs'] = jnp.concatenate([wuq[..., :B_NOPE], q2, q1, qz], axis=-1).reshape(B_Q_LORA, HP).astype(BF16)
    wukv = p['b_w_ukv'][l].reshape(B_KV_LORA, B_HEADS, B_NOPE + B_VDIM)
    w['wk'] = _head_pad_cols(wukv[..., :B_NOPE].reshape(B_KV_LORA, -1), B_NOPE).astype(BF16)
    w['wv'] = _head_pad_cols(wukv[..., B_NOPE:].reshape(B_KV_LORA, -1), B_VDIM).astype(BF16)
    place = jnp.zeros((LANE, B_HEADS, LANE), F32)
    idx = jnp.arange(B_ROPE)
    place = place.at[idx, :, B_NOPE + idx].set(1.0)
    w['place'] = place.reshape(LANE, HP).astype(BF16)
    w['qg'] = p['b_q_norm'][l][None]
    w['kvg'] = p['b_kv_norm'][l][None]

    w['wa'] = w_a.astype(BF16)
    w['wg'] = w_g.astype(BF16)
    w['vg'] = p['a_v_gain'][l][None]
    w['ws'] = p['a_ws'][l].astype(BF16)
    w['bsb'] = jnp.broadcast_to(p['a_bs'][l].T[:, :, None], (A_CHUNK, A_GROUPS, LANE)).reshape(A_CHUNK, A_WIDTH)

    cw = C_WIDTH
    pieces = [_head_pad_cols(w_c[:, i * cw:(i + 1) * cw], C_HEAD) for i in range(3)]
    w['wc'] = jnp.concatenate(pieces + [w_c[:, 3 * cw:]], axis=1).astype(BF16)
    mu = p['c_mu'][l]
    w['mu'] = jnp.concatenate([_head_pad_cols(mu[i * cw:(i + 1) * cw], C_HEAD) for i in range(3)]
                              + [mu[3 * cw:]])[None]

    def lora_block(m, rank):
        m = _head_pad_cols(m, C_HEAD)
        z = jnp.zeros_like(m[0])
        return jnp.stack([jnp.concatenate([m[0], z], axis=0), jnp.concatenate([z, m[1]], axis=0)]).astype(BF16)

    w['w2'] = lora_block(p['c_w2'][l], C_DECAY_LORA)
    w['a2'] = lora_block(p['c_a2'][l], C_AAA_LORA)
    w['w0'] = _head_pad_cols(p['c_w0'][l], C_HEAD)
    w['a0'] = _head_pad_cols(p['c_a0'][l], C_HEAD)
    w['kk'] = _head_pad_cols(p['c_k_k'][l], C_HEAD)[None]
    w['ka'] = _head_pad_cols(p['c_k_a'][l], C_HEAD)[None]
    w['rk'] = _head_pad_cols(p['c_r_k'][l].reshape(-1), C_HEAD)[None]
    w['g2'] = _head_pad_cols(p['c_g2'][l], C_HEAD).astype(BF16)
    w['gng'] = _head_pad_cols(p['c_gn_g'][l], C_HEAD)[None]
    w['gnb'] = _head_pad_cols(p['c_gn_b'][l], C_HEAD)[None]

    w['wup_a'] = p['w_up_a'][l].astype(BF16)
    w['wup_b'] = p['w_up_b'][l].astype(BF16)
    w['wup_c'] = _head_pad_rows(p['w_up_c'][l], C_HEAD).astype(BF16)
    w['wo'] = p['w_o'][l].astype(BF16)
    w['g1'] = p['norm1_g'][l][None]
    w['n2'] = p['norm2_g'][l][None]
    w['w1'] = p['moe_w1'][l].astype(BF16)
    w['w3'] = p['moe_w3'][l].astype(BF16)
    w['w2e'] = p['moe_w2'][l].astype(BF16)
    return w


def _layer(xs, mods, w, tabs, rw, rb, fg, *, bn, t, n_ctx, last):
    m, d = xs.shape
    n_tiles = m // TM
    n_lat = bn * t
    n_lat_tiles = n_lat // TM
    tpl = t // TM
    row = lambda i: (i, 0)
    seg_of = lambda blk: jnp.where(blk < n_lat // SEG_ROWS, 1 + blk // (t // SEG_ROWS), 0)
    seg = lambda i: (seg_of(i * (TM // SEG_ROWS)), 0, 0)
    x_spec = pl.BlockSpec((TM, d), row)
    mod_spec = pl.BlockSpec((1, 6, d), seg)
    hp_spec = pl.BlockSpec((TM, HP), row)
    hp2_spec = pl.BlockSpec((2, TM, HP), lambda i: (0, i, 0))
    hp_f32 = jax.ShapeDtypeStruct((m, HP), F32)
    hp_bf16 = jax.ShapeDtypeStruct((m, HP), BF16)
    hp2_bf16 = jax.ShapeDtypeStruct((2, m, HP), BF16)
    cq, sq, ck, sk = tabs
    tab_spec = pl.BlockSpec(
        (TM, LANE), lambda i: (jnp.where(i < n_lat_tiles, i % tpl, tpl + i - n_lat_tiles), 0))

    q, k, v = pl.pallas_call(
        _mla_kernel,
        grid=(n_tiles,),
        in_specs=[x_spec, _full((1, d)), mod_spec, _full(w['win_b'].shape), _full((1, B_Q_LORA)),
                  _full((1, B_KV_LORA)), _full(w['wq'].shape), _full(w['wqs'].shape),
                  _full(w['wk'].shape), _full(w['wv'].shape), _full(w['place'].shape),
                  tab_spec, tab_spec, tab_spec, tab_spec],
        out_specs=[hp_spec, hp_spec, hp_spec],
        out_shape=[hp_bf16, hp_bf16, hp_bf16],
        compiler_params=_params(1),
        name="mla_proj",
    )(xs, w['g1'], mods, w['win_b'], w['qg'], w['kvg'], w['wq'], w['wqs'], w['wk'], w['wv'],
      w['place'], cq, sq, ck, sk)

    ya, gate = pl.pallas_call(
        _gmlp_kernel,
        grid=(n_tiles,),
        in_specs=[x_spec, _full((1, d)), mod_spec, _full(w['wa'].shape), _full(w['wg'].shape),
                  _full((1, A_WIDTH)), _full(w['ws'].shape), _full(w['bsb'].shape)],
        out_specs=[pl.BlockSpec((TM, A_WIDTH), row), pl.BlockSpec((TM, G_COLS), row)],
        out_shape=[jax.ShapeDtypeStruct((m, A_WIDTH), BF16), jax.ShapeDtypeStruct((m, G_COLS), BF16)],
        compiler_params=_params(1),
        name="gmlp_gates",
    )(xs, w['g1'], mods, w['wa'], w['wg'], w['vg'], w['ws'], w['bsb'])

    tf = TM_FEAT
    sub = tf // 8
    seq_starts = tuple(b * t for b in range(bn)) + tuple(n_lat + b * n_ctx for b in range(bn))
    seq_ends = tuple(s - 1 for s in seq_starts[1:]) + (m - 1,)
    assert all(s % tf == 0 for s in seq_starts)
    f_hp = pl.BlockSpec((tf, HP), row)
    f_hp2 = pl.BlockSpec((2, tf, HP), lambda i: (0, i, 0))
    pe_spec = pl.BlockSpec((2, 8 * (tf // SCAN_C), HP), lambda i: (0, i, 0))
    pe_shape = jax.ShapeDtypeStruct((2, 8 * (m // SCAN_C), HP), F32)
    vv, ab, bh, kh, rbar, pe, bonus, gg = pl.pallas_call(
        functools.partial(_feat_kernel, seq_starts=seq_starts, seq_ends=seq_ends),
        grid=(m // tf,),
        in_specs=[pl.BlockSpec((tf, d), row),
                  pl.BlockSpec((8, d), lambda i: (jnp.maximum(i * sub - 1, 0), 0)),
                  pl.BlockSpec((8, d), lambda i: (jnp.minimum((i + 1) * sub, m // 8 - 1), 0)),
                  _full((1, d)), pl.BlockSpec((1, 6, d), lambda i: (seg_of(i * (tf // SEG_ROWS)), 0, 0)),
                  _full(w['wc'].shape), _full(w['mu'].shape),
                  _full((2, HP)), _full(w['w2'].shape), _full((2, HP)), _full(w['a2'].shape),
                  _full((1, HP)), _full((1, HP)), _full((1, HP)), _full(w['g2'].shape)],
        out_specs=[f_hp, f_hp2, f_hp2, f_hp2, f_hp2, pe_spec, f_hp, f_hp],
        out_shape=[hp_bf16, hp2_bf16, hp2_bf16, hp2_bf16, hp2_bf16, pe_shape, hp_f32, hp_f32],
        compiler_params=_params(1),
        name="rwkv_features",
    )(xs, xs, xs, w['g1'], mods, w['wc'], w['mu'], w['w0'], w['w2'], w['a0'], w['a2'],
      w['kk'], w['ka'], w['rk'], w['g2'])

    blk_rows = SCAN_C * SCAN_STEP
    nctx_c = n_ctx // blk_rows
    nlat_c = t // blk_rows
    nc = nctx_c + nlat_c

    def chunk_of(dd, b, j):
        ctx_base = n_lat // blk_rows + b * nctx_c
        lat_base = b * nlat_c
        if dd == 0:
            return jnp.where(j < nctx_c, ctx_base + j, lat_base + j - nctx_c)
        return jnp.where(j < nctx_c, ctx_base + nctx_c - 1 - j, lat_base + nc - 1 - j)

    def tok_spec(dd):
        return pl.BlockSpec((blk_rows, HP), lambda b, j: (chunk_of(dd, b, j), 0))

    def dir_spec(dd, rows=blk_rows):
        return pl.BlockSpec((1, rows, HP), lambda b, j: (dd, chunk_of(dd, b, j), 0))

    o_f, o_b = pl.pallas_call(
        _scan_kernel,
        grid=(bn, nc),
        in_specs=[tok_spec(0), tok_spec(1)] + [dir_spec(dd) for _ in range(4) for dd in range(2)]
                 + [dir_spec(0, 8 * SCAN_STEP), dir_spec(1, 8 * SCAN_STEP)],
        out_specs=[tok_spec(0), tok_spec(1)],
        out_shape=[hp_f32, hp_f32],
        scratch_shapes=[pltpu.VMEM((2, C_HEADS, LANE, LANE), F32)],
        compiler_params=_params(2),
        name="rwkv_scan",
    )(vv, vv, ab, ab, bh, bh, kh, kh, rbar, rbar, pe, pe)

    ctx_blk = n_lat // n_ctx
    tq = ATTN_TQ
    vt = v.reshape(m, B_HEADS, LANE)[:, :, :B_VDIM].transpose(1, 2, 0).reshape(B_HEADS * B_VDIM, m)
    ctx_k = pl.BlockSpec((n_ctx, LANE), lambda b, hh, qi: (ctx_blk + b, hh))
    ctx_vt = pl.BlockSpec((B_VDIM, n_ctx), lambda b, hh, qi: (hh, ctx_blk + b))
    lat_k = pl.BlockSpec((t, LANE), lambda b, hh, qi: (b, hh))
    lat_vt = pl.BlockSpec((B_VDIM, t), lambda b, hh, qi: (hh, b))
    q_lat = pl.BlockSpec((tq, LANE), lambda b, hh, qi: (b * (t // tq) + qi, hh))
    yb_lat_t = pl.pallas_call(
        functools.partial(_attn_kernel, k_chunk=ATTN_KC),
        grid=(bn, B_HEADS, t // tq),
        in_specs=[q_lat, ctx_k, ctx_vt, lat_k, lat_vt],
        out_specs=pl.BlockSpec((B_VDIM, tq), lambda b, hh, qi: (hh, b * (t // tq) + qi)),
        out_shape=jax.ShapeDtypeStruct((B_HEADS * B_VDIM, n_lat), BF16),
        compiler_params=_params(3),
        name="mla_attention",
    )(q, k, vt, k, vt)
    yb_lat = yb_lat_t.T
    ctx_q = pl.BlockSpec((n_ctx, LANE), lambda b, hh: (ctx_blk + b, hh))
    yb_ctx = pl.pallas_call(
        _attn_ctx_kernel,
        grid=(bn, B_HEADS),
        in_specs=[ctx_q, ctx_q, ctx_q],
        out_specs=pl.BlockSpec((n_ctx, LANE), lambda b, hh: (b, hh)),
        out_shape=jax.ShapeDtypeStruct((bn * n_ctx, HP), BF16),
        compiler_params=_params(2),
        name="mla_attention_ctx",
    )(q, k, v)
    yb_ctx = yb_ctx.reshape(bn * n_ctx, B_HEADS, LANE)[:, :, :B_VDIM].reshape(bn * n_ctx, B_HEADS * B_VDIM)

    n_ctx_tiles = n_tiles - n_lat_tiles
    yb_w = B_HEADS * B_VDIM
    xn, h2, comb = pl.pallas_call(
        functools.partial(_merge_kernel, n_lat_tiles=n_lat_tiles),
        grid=(n_tiles,),
        in_specs=[hp_spec, hp_spec, hp_spec, hp_spec, pl.BlockSpec((TM, A_WIDTH), row),
                  pl.BlockSpec((TM, yb_w), lambda i: (jnp.minimum(i, n_lat_tiles - 1), 0)),
                  pl.BlockSpec((TM, yb_w), lambda i: (jnp.clip(i - n_lat_tiles, 0, n_ctx_tiles - 1), 0)),
                  pl.BlockSpec((TM, G_COLS), row), x_spec, mod_spec, _full((1, HP)), _full((1, HP)),
                  _full(w['wup_a'].shape), _full(w['wup_b'].shape), _full(w['wup_c'].shape),
                  _full(w['wo'].shape), _full((1, d)), _full(rw.shape), _full(rb.shape)],
        out_specs=[x_spec, x_spec, pl.BlockSpec((TM, LANE), row)],
        out_shape=[jax.ShapeDtypeStruct((m, d), F32), jax.ShapeDtypeStruct((m, d), BF16),
                   jax.ShapeDtypeStruct((m, LANE), F32)],
        compiler_params=_params(1),
        name="merge_router",
    )(o_f, o_b, bonus, gg, ya, yb_lat, yb_ctx, gate, xs, mods, w['gng'], w['gnb'], w['wup_a'], w['wup_b'],
      w['wup_c'], w['wo'], w['n2'], rw, rb)

    tm_moe = MOE_SUB * SEG_ROWS
    counts = jnp.sum((comb > 0.0).reshape(m // tm_moe, tm_moe, LANE), axis=1, dtype=jnp.int32)
    tri = jnp.tril(jnp.ones((tm_moe, tm_moe), BF16), -1)
    mrow = lambda i, e, cnt: (i, 0)
    once = pl.Buffered(1)
    mseg = lambda s: pl.BlockSpec((1, 6, d), lambda i, e, cnt: (seg_of(i * MOE_SUB + s), 0, 0))
    wexp = lambda shape: pl.BlockSpec((1,) + shape, lambda i, e, cnt: (e, 0, 0))
    grid_spec = pltpu.PrefetchScalarGridSpec(
        num_scalar_prefetch=1,
        grid=(m // tm_moe, N_EXPERTS),
        in_specs=[pl.BlockSpec((tm_moe, d), mrow), pl.BlockSpec((tm_moe, LANE), mrow),
                  pl.BlockSpec((tm_moe, d), mrow, pipeline_mode=once)]
                 + [mseg(s) for s in range(MOE_SUB)]
                 + [wexp((d, D_EXPERT)), wexp((d, D_EXPERT)), wexp((D_EXPERT, d)),
                    pl.BlockSpec((tm_moe, tm_moe), lambda i, e, cnt: (0, 0), pipeline_mode=once),
                    pl.BlockSpec((1, d), lambda i, e, cnt: (0, 0))],
        out_specs=pl.BlockSpec((tm_moe, d), mrow),
        scratch_shapes=[pltpu.VMEM((tm_moe, LANE), F32), pltpu.VMEM((LANE, tm_moe), F32),
                        pltpu.VMEM((LANE, tm_moe), F32)],
    )
    return pl.pallas_call(
        functools.partial(_moe_kernel, last=last),
        grid_spec=grid_spec,
        out_shape=jax.ShapeDtypeStruct((m, d), F32),
        compiler_params=_params(2),
        name="moe",
    )(counts, h2, comb, xn, *([mods] * MOE_SUB), w['w1'], w['w3'], w['w2e'], tri, fg)


def kernel(x, c, ctx, c_ctx, mod_w, mod_b, norm1_g, norm2_g, w_in, a_v_gain, a_ws, a_bs, b_q_norm, b_w_uq, b_kv_norm, b_w_ukv, c_mu, c_w0, c_w2, c_a0, c_a2, c_g2, c_k_k, c_k_a, c_r_k, c_gn_g, c_gn_b, w_up_a, w_up_b, w_up_c, w_o, router_w, router_b, moe_w1, moe_w3, moe_w2, final_g):
    p = dict(w_in=w_in, a_v_gain=a_v_gain, a_ws=a_ws, a_bs=a_bs, b_q_norm=b_q_norm, b_w_uq=b_w_uq,
             b_kv_norm=b_kv_norm, b_w_ukv=b_w_ukv, c_mu=c_mu, c_w0=c_w0, c_w2=c_w2, c_a0=c_a0,
             c_a2=c_a2, c_g2=c_g2, c_k_k=c_k_k, c_k_a=c_k_a, c_r_k=c_r_k, c_gn_g=c_gn_g,
             c_gn_b=c_gn_b, w_up_a=w_up_a, w_up_b=w_up_b, w_up_c=w_up_c, w_o=w_o,
             norm1_g=norm1_g, norm2_g=norm2_g, moe_w1=moe_w1, moe_w3=moe_w3, moe_w2=moe_w2)
    bn, t, d = x.shape
    n_ctx = ctx.shape[1]
    depth = mod_w.shape[0]
    m = bn * (t + n_ctx)
    assert n_ctx == SEG_ROWS and (bn * n_ctx) % TM == 0 and t % TM == 0 and t % ATTN_TQ == 0
    assert t % ATTN_KC == 0 and m % (MOE_SUB * SEG_ROWS) == 0 and 1 + bn <= 8

    cvec = jnp.concatenate([c_ctx[None], c, jnp.zeros((8 - 1 - bn, d), F32)], axis=0)
    mods = _modulation(cvec, mod_w, mod_b).reshape(depth, 8, 6, d)

    tabs = _rope_tables(bn * n_ctx, t)
    rw = jnp.pad(router_w, ((0, 0), (0, LANE - N_EXPERTS))).astype(BF16)
    rb = jnp.pad(router_b, (0, LANE - N_EXPERTS))[None]
    fg = final_g[None]

    xs = jnp.concatenate([x.reshape(bn * t, d), ctx.reshape(bn * n_ctx, d)], axis=0)
    for l in range(depth):
        xs = _layer(xs, mods[l], _layer_weights(p, l), tabs, rw, rb, fg,
                    bn=bn, t=t, n_ctx=n_ctx, last=(l == depth - 1))
    return xs[:bn * t].reshape(bn, t, d)
```

```python
import functools
import math

import jax
import jax.numpy as jnp
from jax import lax
from jax.experimental import pallas as pl
from jax.experimental.pallas import tpu as pltpu

F32 = jnp.float32
BF16 = jnp.bfloat16
HIGHEST = lax.Precision.HIGHEST

D_MODEL = 1024
GRID_W = 64
EPS = 1e-6

A_WIDTH = 512
A_GROUPS = 4
A_CHUNK = 128

B_HEADS = 8
B_Q_LORA = 384
B_KV_LORA = 256
B_NOPE = 64
B_ROPE = 32
B_VDIM = 64
ROPE_BASE = 10000.0
ROPE_FREQS = B_ROPE // 4
MLA_SCALE = (B_NOPE + B_ROPE) ** -0.5

C_HEADS = 8
C_HEAD = 64
C_WIDTH = C_HEADS * C_HEAD
C_DECAY_LORA = 64
C_AAA_LORA = 64
C_GATE_LORA = 128
C_GN_EPS = 64e-5
DECAY_SCALE = 0.6065306597126334

B_COLS = B_Q_LORA + B_KV_LORA + B_ROPE
C_COLS = 3 * C_WIDTH + 2 * C_DECAY_LORA + 2 * C_AAA_LORA + C_GATE_LORA
A_COLS = 2 * A_WIDTH
G_COLS = 3 * D_MODEL

N_EXPERTS = 16
N_GROUPS = 4
EXPERTS_PER_GROUP = 4
D_EXPERT = 512

LANE = 128
HP = C_HEADS * LANE
TM = 512
TM_FEAT = 256
SEG_ROWS = 256
SCAN_C = 64
SCAN_STEP = 2
CUM_ROWS = 256
MOE_SUB = 6
MOE_CAP = 256
ATTN_TQ = 512
ATTN_KC = 1024
VMEM_LIMIT = 52 * 1024 * 1024


def _dot(a, b, precision=None):
    return jnp.dot(a, b, preferred_element_type=F32, precision=precision)


def _dot_nt(a, b, precision=None):
    return lax.dot_general(a, b, (((1,), (1,)), ((), ())),
                           preferred_element_type=F32, precision=precision)


def _dot_tn(a, b, precision=None):
    return lax.dot_general(a, b, (((0,), (0,)), ((), ())),
                           preferred_element_type=F32, precision=precision)


def _rms(x, eps=EPS):
    return x * lax.rsqrt(jnp.mean(x * x, axis=-1, keepdims=True) + eps)


def _normmod(x, g, mod, first):
    return _rms(x) * g * (1.0 + mod[first + 1:first + 2]) + mod[first:first + 1]


def _sigmoid(x):
    return 0.5 * jnp.tanh(0.5 * x) + 0.5


def _params(n_grid):
    return pltpu.CompilerParams(
        dimension_semantics=("arbitrary",) * n_grid, vmem_limit_bytes=VMEM_LIMIT)


def _full(shape):
    n = len(shape)
    return pl.BlockSpec(shape, lambda *_: (0,) * n)


def _mod_kernel(c_ref, w_ref, b_ref, o_ref):
    c = c_ref[...]
    act = (c * _sigmoid(c)).astype(BF16)
    o_ref[0] = _dot(act, w_ref[0].astype(BF16)) + b_ref[0]


def _modulation(cvec, mod_w, mod_b):
    depth, d, n = mod_w.shape
    tn = 1536
    return pl.pallas_call(
        _mod_kernel,
        grid=(depth, n // tn),
        in_specs=[
            pl.BlockSpec((8, d), lambda l, j: (0, 0)),
            pl.BlockSpec((1, d, tn), lambda l, j: (l, 0, j)),
            pl.BlockSpec((1, 1, tn), lambda l, j: (l, 0, j)),
        ],
        out_specs=pl.BlockSpec((1, 8, tn), lambda l, j: (l, 0, j)),
        out_shape=jax.ShapeDtypeStruct((depth, 8, n), F32),
        compiler_params=_params(2),
        name="modulation",
    )(cvec, mod_w, mod_b.reshape(depth, 1, n))


def _mla_kernel(x_ref, g1_ref, mod_ref, win_ref, qg_ref, kvg_ref, wq_ref, wqs_ref,
                wk_ref, wv_ref, place_ref, cq_ref, sq_ref, ck_ref, sk_ref,
                q_out, k_out, v_out):
    h = _normmod(x_ref[...], g1_ref[...], mod_ref[0], 0).astype(BF16)
    z = _dot(h, win_ref[...])
    zq = z[:, :B_Q_LORA]
    zkv = z[:, B_Q_LORA:B_Q_LORA + B_KV_LORA]
    kr_a = z[:, B_Q_LORA + B_KV_LORA:B_Q_LORA + B_KV_LORA + LANE]
    kr_b = z[:, B_Q_LORA + B_KV_LORA + LANE:]
    qn = (_rms(zq) * qg_ref[...]).astype(BF16)
    kvn = (_rms(zkv) * kvg_ref[...]).astype(BF16)
    q1 = _dot(qn, wq_ref[...])
    q2 = _dot(qn, wqs_ref[...])
    kr = (kr_a * ck_ref[...] + kr_b * sk_ref[...]).astype(BF16)
    k = _dot(kvn, wk_ref[...]) + _dot(kr, place_ref[...])
    cq = cq_ref[...]
    sq = sq_ref[...]
    for hh in range(B_HEADS):
        sl = slice(hh * LANE, (hh + 1) * LANE)
        q_out[:, sl] = (q1[:, sl] * cq + q2[:, sl] * sq).astype(BF16)
    k_out[...] = k.astype(BF16)
    v_out[...] = _dot(kvn, wv_ref[...]).astype(BF16)


def _gelu_tanh(x):
    return 0.5 * x * (1.0 + jnp.tanh(math.sqrt(2.0 / math.pi) * (x + 0.044715 * (x * x * x))))


def _gmlp_kernel(x_ref, g1_ref, mod_ref, wa_ref, wg_ref, vg_ref, ws_ref, bsb_ref,
                 ya_out, gate_out):
    h = _normmod(x_ref[...], g1_ref[...], mod_ref[0], 0).astype(BF16)
    gate_out[...] = _sigmoid(_dot(h, wg_ref[...])).astype(BF16)
    zg = _gelu_tanh(_dot(h, wa_ref[...]))
    u = zg[:, :A_WIDTH]
    v = (_rms(zg[:, A_WIDTH:]) * vg_ref[...]).astype(BF16)
    for c in range(TM // A_CHUNK):
        rows = slice(c * A_CHUNK, (c + 1) * A_CHUNK)
        for g in range(A_GROUPS):
            cols = slice(g * LANE, (g + 1) * LANE)
            mixed = _dot(ws_ref[g], v[rows, cols]) + bsb_ref[:, cols]
            ya_out[rows, cols] = (u[rows, cols] * mixed).astype(BF16)


def _feat_kernel(x_ref, xp_ref, xn_ref, g1_ref, mod_ref, wc_ref, mu_ref,
                 w0_ref, w2_ref, a0_ref, a2_ref, kk_ref, ka_ref, rk_ref, g2_ref,
                 v_out, ab_out, bh_out, kh_out, rb_out, pe_out, bonus_out, g_out,
                 *, seq_starts, seq_ends):
    mod = mod_ref[0]
    g1 = g1_ref[...]
    h = _normmod(x_ref[...], g1, mod, 0).astype(BF16)
    hp = _normmod(xp_ref[...], g1, mod, 0).astype(BF16)
    hn = _normmod(xn_ref[...], g1, mod, 0).astype(BF16)
    tm = x_ref.shape[0]
    row = lax.broadcasted_iota(jnp.int32, (tm, 1), 0)
    first = pl.program_id(0) * tm
    has_prev = 1.0 - functools.reduce(jnp.maximum, [jnp.where(first == s, 1.0, 0.0) for s in seq_starts])
    has_next = 1.0 - functools.reduce(jnp.maximum,
                                      [jnp.where(first + tm - 1 == e, 1.0, 0.0) for e in seq_ends])

    def shifted_proj(cols):
        w = wc_ref[:, cols]
        z = _dot(h, w)
        z_up = pltpu.roll(z, 1, 0)
        z_dn = pltpu.roll(z, tm - 1, 0)
        z_up = jnp.concatenate(
            [jnp.where(row[:8] == 0, _dot(hp, w)[7:8] * has_prev, z_up[:8]), z_up[8:]], axis=0)
        z_dn = jnp.concatenate(
            [z_dn[:tm - 8], jnp.where(row[tm - 8:] == tm - 1, _dot(hn, w)[0:1] * has_next, z_dn[tm - 8:])],
            axis=0)
        return z + mu_ref[:, cols] * (0.5 * (z_up + z_dn) - z)

    zl = shifted_proj(slice(3 * HP, 3 * HP + 3 * LANE))
    zw = jnp.tanh(zl[:, :LANE]).astype(BF16)
    za = zl[:, LANE:2 * LANE].astype(BF16)
    zg = _sigmoid(zl[:, 2 * LANE:]).astype(BF16)
    w_logit = [w0_ref[d:d + 1] + _dot(zw, w2_ref[d]) for d in range(2)]
    a_logit = [a0_ref[d:d + 1] + _dot(za, a2_ref[d]) for d in range(2)]
    g_out[...] = _dot(zg, g2_ref[...])
    r = shifted_proj(slice(0, HP))

    ti = lax.broadcasted_iota(jnp.int32, (CUM_ROWS, CUM_ROWS), 0)
    tj = lax.broadcasted_iota(jnp.int32, (CUM_ROWS, CUM_ROWS), 1)
    shift = SCAN_C.bit_length() - 1
    same = lax.shift_right_logical(ti, shift) == lax.shift_right_logical(tj, shift)
    lws, css = [], []
    for d in range(2):
        lw = -DECAY_SCALE * _sigmoid(w_logit[d])
        tri = jnp.where(same & ((tj <= ti) if d == 0 else (tj >= ti)), 1.0, 0.0).astype(BF16)
        hi = lw.astype(BF16)
        lo = (lw - hi.astype(F32)).astype(BF16)
        css.append(jnp.concatenate(
            [_dot(tri, hi[rows]) + _dot(tri, lo[rows])
             for rows in (slice(c0, c0 + CUM_ROWS) for c0 in range(0, tm, CUM_ROWS))], axis=0))
        lws.append(lw)
    k = shifted_proj(slice(HP, 2 * HP))
    v = shifted_proj(slice(2 * HP, 3 * HP))
    v_out[...] = v.astype(BF16)

    kk = k * kk_ref[...]
    ka = ka_ref[...]
    rk = rk_ref[...]
    a = [_sigmoid(a_logit[d]) for d in range(2)]
    kd = [k * (1.0 + (a[d] - 1.0) * ka) for d in range(2)]
    p_in, p_prev, p_inv = [], [], []
    for d in range(2):
        p_in.append(jnp.exp(css[d]))
        p_prev.append(jnp.exp(css[d] - lws[d]))
        p_inv.append(jnp.exp(-css[d]))
        kh_out[d] = (kd[d] * p_inv[d]).astype(BF16)
        rb_out[d] = (r * p_in[d]).astype(BF16)
        for cc in range(tm // SCAN_C):
            last = cc * SCAN_C + (SCAN_C - 1 if d == 0 else 0)
            pe_out[d, cc * 8:(cc + 1) * 8, :] = jnp.broadcast_to(p_in[d][last:last + 1], (8, HP))

    for hh in range(C_HEADS):
        sl = slice(hh * LANE, (hh + 1) * LANE)
        kh = kk[:, sl]
        kh = kh * lax.rsqrt(jnp.maximum(jnp.sum(kh * kh, axis=-1, keepdims=True), 1e-12))
        rh = r[:, sl] * rk[:, sl]
        bon = jnp.zeros((tm, 1), F32)
        for d in range(2):
            ab_out[d, :, sl] = (-kh * p_prev[d][:, sl]).astype(BF16)
            bh_out[d, :, sl] = (kh * a[d][:, sl] * p_inv[d][:, sl]).astype(BF16)
            bon = bon + jnp.sum(rh * kd[d][:, sl], axis=-1, keepdims=True)
        bonus_out[:, sl] = bon * v[:, sl]


def _mm(a, b, dims=((1,), (0,))):
    return lax.dot_general(a.astype(BF16), b.astype(BF16), (dims, ((), ())),
                           preferred_element_type=F32)


_NT = ((1,), (1,))
_TN = ((0,), (0,))


def _scan_kernel(vf_ref, vb_ref, abf_ref, abb_ref, bhf_ref, bhb_ref, khf_ref, khb_ref,
                 rbf_ref, rbb_ref, pef_ref, peb_ref, of_ref, ob_ref, s_ref):
    @pl.when(pl.program_id(1) == 0)
    def _():
        s_ref[...] = jnp.zeros_like(s_ref)

    c = SCAN_C
    row = lax.broadcasted_iota(jnp.int32, (c, c), 0)
    col = lax.broadcasted_iota(jnp.int32, (c, c), 1)
    refs = ((vf_ref, abf_ref, bhf_ref, khf_ref, rbf_ref, pef_ref, of_ref),
            (vb_ref, abb_ref, bhb_ref, khb_ref, rbb_ref, peb_ref, ob_ref))

    chains = [(d, hh) for d in range(2) for hh in range(C_HEADS)]

    def sl(hh):
        return slice(hh * LANE, (hh + 1) * LANE)

    def rows(key):
        sub, (d, _) = key
        k = sub if d == 0 else SCAN_STEP - 1 - sub
        return k, slice(k * c, (k + 1) * c)

    al = lambda key: refs[key[1][0]][1][0, rows(key)[1], sl(key[1][1])]
    bh = lambda key: refs[key[1][0]][2][0, rows(key)[1], sl(key[1][1])]
    kh = lambda key: refs[key[1][0]][3][0, rows(key)[1], sl(key[1][1])]
    rb = lambda key: refs[key[1][0]][4][0, rows(key)[1], sl(key[1][1])]
    vv = lambda key: refs[key[1][0]][0][rows(key)[1], sl(key[1][1])]
    pe = lambda key: refs[key[1][0]][5][0, 8 * rows(key)[0]:8 * rows(key)[0] + 1, sl(key[1][1])]
    strict = lambda key: (col < row) if key[1][0] == 0 else (col > row)
    incl = lambda key: (col <= row) if key[1][0] == 0 else (col >= row)

    keys = [(sub, ch) for sub in range(SCAN_STEP) for ch in chains]
    xb = {key: _mm(jnp.concatenate([al(key), rb(key)], axis=0), bh(key), _NT) for key in keys}
    xk = {key: _mm(jnp.concatenate([al(key), rb(key)], axis=0), kh(key), _NT) for key in keys}
    l_pows = {key: [jnp.where(strict(key), xb[key][:c], 0.0)] for key in keys}
    lakv = {key: _mm(jnp.where(strict(key), xk[key][:c], 0.0), vv(key)) for key in keys}
    for _ in range(5):
        for key in keys:
            l_pows[key].append(_mm(l_pows[key][-1], l_pows[key][-1]))

    for sub in range(SCAN_STEP):
        now = [(sub, ch) for ch in chains]
        u = {key: lakv[key] + _mm(al(key), s_ref[key[1]], _NT) for key in now}
        for it in range(6):
            u = {key: u[key] + _mm(l_pows[key][it], u[key]) for key in now}
        for key in now:
            o_ref = refs[key[1][0]][6]
            o_ref[rows(key)[1], sl(key[1][1])] = (
                _mm(rb(key), s_ref[key[1]], _NT)
                + _mm(jnp.where(incl(key), xb[key][c:], 0.0), u[key])
                + _mm(jnp.where(incl(key), xk[key][c:], 0.0), vv(key)))
        for key in now:
            upd = _mm(jnp.concatenate([u[key].astype(BF16), vv(key)], axis=0),
                      jnp.concatenate([bh(key), kh(key)], axis=0), _TN)
            s_ref[key[1]] = (s_ref[key[1]] + upd) * pe(key)


def _attn_ctx_kernel(q_ref, k_ref, v_ref, o_ref):
    s = _dot_nt(q_ref[...], k_ref[...])
    p = jnp.exp2(s - jnp.max(s, axis=-1, keepdims=True))
    l = jnp.sum(p, axis=-1, keepdims=True)
    o_ref[...] = (_dot(p.astype(BF16), v_ref[...]) / l).astype(BF16)


def _attn_kernel(q_ref, kc_ref, vtc_ref, kl_ref, vtl_ref, o_ref, *, k_chunk):
    q = q_ref[...]
    tq = q.shape[0]
    parts = [(kc_ref, vtc_ref, 0, kc_ref.shape[0])]
    parts += [(kl_ref, vtl_ref, c0, k_chunk) for c0 in range(0, kl_ref.shape[0], k_chunk)]
    scores = lambda part: _dot_nt(part[0][part[2]:part[2] + part[3], :], q)
    m = jnp.full((1, tq), -jnp.inf, F32)
    l = jnp.zeros((1, tq), F32)
    acc = jnp.zeros((o_ref.shape[0], tq), F32)
    s_next = scores(parts[0])
    for n, (_, vt_ref, c0, size) in enumerate(parts):
        s = s_next
        if n + 1 < len(parts):
            s_next = scores(parts[n + 1])
        m_new = jnp.maximum(m, jnp.max(s, axis=0, keepdims=True))
        p = jnp.exp2(s - m_new)
        corr = jnp.exp2(m - m_new)
        l = l * corr + jnp.sum(p, axis=0, keepdims=True)
        acc = acc * corr + _dot(vt_ref[:, c0:c0 + size], p.astype(BF16))
        m = m_new
    o_ref[...] = (acc / l).astype(BF16)


def _merge_kernel(of_ref, ob_ref, bonus_ref, g_ref, ya_ref, ybl_ref, ybc_ref, gate_ref, x_ref,
                  mod_ref, gng_ref, gnb_ref, wa_ref, wb_ref, wc_ref, wo_ref, n2_ref,
                  rw_ref, rb_ref, xn_out, h2_out, comb_out, *, n_lat_tiles):
    mod = mod_ref[0]
    yb = jnp.where(pl.program_id(0) < n_lat_tiles, ybl_ref[...], ybc_ref[...])
    o = of_ref[...] + ob_ref[...]
    lane = lax.broadcasted_iota(jnp.int32, (1, LANE), 1)
    real = lane < C_HEAD
    ycs = []
    for hh in range(C_HEADS):
        sl = slice(hh * LANE, (hh + 1) * LANE)
        oh = o[:, sl]
        mean = jnp.sum(oh, axis=-1, keepdims=True) * (1.0 / C_HEAD)
        dev = jnp.where(real, oh - mean, 0.0)
        var = jnp.sum(dev * dev, axis=-1, keepdims=True) * (1.0 / C_HEAD)
        y = dev * lax.rsqrt(var + C_GN_EPS) * gng_ref[:, sl] + gnb_ref[:, sl]
        ycs.append(((y + bonus_ref[:, sl]) * g_ref[:, sl]).astype(BF16))
    yc = jnp.concatenate(ycs, axis=1)
    gate = gate_ref[...].astype(F32)
    merged = (gate[:, :D_MODEL] * _dot(ya_ref[...], wa_ref[...])
              + gate[:, D_MODEL:2 * D_MODEL] * _dot(yb, wb_ref[...])
              + gate[:, 2 * D_MODEL:] * _dot(yc, wc_ref[...]))
    xn = x_ref[...] + mod[2:3] * _dot(merged.astype(BF16), wo_ref[...])
    xn_out[...] = xn
    h2 = _normmod(xn, n2_ref[...], mod, 3).astype(BF16)
    h2_out[...] = h2

    scores = 1.0 / (1.0 + jnp.exp(-_dot(h2, rw_ref[...])))
    neg = -jnp.inf
    sel = jnp.where(lane < N_EXPERTS, scores + rb_ref[...], neg)

    lane_f = lane.astype(F32)

    def top1(s):
        mx = jnp.max(s, axis=-1, keepdims=True)
        idx = jnp.min(jnp.where(s == mx, lane_f, float(LANE)), axis=-1, keepdims=True)
        return mx, idx.astype(jnp.int32)

    best = None
    for g in range(N_GROUPS):
        in_g = (lane >= g * EXPERTS_PER_GROUP) & (lane < (g + 1) * EXPERTS_PER_GROUP)
        s = jnp.where(in_g, sel, neg)
        m1, i1 = top1(s)
        m2, _ = top1(jnp.where(lane == i1, neg, s))
        score = m1 + m2
        if best is None:
            best, gidx = score, jnp.zeros_like(i1)
        else:
            better = score > best
            gidx = jnp.where(better, g, gidx)
            best = jnp.where(better, score, best)
    lo = gidx * EXPERTS_PER_GROUP
    s = jnp.where((lane >= lo) & (lane < lo + EXPERTS_PER_GROUP), sel, neg)
    _, i1 = top1(s)
    _, i2 = top1(jnp.where(lane == i1, neg, s))
    picked = jnp.where((lane == i1) | (lane == i2), scores, 0.0)
    comb_out[...] = picked / jnp.sum(picked, axis=-1, keepdims=True)


def _moe_kernel(cnt_ref, h2_ref, comb_ref, xn_ref, *rest, last):
    mod_refs = rest[:MOE_SUB]
    w1_ref, w3_ref, w2_ref, tri_ref, fg_ref, out_ref, rank_ref, rank_t_ref, comb_t_ref = rest[MOE_SUB:]
    i = pl.program_id(0)
    e = pl.program_id(1)
    tm = h2_ref.shape[0]
    lane = lax.broadcasted_iota(jnp.int32, (1, LANE), 1)

    @pl.when(e == 0)
    def _():
        comb = comb_ref[...]
        picked = jnp.where(comb > 0.0, 1.0, 0.0).astype(BF16)
        rank = _dot(tri_ref[...], picked)
        rank_ref[...] = rank
        rank_t_ref[...] = rank.T
        comb_t_ref[...] = comb.T
        out_ref[...] = jnp.zeros_like(out_ref)

    comb_e = jnp.sum(jnp.where(lane == e, comb_ref[...], 0.0), axis=-1, keepdims=True)
    rank_col = jnp.sum(jnp.where(lane == e, rank_ref[...], 0.0), axis=-1, keepdims=True)
    rank_row = rank_t_ref[pl.ds(e, 1), :]
    picked_row = comb_t_ref[pl.ds(e, 1), :] > 0.0
    slot_col = lax.broadcasted_iota(jnp.int32, (MOE_CAP, 1), 0)
    slot_row = lax.broadcasted_iota(jnp.int32, (1, MOE_CAP), 1)
    n_blocks = lax.shift_right_logical(cnt_ref[i, e] + (MOE_CAP - 1), MOE_CAP.bit_length() - 1)

    def block(blk, carry):
        base = blk * MOE_CAP
        gather = jnp.where(picked_row & (rank_row == (slot_col + base).astype(F32)), 1.0, 0.0).astype(BF16)
        xg = _dot(gather, h2_ref[...]).astype(BF16)
        a = _dot(xg, w1_ref[0, 0].astype(BF16))
        he = (a * _sigmoid(a) * _dot(xg, w3_ref[0, 0].astype(BF16))).astype(BF16)
        y = _dot(he, w2_ref[0, 0].astype(BF16)).astype(BF16)
        scatter = jnp.where((comb_e > 0.0) & (rank_col == (slot_row + base).astype(F32)), 1.0, 0.0).astype(BF16)
        out_ref[...] += comb_e * _dot(scatter, y)
        return carry

    lax.fori_loop(0, n_blocks, block, 0)

    @pl.when(e == N_EXPERTS - 1)
    def _():
        for s, m_ref in enumerate(mod_refs):
            rows = slice(s * SEG_ROWS, (s + 1) * SEG_ROWS)
            y = xn_ref[rows] + m_ref[0][5:6] * out_ref[rows]
            if last:
                y = _rms(y) * fg_ref[...]
            out_ref[rows] = y


def _head_pad_cols(w, width):
    lead = w.shape[:-1]
    w = w.reshape(lead + (C_HEADS, width))
    w = jnp.pad(w, [(0, 0)] * len(lead) + [(0, 0), (0, LANE - width)])
    return w.reshape(lead + (HP,))


def _head_pad_rows(w, width):
    n = w.shape[-1]
    w = w.reshape(C_HEADS, width, n)
    w = jnp.pad(w, ((0, 0), (0, LANE - width), (0, 0)))
    return w.reshape(HP, n)


def _split_rope(w):
    lead = w.shape[:-1]
    w = w.reshape(lead + (2, 2, ROPE_FREQS))
    x1 = w[..., 0, :].reshape(lead + (2 * ROPE_FREQS,))
    x2 = w[..., 1, :].reshape(lead + (2 * ROPE_FREQS,))
    return x1, x2


def _rope_tables(n_ctx, t):
    rows = t // GRID_W
    row = jnp.repeat(jnp.arange(rows, dtype=F32), GRID_W)
    col = jnp.tile(jnp.arange(GRID_W, dtype=F32), rows)
    inv = jnp.power(ROPE_BASE, -jnp.arange(ROPE_FREQS, dtype=F32) / ROPE_FREQS)
    ang = jnp.concatenate([row[:, None] * inv, col[:, None] * inv], axis=1)
    cos = jnp.concatenate([jnp.cos(ang), jnp.ones((n_ctx, 2 * ROPE_FREQS), F32)], axis=0)
    sin = jnp.concatenate([jnp.sin(ang), jnp.zeros((n_ctx, 2 * ROPE_FREQS), F32)], axis=0)
    n = n_ctx + t
    one = jnp.ones((n, B_NOPE), F32)
    zero = lambda w: jnp.zeros((n, w), F32)
    qs = MLA_SCALE * math.log2(math.e)
    cq = jnp.concatenate([one, cos, cos, zero(LANE - B_NOPE - B_ROPE)], axis=1) * qs
    sq = jnp.concatenate([zero(B_NOPE), -sin, sin, zero(LANE - B_NOPE - B_ROPE)], axis=1) * qs
    ck = jnp.concatenate([cos, cos, zero(LANE - B_ROPE)], axis=1)
    sk = jnp.concatenate([-sin, sin, zero(LANE - B_ROPE)], axis=1)
    return cq, sq, ck, sk


def _layer_weights(p, l):
    w = {}
    w_in = p['w_in'][l]
    d = w_in.shape[0]
    o = 0
    w_q = w_in[:, o:o + B_Q_LORA]; o += B_Q_LORA
    w_kv = w_in[:, o:o + B_KV_LORA]; o += B_KV_LORA
    w_kr = w_in[:, o:o + B_ROPE]; o += B_ROPE
    w_c = w_in[:, o:o + C_COLS]; o += C_COLS
    w_a = w_in[:, o:o + A_COLS]; o += A_COLS
    w_g = w_in[:, o:o + G_COLS]

    k1, k2 = _split_rope(w_kr)
    zpad = jnp.zeros((d, LANE - B_ROPE), F32)
    w['win_b'] = jnp.concatenate([w_q, w_kv, k1, k2, zpad, k2, k1, zpad], axis=1).astype(BF16)

    wuq = p['b_w_uq'][l].reshape(B_Q_LORA, B_HEADS, B_NOPE + B_ROPE)
    q1, q2 = _split_rope(wuq[..., B_NOPE:])
    qz = jnp.zeros((B_Q_LORA, B_HEADS, LANE - B_NOPE - B_ROPE), F32)
    w['wq'] = jnp.concatenate([wuq[..., :B_NOPE], q1, q2, qz], axis=-1).reshape(B_Q_LORA, HP).astype(BF16)
    w['wqs'] = jnp.concatenate([wuq[..., :B_NOPE], q2, q1, qz], axis=-1).reshape(B_Q_LORA, HP).astype(BF16)
    wukv = p['b_w_ukv'][l].reshape(B_KV_LORA, B_HEADS, B_NOPE + B_VDIM)
    w['wk'] = _head_pad_cols(wukv[..., :B_NOPE].reshape(B_KV_LORA, -1), B_NOPE).astype(BF16)
    w['wv'] = _head_pad_cols(wukv[..., B_NOPE:].reshape(B_KV_LORA, -1), B_VDIM).astype(BF16)
    place = jnp.zeros((LANE, B_HEADS, LANE), F32)
    idx = jnp.arange(B_ROPE)
    place = place.at[idx, :, B_NOPE + idx].set(1.0)
    w['place'] = place.reshape(LANE, HP).astype(BF16)
    w['qg'] = p['b_q_norm'][l][None]
    w['kvg'] = p['b_kv_norm'][l][None]

    w['wa'] = w_a.astype(BF16)
    w['wg'] = w_g.astype(BF16)
    w['vg'] = p['a_v_gain'][l][None]
    w['ws'] = p['a_ws'][l].astype(BF16)
    w['bsb'] = jnp.broadcast_to(p['a_bs'][l].T[:, :, None], (A_CHUNK, A_GROUPS, LANE)).reshape(A_CHUNK, A_WIDTH)

    cw = C_WIDTH
    pieces = [_head_pad_cols(w_c[:, i * cw:(i + 1) * cw], C_HEAD) for i in range(3)]
    w['wc'] = jnp.concatenate(pieces + [w_c[:, 3 * cw:]], axis=1).astype(BF16)
    mu = p['c_mu'][l]
    w['mu'] = jnp.concatenate([_head_pad_cols(mu[i * cw:(i + 1) * cw], C_HEAD) for i in range(3)]
                              + [mu[3 * cw:]])[None]

    def lora_block(m, rank):
        m = _head_pad_cols(m, C_HEAD)
        z = jnp.zeros_like(m[0])
        return jnp.stack([jnp.concatenate([m[0], z], axis=0), jnp.concatenate([z, m[1]], axis=0)]).astype(BF16)

    w['w2'] = lora_block(p['c_w2'][l], C_DECAY_LORA)
    w['a2'] = lora_block(p['c_a2'][l], C_AAA_LORA)
    w['w0'] = _head_pad_cols(p['c_w0'][l], C_HEAD)
    w['a0'] = _head_pad_cols(p['c_a0'][l], C_HEAD)
    w['kk'] = _head_pad_cols(p['c_k_k'][l], C_HEAD)[None]
    w['ka'] = _head_pad_cols(p['c_k_a'][l], C_HEAD)[None]
    w['rk'] = _head_pad_cols(p['c_r_k'][l].reshape(-1), C_HEAD)[None]
    w['g2'] = _head_pad_cols(p['c_g2'][l], C_HEAD).astype(BF16)
    w['gng'] = _head_pad_cols(p['c_gn_g'][l], C_HEAD)[None]
    w['gnb'] = _head_pad_cols(p['c_gn_b'][l], C_HEAD)[None]

    w['wup_a'] = p['w_up_a'][l].astype(BF16)
    w['wup_b'] = p['w_up_b'][l].astype(BF16)
    w['wup_c'] = _head_pad_rows(p['w_up_c'][l], C_HEAD).astype(BF16)
    w['wo'] = p['w_o'][l].astype(BF16)
    w['g1'] = p['norm1_g'][l][None]
    w['n2'] = p['norm2_g'][l][None]
    return w


def _layer(xs, mods, w, moe_w, tabs, rw, rb, fg, *, layer, bn, t, n_ctx, last):
    m, d = xs.shape
    n_tiles = m // TM
    n_lat = bn * t
    n_lat_tiles = n_lat // TM
    tpl = t // TM
    row = lambda i: (i, 0)
    seg_of = lambda blk: jnp.where(blk < n_lat // SEG_ROWS, 1 + blk // (t // SEG_ROWS), 0)
    seg = lambda i: (seg_of(i * (TM // SEG_ROWS)), 0, 0)
    x_spec = pl.BlockSpec((TM, d), row)
    mod_spec = pl.BlockSpec((1, 6, d), seg)
    hp_spec = pl.BlockSpec((TM, HP), row)
    hp2_spec = pl.BlockSpec((2, TM, HP), lambda i: (0, i, 0))
    hp_f32 = jax.ShapeDtypeStruct((m, HP), F32)
    hp_bf16 = jax.ShapeDtypeStruct((m, HP), BF16)
    hp2_bf16 = jax.ShapeDtypeStruct((2, m, HP), BF16)
    cq, sq, ck, sk = tabs
    tab_spec = pl.BlockSpec(
        (TM, LANE), lambda i: (jnp.where(i < n_lat_tiles, i % tpl, tpl + i - n_lat_tiles), 0))

    q, k, v = pl.pallas_call(
        _mla_kernel,
        grid=(n_tiles,),
        in_specs=[x_spec, _full((1, d)), mod_spec, _full(w['win_b'].shape), _full((1, B_Q_LORA)),
                  _full((1, B_KV_LORA)), _full(w['wq'].shape), _full(w['wqs'].shape),
                  _full(w['wk'].shape), _full(w['wv'].shape), _full(w['place'].shape),
                  tab_spec, tab_spec, tab_spec, tab_spec],
        out_specs=[hp_spec, hp_spec, hp_spec],
        out_shape=[hp_bf16, hp_bf16, hp_bf16],
        compiler_params=_params(1),
        name="mla_proj",
    )(xs, w['g1'], mods, w['win_b'], w['qg'], w['kvg'], w['wq'], w['wqs'], w['wk'], w['wv'],
      w['place'], cq, sq, ck, sk)

    ya, gate = pl.pallas_call(
        _gmlp_kernel,
        grid=(n_tiles,),
        in_specs=[x_spec, _full((1, d)), mod_spec, _full(w['wa'].shape), _full(w['wg'].shape),
                  _full((1, A_WIDTH)), _full(w['ws'].shape), _full(w['bsb'].shape)],
        out_specs=[pl.BlockSpec((TM, A_WIDTH), row), pl.BlockSpec((TM, G_COLS), row)],
        out_shape=[jax.ShapeDtypeStruct((m, A_WIDTH), BF16), jax.ShapeDtypeStruct((m, G_COLS), BF16)],
        compiler_params=_params(1),
        name="gmlp_gates",
    )(xs, w['g1'], mods, w['wa'], w['wg'], w['vg'], w['ws'], w['bsb'])

    tf = TM_FEAT
    sub = tf // 8
    seq_starts = tuple(b * t for b in range(bn)) + tuple(n_lat + b * n_ctx for b in range(bn))
    seq_ends = tuple(s - 1 for s in seq_starts[1:]) + (m - 1,)
    assert all(s % tf == 0 for s in seq_starts)
    f_hp = pl.BlockSpec((tf, HP), row)
    f_hp2 = pl.BlockSpec((2, tf, HP), lambda i: (0, i, 0))
    pe_spec = pl.BlockSpec((2, 8 * (tf // SCAN_C), HP), lambda i: (0, i, 0))
    pe_shape = jax.ShapeDtypeStruct((2, 8 * (m // SCAN_C), HP), F32)
    vv, ab, bh, kh, rbar, pe, bonus, gg = pl.pallas_call(
        functools.partial(_feat_kernel, seq_starts=seq_starts, seq_ends=seq_ends),
        grid=(m // tf,),
        in_specs=[pl.BlockSpec((tf, d), row),
                  pl.BlockSpec((8, d), lambda i: (jnp.maximum(i * sub - 1, 0), 0)),
                  pl.BlockSpec((8, d), lambda i: (jnp.minimum((i + 1) * sub, m // 8 - 1), 0)),
                  _full((1, d)), pl.BlockSpec((1, 6, d), lambda i: (seg_of(i * (tf // SEG_ROWS)), 0, 0)),
                  _full(w['wc'].shape), _full(w['mu'].shape),
                  _full((2, HP)), _full(w['w2'].shape), _full((2, HP)), _full(w['a2'].shape),
                  _full((1, HP)), _full((1, HP)), _full((1, HP)), _full(w['g2'].shape)],
        out_specs=[f_hp, f_hp2, f_hp2, f_hp2, f_hp2, pe_spec, f_hp, f_hp],
        out_shape=[hp_bf16, hp2_bf16, hp2_bf16, hp2_bf16, hp2_bf16, pe_shape, hp_f32, hp_f32],
        compiler_params=_params(1),
        name="rwkv_features",
    )(xs, xs, xs, w['g1'], mods, w['wc'], w['mu'], w['w0'], w['w2'], w['a0'], w['a2'],
      w['kk'], w['ka'], w['rk'], w['g2'])

    blk_rows = SCAN_C * SCAN_STEP
    nctx_c = n_ctx // blk_rows
    nlat_c = t // blk_rows
    nc = nctx_c + nlat_c

    def chunk_of(dd, b, j):
        ctx_base = n_lat // blk_rows + b * nctx_c
        lat_base = b * nlat_c
        if dd == 0:
            return jnp.where(j < nctx_c, ctx_base + j, lat_base + j - nctx_c)
        return jnp.where(j < nctx_c, ctx_base + nctx_c - 1 - j, lat_base + nc - 1 - j)

    def tok_spec(dd):
        return pl.BlockSpec((blk_rows, HP), lambda b, j: (chunk_of(dd, b, j), 0))

    def dir_spec(dd, rows=blk_rows):
        return pl.BlockSpec((1, rows, HP), lambda b, j: (dd, chunk_of(dd, b, j), 0))

    o_f, o_b = pl.pallas_call(
        _scan_kernel,
        grid=(bn, nc),
        in_specs=[tok_spec(0), tok_spec(1)] + [dir_spec(dd) for _ in range(4) for dd in range(2)]
                 + [dir_spec(0, 8 * SCAN_STEP), dir_spec(1, 8 * SCAN_STEP)],
        out_specs=[tok_spec(0), tok_spec(1)],
        out_shape=[hp_f32, hp_f32],
        scratch_shapes=[pltpu.VMEM((2, C_HEADS, LANE, LANE), F32)],
        compiler_params=_params(2),
        name="rwkv_scan",
    )(vv, vv, ab, ab, bh, bh, kh, kh, rbar, rbar, pe, pe)

    ctx_blk = n_lat // n_ctx
    tq = ATTN_TQ
    vt = v.reshape(m, B_HEADS, LANE)[:, :, :B_VDIM].transpose(1, 2, 0).reshape(B_HEADS * B_VDIM, m)
    ctx_k = pl.BlockSpec((n_ctx, LANE), lambda b, hh, qi: (ctx_blk + b, hh))
    ctx_vt = pl.BlockSpec((B_VDIM, n_ctx), lambda b, hh, qi: (hh, ctx_blk + b))
    lat_k = pl.BlockSpec((t, LANE), lambda b, hh, qi: (b, hh))
    lat_vt = pl.BlockSpec((B_VDIM, t), lambda b, hh, qi: (hh, b))
    q_lat = pl.BlockSpec((tq, LANE), lambda b, hh, qi: (b * (t // tq) + qi, hh))
    yb_lat_t = pl.pallas_call(
        functools.partial(_attn_kernel, k_chunk=ATTN_KC),
        grid=(bn, B_HEADS, t // tq),
        in_specs=[q_lat, ctx_k, ctx_vt, lat_k, lat_vt],
        out_specs=pl.BlockSpec((B_VDIM, tq), lambda b, hh, qi: (hh, b * (t // tq) + qi)),
        out_shape=jax.ShapeDtypeStruct((B_HEADS * B_VDIM, n_lat), BF16),
        compiler_params=_params(3),
        name="mla_attention",
    )(q, k, vt, k, vt)
    yb_lat = yb_lat_t.T
    ctx_q = pl.BlockSpec((n_ctx, LANE), lambda b, hh: (ctx_blk + b, hh))
    yb_ctx = pl.pallas_call(
        _attn_ctx_kernel,
        grid=(bn, B_HEADS),
        in_specs=[ctx_q, ctx_q, ctx_q],
        out_specs=pl.BlockSpec((n_ctx, LANE), lambda b, hh: (b, hh)),
        out_shape=jax.ShapeDtypeStruct((bn * n_ctx, HP), BF16),
        compiler_params=_params(2),
        name="mla_attention_ctx",
    )(q, k, v)
    yb_ctx = yb_ctx.reshape(bn * n_ctx, B_HEADS, LANE)[:, :, :B_VDIM].reshape(bn * n_ctx, B_HEADS * B_VDIM)

    n_ctx_tiles = n_tiles - n_lat_tiles
    yb_w = B_HEADS * B_VDIM
    xn, h2, comb = pl.pallas_call(
        functools.partial(_merge_kernel, n_lat_tiles=n_lat_tiles),
        grid=(n_tiles,),
        in_specs=[hp_spec, hp_spec, hp_spec, hp_spec, pl.BlockSpec((TM, A_WIDTH), row),
                  pl.BlockSpec((TM, yb_w), lambda i: (jnp.minimum(i, n_lat_tiles - 1), 0)),
                  pl.BlockSpec((TM, yb_w), lambda i: (jnp.clip(i - n_lat_tiles, 0, n_ctx_tiles - 1), 0)),
                  pl.BlockSpec((TM, G_COLS), row), x_spec, mod_spec, _full((1, HP)), _full((1, HP)),
                  _full(w['wup_a'].shape), _full(w['wup_b'].shape), _full(w['wup_c'].shape),
                  _full(w['wo'].shape), _full((1, d)), _full(rw.shape), _full(rb.shape)],
        out_specs=[x_spec, x_spec, pl.BlockSpec((TM, LANE), row)],
        out_shape=[jax.ShapeDtypeStruct((m, d), F32), jax.ShapeDtypeStruct((m, d), BF16),
                   jax.ShapeDtypeStruct((m, LANE), F32)],
        compiler_params=_params(1),
        name="merge_router",
    )(o_f, o_b, bonus, gg, ya, yb_lat, yb_ctx, gate, xs, mods, w['gng'], w['gnb'], w['wup_a'], w['wup_b'],
      w['wup_c'], w['wo'], w['n2'], rw, rb)

    tm_moe = MOE_SUB * SEG_ROWS
    counts = jnp.sum((comb > 0.0).reshape(m // tm_moe, tm_moe, LANE), axis=1, dtype=jnp.int32)
    tri = jnp.tril(jnp.ones((tm_moe, tm_moe), BF16), -1)
    mrow = lambda i, e, cnt: (i, 0)
    once = pl.Buffered(1)
    mseg = lambda s: pl.BlockSpec((1, 6, d), lambda i, e, cnt: (seg_of(i * MOE_SUB + s), 0, 0))
    wexp = lambda shape: pl.BlockSpec((1, 1) + shape, lambda i, e, cnt: (layer, e, 0, 0))
    grid_spec = pltpu.PrefetchScalarGridSpec(
        num_scalar_prefetch=1,
        grid=(m // tm_moe, N_EXPERTS),
        in_specs=[pl.BlockSpec((tm_moe, d), mrow), pl.BlockSpec((tm_moe, LANE), mrow),
                  pl.BlockSpec((tm_moe, d), mrow, pipeline_mode=once)]
                 + [mseg(s) for s in range(MOE_SUB)]
                 + [wexp((d, D_EXPERT)), wexp((d, D_EXPERT)), wexp((D_EXPERT, d)),
                    pl.BlockSpec((tm_moe, tm_moe), lambda i, e, cnt: (0, 0), pipeline_mode=once),
                    pl.BlockSpec((1, d), lambda i, e, cnt: (0, 0))],
        out_specs=pl.BlockSpec((tm_moe, d), mrow),
        scratch_shapes=[pltpu.VMEM((tm_moe, LANE), F32), pltpu.VMEM((LANE, tm_moe), F32),
                        pltpu.VMEM((LANE, tm_moe), F32)],
    )
    return pl.pallas_call(
        functools.partial(_moe_kernel, last=last),
        grid_spec=grid_spec,
        out_shape=jax.ShapeDtypeStruct((m, d), F32),
        compiler_params=_params(2),
        name="moe",
    )(counts, h2, comb, xn, *([mods] * MOE_SUB), moe_w[0], moe_w[1], moe_w[2], tri, fg)


def kernel(x, c, ctx, c_ctx, mod_w, mod_b, norm1_g, norm2_g, w_in, a_v_gain, a_ws, a_bs, b_q_norm, b_w_uq, b_kv_norm, b_w_ukv, c_mu, c_w0, c_w2, c_a0, c_a2, c_g2, c_k_k, c_k_a, c_r_k, c_gn_g, c_gn_b, w_up_a, w_up_b, w_up_c, w_o, router_w, router_b, moe_w1, moe_w3, moe_w2, final_g):
    p = dict(w_in=w_in, a_v_gain=a_v_gain, a_ws=a_ws, a_bs=a_bs, b_q_norm=b_q_norm, b_w_uq=b_w_uq,
             b_kv_norm=b_kv_norm, b_w_ukv=b_w_ukv, c_mu=c_mu, c_w0=c_w0, c_w2=c_w2, c_a0=c_a0,
             c_a2=c_a2, c_g2=c_g2, c_k_k=c_k_k, c_k_a=c_k_a, c_r_k=c_r_k, c_gn_g=c_gn_g,
             c_gn_b=c_gn_b, w_up_a=w_up_a, w_up_b=w_up_b, w_up_c=w_up_c, w_o=w_o,
             norm1_g=norm1_g, norm2_g=norm2_g)
    bn, t, d = x.shape
    n_ctx = ctx.shape[1]
    depth = mod_w.shape[0]
    m = bn * (t + n_ctx)
    assert n_ctx == SEG_ROWS and (bn * n_ctx) % TM == 0 and t % TM == 0 and t % ATTN_TQ == 0
    assert t % ATTN_KC == 0 and m % (MOE_SUB * SEG_ROWS) == 0 and 1 + bn <= 8

    cvec = jnp.concatenate([c_ctx[None], c, jnp.zeros((8 - 1 - bn, d), F32)], axis=0)
    mods = _modulation(cvec, mod_w, mod_b).reshape(depth, 8, 6, d)

    tabs = _rope_tables(bn * n_ctx, t)
    rw = jnp.pad(router_w, ((0, 0), (0, LANE - N_EXPERTS))).astype(BF16)
    rb = jnp.pad(router_b, (0, LANE - N_EXPERTS))[None]
    fg = final_g[None]

    xs = jnp.concatenate([x.reshape(bn * t, d), ctx.reshape(bn * n_ctx, d)], axis=0)
    for l in range(depth):
        xs = _layer(xs, mods[l], _layer_weights(p, l), (moe_w1, moe_w3, moe_w2), tabs, rw, rb, fg,
                    layer=l, bn=bn, t=t, n_ctx=n_ctx, last=(l == depth - 1))
    return xs[:bn * t].reshape(bn, t, d)
```

```python
import functools
import math

import jax
import jax.numpy as jnp
from jax import lax
from jax.experimental import pallas as pl
from jax.experimental.pallas import tpu as pltpu

F32 = jnp.float32
BF16 = jnp.bfloat16
HIGHEST = lax.Precision.HIGHEST

D_MODEL = 1024
GRID_W = 64
EPS = 1e-6

A_WIDTH = 512
A_GROUPS = 4
A_CHUNK = 128

B_HEADS = 8
B_Q_LORA = 384
B_KV_LORA = 256
B_NOPE = 64
B_ROPE = 32
B_VDIM = 64
ROPE_BASE = 10000.0
ROPE_FREQS = B_ROPE // 4
MLA_SCALE = (B_NOPE + B_ROPE) ** -0.5

C_HEADS = 8
C_HEAD = 64
C_WIDTH = C_HEADS * C_HEAD
C_DECAY_LORA = 64
C_AAA_LORA = 64
C_GATE_LORA = 128
C_GN_EPS = 64e-5
DECAY_SCALE = 0.6065306597126334

B_COLS = B_Q_LORA + B_KV_LORA + B_ROPE
C_COLS = 3 * C_WIDTH + 2 * C_DECAY_LORA + 2 * C_AAA_LORA + C_GATE_LORA
A_COLS = 2 * A_WIDTH
G_COLS = 3 * D_MODEL

N_EXPERTS = 16
N_GROUPS = 4
EXPERTS_PER_GROUP = 4
D_EXPERT = 512

LANE = 128
HP = C_HEADS * LANE
TM = 512
TM_FEAT = 256
SEG_ROWS = 256
SCAN_C = 64
SCAN_STEP = 2
CUM_ROWS = 256
MOE_SUB = 6
MOE_CAP = 256
ATTN_TQ = 512
ATTN_KC = 1024
VMEM_LIMIT = 52 * 1024 * 1024


def _dot(a, b, precision=None):
    return jnp.dot(a, b, preferred_element_type=F32, precision=precision)


def _dot_nt(a, b, precision=None):
    return lax.dot_general(a, b, (((1,), (1,)), ((), ())),
                           preferred_element_type=F32, precision=precision)


def _dot_tn(a, b, precision=None):
    return lax.dot_general(a, b, (((0,), (0,)), ((), ())),
                           preferred_element_type=F32, precision=precision)


def _rms(x, eps=EPS):
    return x * lax.rsqrt(jnp.mean(x * x, axis=-1, keepdims=True) + eps)


def _normmod(x, g, mod, first):
    return _rms(x) * g * (1.0 + mod[first + 1:first + 2]) + mod[first:first + 1]


def _sigmoid(x):
    return 0.5 * jnp.tanh(0.5 * x) + 0.5


def _params(n_grid):
    return pltpu.CompilerParams(
        dimension_semantics=("arbitrary",) * n_grid, vmem_limit_bytes=VMEM_LIMIT)


def _full(shape):
    n = len(shape)
    return pl.BlockSpec(shape, lambda *_: (0,) * n)


def _mod_kernel(c_ref, w_ref, b_ref, o_ref):
    c = c_ref[...]
    act = (c * _sigmoid(c)).astype(BF16)
    o_ref[0] = _dot(act, w_ref[0].astype(BF16)) + b_ref[0]


def _modulation(cvec, mod_w, mod_b):
    depth, d, n = mod_w.shape
    tn = 1536
    return pl.pallas_call(
        _mod_kernel,
        grid=(depth, n // tn),
        in_specs=[
            pl.BlockSpec((8, d), lambda l, j: (0, 0)),
            pl.BlockSpec((1, d, tn), lambda l, j: (l, 0, j)),
            pl.BlockSpec((1, 1, tn), lambda l, j: (l, 0, j)),
        ],
        out_specs=pl.BlockSpec((1, 8, tn), lambda l, j: (l, 0, j)),
        out_shape=jax.ShapeDtypeStruct((depth, 8, n), F32),
        compiler_params=_params(2),
        name="modulation",
    )(cvec, mod_w, mod_b.reshape(depth, 1, n))


def _mla_kernel(x_ref, g1_ref, mod_ref, win_ref, qg_ref, kvg_ref, wq_ref, wqs_ref,
                wk_ref, wv_ref, place_ref, cq_ref, sq_ref, ck_ref, sk_ref,
                q_out, k_out, vt_out):
    h = _normmod(x_ref[...], g1_ref[...], mod_ref[0], 0).astype(BF16)
    z = _dot(h, win_ref[...])
    zq = z[:, :B_Q_LORA]
    zkv = z[:, B_Q_LORA:B_Q_LORA + B_KV_LORA]
    kr_a = z[:, B_Q_LORA + B_KV_LORA:B_Q_LORA + B_KV_LORA + LANE]
    kr_b = z[:, B_Q_LORA + B_KV_LORA + LANE:]
    qn = (_rms(zq) * qg_ref[...]).astype(BF16)
    kvn = (_rms(zkv) * kvg_ref[...]).astype(BF16)
    q1 = _dot(qn, wq_ref[...])
    q2 = _dot(qn, wqs_ref[...])
    kr = (kr_a * ck_ref[...] + kr_b * sk_ref[...]).astype(BF16)
    k = _dot(kvn, wk_ref[...]) + _dot(kr, place_ref[...])
    cq = cq_ref[...]
    sq = sq_ref[...]
    for hh in range(B_HEADS):
        sl = slice(hh * LANE, (hh + 1) * LANE)
        q_out[:, sl] = (q1[:, sl] * cq + q2[:, sl] * sq).astype(BF16)
    k_out[...] = k.astype(BF16)
    vt_out[...] = _dot_nt(wv_ref[...], kvn).astype(BF16)


def _gelu_tanh(x):
    return 0.5 * x * (1.0 + jnp.tanh(math.sqrt(2.0 / math.pi) * (x + 0.044715 * (x * x * x))))


def _gmlp_kernel(x_ref, g1_ref, mod_ref, wa_ref, wg_ref, vg_ref, ws_ref, bsb_ref,
                 ya_out, gate_out):
    h = _normmod(x_ref[...], g1_ref[...], mod_ref[0], 0).astype(BF16)
    gate_out[...] = _sigmoid(_dot(h, wg_ref[...])).astype(BF16)
    zg = _gelu_tanh(_dot(h, wa_ref[...]))
    u = zg[:, :A_WIDTH]
    v = (_rms(zg[:, A_WIDTH:]) * vg_ref[...]).astype(BF16)
    for c in range(TM // A_CHUNK):
        rows = slice(c * A_CHUNK, (c + 1) * A_CHUNK)
        for g in range(A_GROUPS):
            cols = slice(g * LANE, (g + 1) * LANE)
            mixed = _dot(ws_ref[g], v[rows, cols]) + bsb_ref[:, cols]
            ya_out[rows, cols] = (u[rows, cols] * mixed).astype(BF16)


def _feat_kernel(x_ref, xp_ref, xn_ref, g1_ref, mod_ref, wc_ref, mu_ref,
                 w0_ref, w2_ref, a0_ref, a2_ref, kk_ref, ka_ref, rk_ref, g2_ref,
                 v_out, ab_out, bh_out, kh_out, rb_out, pe_out, bonus_out, g_out,
                 *, seq_starts, seq_ends):
    mod = mod_ref[0]
    g1 = g1_ref[...]
    h = _normmod(x_ref[...], g1, mod, 0).astype(BF16)
    hp = _normmod(xp_ref[...], g1, mod, 0).astype(BF16)
    hn = _normmod(xn_ref[...], g1, mod, 0).astype(BF16)
    tm = x_ref.shape[0]
    row = lax.broadcasted_iota(jnp.int32, (tm, 1), 0)
    first = pl.program_id(0) * tm
    has_prev = 1.0 - functools.reduce(jnp.maximum, [jnp.where(first == s, 1.0, 0.0) for s in seq_starts])
    has_next = 1.0 - functools.reduce(jnp.maximum,
                                      [jnp.where(first + tm - 1 == e, 1.0, 0.0) for e in seq_ends])

    def shifted_proj(cols):
        w = wc_ref[:, cols]
        z = _dot(h, w)
        z_up = pltpu.roll(z, 1, 0)
        z_dn = pltpu.roll(z, tm - 1, 0)
        z_up = jnp.concatenate(
            [jnp.where(row[:8] == 0, _dot(hp, w)[7:8] * has_prev, z_up[:8]), z_up[8:]], axis=0)
        z_dn = jnp.concatenate(
            [z_dn[:tm - 8], jnp.where(row[tm - 8:] == tm - 1, _dot(hn, w)[0:1] * has_next, z_dn[tm - 8:])],
            axis=0)
        return z + mu_ref[:, cols] * (0.5 * (z_up + z_dn) - z)

    zl = shifted_proj(slice(3 * HP, 3 * HP + 3 * LANE))
    zw = jnp.tanh(zl[:, :LANE]).astype(BF16)
    za = zl[:, LANE:2 * LANE].astype(BF16)
    zg = _sigmoid(zl[:, 2 * LANE:]).astype(BF16)
    w_logit = [w0_ref[d:d + 1] + _dot(zw, w2_ref[d]) for d in range(2)]
    a_logit = [a0_ref[d:d + 1] + _dot(za, a2_ref[d]) for d in range(2)]
    g_out[...] = _dot(zg, g2_ref[...])
    r = shifted_proj(slice(0, HP))

    ti = lax.broadcasted_iota(jnp.int32, (CUM_ROWS, CUM_ROWS), 0)
    tj = lax.broadcasted_iota(jnp.int32, (CUM_ROWS, CUM_ROWS), 1)
    shift = SCAN_C.bit_length() - 1
    same = lax.shift_right_logical(ti, shift) == lax.shift_right_logical(tj, shift)
    lws, css = [], []
    for d in range(2):
        lw = -DECAY_SCALE * _sigmoid(w_logit[d])
        tri = jnp.where(same & ((tj <= ti) if d == 0 else (tj >= ti)), 1.0, 0.0).astype(BF16)
        hi = lw.astype(BF16)
        lo = (lw - hi.astype(F32)).astype(BF16)
        css.append(jnp.concatenate(
            [_dot(tri, hi[rows]) + _dot(tri, lo[rows])
             for rows in (slice(c0, c0 + CUM_ROWS) for c0 in range(0, tm, CUM_ROWS))], axis=0))
        lws.append(lw)
    k = shifted_proj(slice(HP, 2 * HP))
    v = shifted_proj(slice(2 * HP, 3 * HP))
    v_out[...] = v.astype(BF16)

    kk = k * kk_ref[...]
    ka = ka_ref[...]
    rk = rk_ref[...]
    a = [_sigmoid(a_logit[d]) for d in range(2)]
    kd = [k * (1.0 + (a[d] - 1.0) * ka) for d in range(2)]
    p_in, p_prev, p_inv = [], [], []
    for d in range(2):
        p_in.append(jnp.exp(css[d]))
        p_prev.append(jnp.exp(css[d] - lws[d]))
        p_inv.append(jnp.exp(-css[d]))
        kh_out[d] = (kd[d] * p_inv[d]).astype(BF16)
        rb_out[d] = (r * p_in[d]).astype(BF16)
        for cc in range(tm // SCAN_C):
            last = cc * SCAN_C + (SCAN_C - 1 if d == 0 else 0)
            pe_out[d, cc * 8:(cc + 1) * 8, :] = jnp.broadcast_to(p_in[d][last:last + 1], (8, HP))

    for hh in range(C_HEADS):
        sl = slice(hh * LANE, (hh + 1) * LANE)
        kh = kk[:, sl]
        kh = kh * lax.rsqrt(jnp.maximum(jnp.sum(kh * kh, axis=-1, keepdims=True), 1e-12))
        rh = r[:, sl] * rk[:, sl]
        bon = jnp.zeros((tm, 1), F32)
        for d in range(2):
            ab_out[d, :, sl] = (-kh * p_prev[d][:, sl]).astype(BF16)
            bh_out[d, :, sl] = (kh * a[d][:, sl] * p_inv[d][:, sl]).astype(BF16)
            bon = bon + jnp.sum(rh * kd[d][:, sl], axis=-1, keepdims=True)
        bonus_out[:, sl] = bon * v[:, sl]


def _mm(a, b, dims=((1,), (0,))):
    return lax.dot_general(a.astype(BF16), b.astype(BF16), (dims, ((), ())),
                           preferred_element_type=F32)


_NT = ((1,), (1,))
_TN = ((0,), (0,))


def _scan_kernel(vf_ref, vb_ref, abf_ref, abb_ref, bhf_ref, bhb_ref, khf_ref, khb_ref,
                 rbf_ref, rbb_ref, pef_ref, peb_ref, of_ref, ob_ref, s_ref):
    @pl.when(pl.program_id(1) == 0)
    def _():
        s_ref[...] = jnp.zeros_like(s_ref)

    c = SCAN_C
    row = lax.broadcasted_iota(jnp.int32, (c, c), 0)
    col = lax.broadcasted_iota(jnp.int32, (c, c), 1)
    refs = ((vf_ref, abf_ref, bhf_ref, khf_ref, rbf_ref, pef_ref, of_ref),
            (vb_ref, abb_ref, bhb_ref, khb_ref, rbb_ref, peb_ref, ob_ref))

    chains = [(d, hh) for d in range(2) for hh in range(C_HEADS)]

    def sl(hh):
        return slice(hh * LANE, (hh + 1) * LANE)

    def rows(key):
        sub, (d, _) = key
        k = sub if d == 0 else SCAN_STEP - 1 - sub
        return k, slice(k * c, (k + 1) * c)

    al = lambda key: refs[key[1][0]][1][0, rows(key)[1], sl(key[1][1])]
    bh = lambda key: refs[key[1][0]][2][0, rows(key)[1], sl(key[1][1])]
    kh = lambda key: refs[key[1][0]][3][0, rows(key)[1], sl(key[1][1])]
    rb = lambda key: refs[key[1][0]][4][0, rows(key)[1], sl(key[1][1])]
    vv = lambda key: refs[key[1][0]][0][rows(key)[1], sl(key[1][1])]
    pe = lambda key: refs[key[1][0]][5][0, 8 * rows(key)[0]:8 * rows(key)[0] + 1, sl(key[1][1])]
    strict = lambda key: (col < row) if key[1][0] == 0 else (col > row)
    incl = lambda key: (col <= row) if key[1][0] == 0 else (col >= row)

    keys = [(sub, ch) for sub in range(SCAN_STEP) for ch in chains]
    xb = {key: _mm(jnp.concatenate([al(key), rb(key)], axis=0), bh(key), _NT) for key in keys}
    xk = {key: _mm(jnp.concatenate([al(key), rb(key)], axis=0), kh(key), _NT) for key in keys}
    l_pows = {key: [jnp.where(strict(key), xb[key][:c], 0.0)] for key in keys}
    lakv = {key: _mm(jnp.where(strict(key), xk[key][:c], 0.0), vv(key)) for key in keys}
    for _ in range(5):
        for key in keys:
            l_pows[key].append(_mm(l_pows[key][-1], l_pows[key][-1]))

    for sub in range(SCAN_STEP):
        now = [(sub, ch) for ch in chains]
        u = {key: lakv[key] + _mm(al(key), s_ref[key[1]], _NT) for key in now}
        for it in range(6):
            u = {key: u[key] + _mm(l_pows[key][it], u[key]) for key in now}
        for key in now:
            o_ref = refs[key[1][0]][6]
            o_ref[rows(key)[1], sl(key[1][1])] = (
                _mm(rb(key), s_ref[key[1]], _NT)
                + _mm(jnp.where(incl(key), xb[key][c:], 0.0), u[key])
                + _mm(jnp.where(incl(key), xk[key][c:], 0.0), vv(key)))
        for key in now:
            upd = _mm(jnp.concatenate([u[key].astype(BF16), vv(key)], axis=0),
                      jnp.concatenate([bh(key), kh(key)], axis=0), _TN)
            s_ref[key[1]] = (s_ref[key[1]] + upd) * pe(key)


def _attn_ctx_kernel(q_ref, k_ref, vt_ref, o_ref):
    s = _dot_nt(k_ref[...], q_ref[...])
    p = jnp.exp2(s - jnp.max(s, axis=0, keepdims=True))
    l = jnp.sum(p, axis=0, keepdims=True)
    o_ref[...] = (_dot(vt_ref[...], p.astype(BF16)) / l).astype(BF16)


def _attn_kernel(q_ref, kc_ref, vtc_ref, kl_ref, vtl_ref, o_ref, *, k_chunk):
    q = q_ref[...]
    tq = q.shape[0]
    parts = [(kc_ref, vtc_ref, 0, kc_ref.shape[0])]
    parts += [(kl_ref, vtl_ref, c0, k_chunk) for c0 in range(0, kl_ref.shape[0], k_chunk)]
    scores = lambda part: _dot_nt(part[0][part[2]:part[2] + part[3], :], q)
    m = jnp.full((1, tq), -jnp.inf, F32)
    l = jnp.zeros((1, tq), F32)
    acc = jnp.zeros((o_ref.shape[0], tq), F32)
    s_next = scores(parts[0])
    for n, (_, vt_ref, c0, size) in enumerate(parts):
        s = s_next
        if n + 1 < len(parts):
            s_next = scores(parts[n + 1])
        m_new = jnp.maximum(m, jnp.max(s, axis=0, keepdims=True))
        p = jnp.exp2(s - m_new)
        corr = jnp.exp2(m - m_new)
        l = l * corr + jnp.sum(p, axis=0, keepdims=True)
        acc = acc * corr + _dot(vt_ref[:, c0:c0 + size], p.astype(BF16))
        m = m_new
    o_ref[...] = (acc / l).astype(BF16)


def _merge_kernel(of_ref, ob_ref, bonus_ref, g_ref, ya_ref, ybl_ref, ybc_ref, gate_ref, x_ref,
                  mod_ref, gng_ref, gnb_ref, wa_ref, wb_ref, wc_ref, wo_ref, n2_ref,
                  rw_ref, rb_ref, xn_out, h2_out, comb_out, *, n_lat_tiles):
    mod = mod_ref[0]
    yb_t = jnp.where(pl.program_id(0) < n_lat_tiles, ybl_ref[...], ybc_ref[...])
    o = of_ref[...] + ob_ref[...]
    lane = lax.broadcasted_iota(jnp.int32, (1, LANE), 1)
    real = lane < C_HEAD
    ycs = []
    for hh in range(C_HEADS):
        sl = slice(hh * LANE, (hh + 1) * LANE)
        oh = o[:, sl]
        mean = jnp.sum(oh, axis=-1, keepdims=True) * (1.0 / C_HEAD)
        dev = jnp.where(real, oh - mean, 0.0)
        var = jnp.sum(dev * dev, axis=-1, keepdims=True) * (1.0 / C_HEAD)
        y = dev * lax.rsqrt(var + C_GN_EPS) * gng_ref[:, sl] + gnb_ref[:, sl]
        ycs.append(((y + bonus_ref[:, sl]) * g_ref[:, sl]).astype(BF16))
    yc = jnp.concatenate(ycs, axis=1)
    gate = gate_ref[...].astype(F32)
    merged = (gate[:, :D_MODEL] * _dot(ya_ref[...], wa_ref[...])
              + gate[:, D_MODEL:2 * D_MODEL] * _dot_tn(yb_t, wb_ref[...])
              + gate[:, 2 * D_MODEL:] * _dot(yc, wc_ref[...]))
    xn = x_ref[...] + mod[2:3] * _dot(merged.astype(BF16), wo_ref[...])
    xn_out[...] = xn
    h2 = _normmod(xn, n2_ref[...], mod, 3).astype(BF16)
    h2_out[...] = h2

    scores = 1.0 / (1.0 + jnp.exp(-_dot(h2, rw_ref[...])))
    neg = -jnp.inf
    sel = jnp.where(lane < N_EXPERTS, scores + rb_ref[...], neg)

    lane_f = lane.astype(F32)

    def top1(s):
        mx = jnp.max(s, axis=-1, keepdims=True)
        idx = jnp.min(jnp.where(s == mx, lane_f, float(LANE)), axis=-1, keepdims=True)
        return mx, idx.astype(jnp.int32)

    best = None
    for g in range(N_GROUPS):
        in_g = (lane >= g * EXPERTS_PER_GROUP) & (lane < (g + 1) * EXPERTS_PER_GROUP)
        s = jnp.where(in_g, sel, neg)
        m1, i1 = top1(s)
        m2, _ = top1(jnp.where(lane == i1, neg, s))
        score = m1 + m2
        if best is None:
            best, gidx = score, jnp.zeros_like(i1)
        else:
            better = score > best
            gidx = jnp.where(better, g, gidx)
            best = jnp.where(better, score, best)
    lo = gidx * EXPERTS_PER_GROUP
    s = jnp.where((lane >= lo) & (lane < lo + EXPERTS_PER_GROUP), sel, neg)
    _, i1 = top1(s)
    _, i2 = top1(jnp.where(lane == i1, neg, s))
    picked = jnp.where((lane == i1) | (lane == i2), scores, 0.0)
    comb_out[...] = picked / jnp.sum(picked, axis=-1, keepdims=True)


def _moe_kernel(cnt_ref, h2_ref, comb_ref, xn_ref, *rest, last):
    mod_refs = rest[:MOE_SUB]
    w1_ref, w3_ref, w2_ref, tri_ref, fg_ref, out_ref, rank_t_ref, comb_t_ref = rest[MOE_SUB:]
    i = pl.program_id(0)
    e = pl.program_id(1)
    tm = h2_ref.shape[0]
    lane = lax.broadcasted_iota(jnp.int32, (1, LANE), 1)

    @pl.when(e == 0)
    def _():
        comb = comb_ref[...]
        picked = jnp.where(comb > 0.0, 1.0, 0.0).astype(BF16)
        rank = _dot(tri_ref[...], picked)
        rank_t_ref[...] = rank.T
        comb_t_ref[...] = comb.T
        out_ref[...] = jnp.zeros_like(out_ref)

    comb_e = jnp.sum(jnp.where(lane == e, comb_ref[...], 0.0), axis=-1, keepdims=True)
    rank_row = rank_t_ref[pl.ds(e, 1), :]
    picked_row = comb_t_ref[pl.ds(e, 1), :] > 0.0
    slot_col = lax.broadcasted_iota(jnp.int32, (MOE_CAP, 1), 0)
    n_blocks = lax.shift_right_logical(cnt_ref[i, e] + (MOE_CAP - 1), MOE_CAP.bit_length() - 1)

    def block(blk, carry):
        base = blk * MOE_CAP
        gather = jnp.where(picked_row & (rank_row == (slot_col + base).astype(F32)), 1.0, 0.0).astype(BF16)
        xg = _dot(gather, h2_ref[...]).astype(BF16)
        a = _dot(xg, w1_ref[0, 0].astype(BF16))
        he = (a * _sigmoid(a) * _dot(xg, w3_ref[0, 0].astype(BF16))).astype(BF16)
        y = _dot(he, w2_ref[0, 0].astype(BF16)).astype(BF16)
        out_ref[...] += comb_e * _dot_tn(gather, y)
        return carry

    lax.fori_loop(0, n_blocks, block, 0)

    @pl.when(e == N_EXPERTS - 1)
    def _():
        for s, m_ref in enumerate(mod_refs):
            rows = slice(s * SEG_ROWS, (s + 1) * SEG_ROWS)
            y = xn_ref[rows] + m_ref[0][5:6] * out_ref[rows]
            if last:
                y = _rms(y) * fg_ref[...]
            out_ref[rows] = y


def _head_pad_cols(w, width):
    lead = w.shape[:-1]
    w = w.reshape(lead + (C_HEADS, width))
    w = jnp.pad(w, [(0, 0)] * len(lead) + [(0, 0), (0, LANE - width)])
    return w.reshape(lead + (HP,))


def _head_pad_rows(w, width):
    n = w.shape[-1]
    w = w.reshape(C_HEADS, width, n)
    w = jnp.pad(w, ((0, 0), (0, LANE - width), (0, 0)))
    return w.reshape(HP, n)


def _split_rope(w):
    lead = w.shape[:-1]
    w = w.reshape(lead + (2, 2, ROPE_FREQS))
    x1 = w[..., 0, :].reshape(lead + (2 * ROPE_FREQS,))
    x2 = w[..., 1, :].reshape(lead + (2 * ROPE_FREQS,))
    return x1, x2


def _rope_tables(n_ctx, t):
    rows = t // GRID_W
    row = jnp.repeat(jnp.arange(rows, dtype=F32), GRID_W)
    col = jnp.tile(jnp.arange(GRID_W, dtype=F32), rows)
    inv = jnp.power(ROPE_BASE, -jnp.arange(ROPE_FREQS, dtype=F32) / ROPE_FREQS)
    ang = jnp.concatenate([row[:, None] * inv, col[:, None] * inv], axis=1)
    cos = jnp.concatenate([jnp.cos(ang), jnp.ones((n_ctx, 2 * ROPE_FREQS), F32)], axis=0)
    sin = jnp.concatenate([jnp.sin(ang), jnp.zeros((n_ctx, 2 * ROPE_FREQS), F32)], axis=0)
    n = n_ctx + t
    one = jnp.ones((n, B_NOPE), F32)
    zero = lambda w: jnp.zeros((n, w), F32)
    qs = MLA_SCALE * math.log2(math.e)
    cq = jnp.concatenate([one, cos, cos, zero(LANE - B_NOPE - B_ROPE)], axis=1) * qs
    sq = jnp.concatenate([zero(B_NOPE), -sin, sin, zero(LANE - B_NOPE - B_ROPE)], axis=1) * qs
    ck = jnp.concatenate([cos, cos, zero(LANE - B_ROPE)], axis=1)
    sk = jnp.concatenate([-sin, sin, zero(LANE - B_ROPE)], axis=1)
    return cq, sq, ck, sk


def _layer_weights(p, l):
    w = {}
    w_in = p['w_in'][l]
    d = w_in.shape[0]
    o = 0
    w_q = w_in[:, o:o + B_Q_LORA]; o += B_Q_LORA
    w_kv = w_in[:, o:o + B_KV_LORA]; o += B_KV_LORA
    w_kr = w_in[:, o:o + B_ROPE]; o += B_ROPE
    w_c = w_in[:, o:o + C_COLS]; o += C_COLS
    w_a = w_in[:, o:o + A_COLS]; o += A_COLS
    w_g = w_in[:, o:o + G_COLS]

    k1, k2 = _split_rope(w_kr)
    zpad = jnp.zeros((d, LANE - B_ROPE), F32)
    w['win_b'] = jnp.concatenate([w_q, w_kv, k1, k2, zpad, k2, k1, zpad], axis=1).astype(BF16)

    wuq = p['b_w_uq'][l].reshape(B_Q_LORA, B_HEADS, B_NOPE + B_ROPE)
    q1, q2 = _split_rope(wuq[..., B_NOPE:])
    qz = jnp.zeros((B_Q_LORA, B_HEADS, LANE - B_NOPE - B_ROPE), F32)
    w['wq'] = jnp.concatenate([wuq[..., :B_NOPE], q1, q2, qz], axis=-1).reshape(B_Q_LORA, HP).astype(BF16)
    w['wqs'] = jnp.concatenate([wuq[..., :B_NOPE], q2, q1, qz], axis=-1).reshape(B_Q_LORA, HP).astype(BF16)
    wukv = p['b_w_ukv'][l].reshape(B_KV_LORA, B_HEADS, B_NOPE + B_VDIM)
    w['wk'] = _head_pad_cols(wukv[..., :B_NOPE].reshape(B_KV_LORA, -1), B_NOPE).astype(BF16)
    w['wv'] = wukv[..., B_NOPE:].reshape(B_KV_LORA, -1).T.astype(BF16)
    place = jnp.zeros((LANE, B_HEADS, LANE), F32)
    idx = jnp.arange(B_ROPE)
    place = place.at[idx, :, B_NOPE + idx].set(1.0)
    w['place'] = place.reshape(LANE, HP).astype(BF16)
    w['qg'] = p['b_q_norm'][l][None]
    w['kvg'] = p['b_kv_norm'][l][None]

    w['wa'] = w_a.astype(BF16)
    w['wg'] = w_g.astype(BF16)
    w['vg'] = p['a_v_gain'][l][None]
    w['ws'] = p['a_ws'][l].astype(BF16)
    w['bsb'] = jnp.broadcast_to(p['a_bs'][l].T[:, :, None], (A_CHUNK, A_GROUPS, LANE)).reshape(A_CHUNK, A_WIDTH)

    cw = C_WIDTH
    pieces = [_head_pad_cols(w_c[:, i * cw:(i + 1) * cw], C_HEAD) for i in range(3)]
    w['wc'] = jnp.concatenate(pieces + [w_c[:, 3 * cw:]], axis=1).astype(BF16)
    mu = p['c_mu'][l]
    w['mu'] = jnp.concatenate([_head_pad_cols(mu[i * cw:(i + 1) * cw], C_HEAD) for i in range(3)]
                              + [mu[3 * cw:]])[None]

    def lora_block(m, rank):
        m = _head_pad_cols(m, C_HEAD)
        z = jnp.zeros_like(m[0])
        return jnp.stack([jnp.concatenate([m[0], z], axis=0), jnp.concatenate([z, m[1]], axis=0)]).astype(BF16)

    w['w2'] = lora_block(p['c_w2'][l], C_DECAY_LORA)
    w['a2'] = lora_block(p['c_a2'][l], C_AAA_LORA)
    w['w0'] = _head_pad_cols(p['c_w0'][l], C_HEAD)
    w['a0'] = _head_pad_cols(p['c_a0'][l], C_HEAD)
    w['kk'] = _head_pad_cols(p['c_k_k'][l], C_HEAD)[None]
    w['ka'] = _head_pad_cols(p['c_k_a'][l], C_HEAD)[None]
    w['rk'] = _head_pad_cols(p['c_r_k'][l].reshape(-1), C_HEAD)[None]
    w['g2'] = _head_pad_cols(p['c_g2'][l], C_HEAD).astype(BF16)
    w['gng'] = _head_pad_cols(p['c_gn_g'][l], C_HEAD)[None]
    w['gnb'] = _head_pad_cols(p['c_gn_b'][l], C_HEAD)[None]

    w['wup_a'] = p['w_up_a'][l].astype(BF16)
    w['wup_b'] = p['w_up_b'][l].astype(BF16)
    w['wup_c'] = _head_pad_rows(p['w_up_c'][l], C_HEAD).astype(BF16)
    w['wo'] = p['w_o'][l].astype(BF16)
    w['g1'] = p['norm1_g'][l][None]
    w['n2'] = p['norm2_g'][l][None]
    return w


def _layer(xs, mods, w, moe_w, tabs, rw, rb, fg, *, layer, bn, t, n_ctx, last):
    m, d = xs.shape
    n_tiles = m // TM
    n_lat = bn * t
    n_lat_tiles = n_lat // TM
    tpl = t // TM
    row = lambda i: (i, 0)
    seg_of = lambda blk: jnp.where(blk < n_lat // SEG_ROWS, 1 + blk // (t // SEG_ROWS), 0)
    seg = lambda i: (seg_of(i * (TM // SEG_ROWS)), 0, 0)
    x_spec = pl.BlockSpec((TM, d), row)
    mod_spec = pl.BlockSpec((1, 6, d), seg)
    hp_spec = pl.BlockSpec((TM, HP), row)
    hp2_spec = pl.BlockSpec((2, TM, HP), lambda i: (0, i, 0))
    hp_f32 = jax.ShapeDtypeStruct((m, HP), F32)
    hp_bf16 = jax.ShapeDtypeStruct((m, HP), BF16)
    hp2_bf16 = jax.ShapeDtypeStruct((2, m, HP), BF16)
    cq, sq, ck, sk = tabs
    tab_spec = pl.BlockSpec(
        (TM, LANE), lambda i: (jnp.where(i < n_lat_tiles, i % tpl, tpl + i - n_lat_tiles), 0))

    yb_w = B_HEADS * B_VDIM
    q, k, vt = pl.pallas_call(
        _mla_kernel,
        grid=(n_tiles,),
        in_specs=[x_spec, _full((1, d)), mod_spec, _full(w['win_b'].shape), _full((1, B_Q_LORA)),
                  _full((1, B_KV_LORA)), _full(w['wq'].shape), _full(w['wqs'].shape),
                  _full(w['wk'].shape), _full(w['wv'].shape), _full(w['place'].shape),
                  tab_spec, tab_spec, tab_spec, tab_spec],
        out_specs=[hp_spec, hp_spec, pl.BlockSpec((yb_w, TM), lambda i: (0, i))],
        out_shape=[hp_bf16, hp_bf16, jax.ShapeDtypeStruct((yb_w, m), BF16)],
        compiler_params=_params(1),
        name="mla_proj",
    )(xs, w['g1'], mods, w['win_b'], w['qg'], w['kvg'], w['wq'], w['wqs'], w['wk'], w['wv'],
      w['place'], cq, sq, ck, sk)

    ya, gate = pl.pallas_call(
        _gmlp_kernel,
        grid=(n_tiles,),
        in_specs=[x_spec, _full((1, d)), mod_spec, _full(w['wa'].shape), _full(w['wg'].shape),
                  _full((1, A_WIDTH)), _full(w['ws'].shape), _full(w['bsb'].shape)],
        out_specs=[pl.BlockSpec((TM, A_WIDTH), row), pl.BlockSpec((TM, G_COLS), row)],
        out_shape=[jax.ShapeDtypeStruct((m, A_WIDTH), BF16), jax.ShapeDtypeStruct((m, G_COLS), BF16)],
        compiler_params=_params(1),
        name="gmlp_gates",
    )(xs, w['g1'], mods, w['wa'], w['wg'], w['vg'], w['ws'], w['bsb'])

    tf = TM_FEAT
    sub = tf // 8
    seq_starts = tuple(b * t for b in range(bn)) + tuple(n_lat + b * n_ctx for b in range(bn))
    seq_ends = tuple(s - 1 for s in seq_starts[1:]) + (m - 1,)
    assert all(s % tf == 0 for s in seq_starts)
    f_hp = pl.BlockSpec((tf, HP), row)
    f_hp2 = pl.BlockSpec((2, tf, HP), lambda i: (0, i, 0))
    pe_spec = pl.BlockSpec((2, 8 * (tf // SCAN_C), HP), lambda i: (0, i, 0))
    pe_shape = jax.ShapeDtypeStruct((2, 8 * (m // SCAN_C), HP), F32)
    vv, ab, bh, kh, rbar, pe, bonus, gg = pl.pallas_call(
        functools.partial(_feat_kernel, seq_starts=seq_starts, seq_ends=seq_ends),
        grid=(m // tf,),
        in_specs=[pl.BlockSpec((tf, d), row),
                  pl.BlockSpec((8, d), lambda i: (jnp.maximum(i * sub - 1, 0), 0)),
                  pl.BlockSpec((8, d), lambda i: (jnp.minimum((i + 1) * sub, m // 8 - 1), 0)),
                  _full((1, d)), pl.BlockSpec((1, 6, d), lambda i: (seg_of(i * (tf // SEG_ROWS)), 0, 0)),
                  _full(w['wc'].shape), _full(w['mu'].shape),
                  _full((2, HP)), _full(w['w2'].shape), _full((2, HP)), _full(w['a2'].shape),
                  _full((1, HP)), _full((1, HP)), _full((1, HP)), _full(w['g2'].shape)],
        out_specs=[f_hp, f_hp2, f_hp2, f_hp2, f_hp2, pe_spec, f_hp, f_hp],
        out_shape=[hp_bf16, hp2_bf16, hp2_bf16, hp2_bf16, hp2_bf16, pe_shape, hp_f32, hp_f32],
        compiler_params=_params(1),
        name="rwkv_features",
    )(xs, xs, xs, w['g1'], mods, w['wc'], w['mu'], w['w0'], w['w2'], w['a0'], w['a2'],
      w['kk'], w['ka'], w['rk'], w['g2'])

    blk_rows = SCAN_C * SCAN_STEP
    nctx_c = n_ctx // blk_rows
    nlat_c = t // blk_rows
    nc = nctx_c + nlat_c

    def chunk_of(dd, b, j):
        ctx_base = n_lat // blk_rows + b * nctx_c
        lat_base = b * nlat_c
        if dd == 0:
            return jnp.where(j < nctx_c, ctx_base + j, lat_base + j - nctx_c)
        return jnp.where(j < nctx_c, ctx_base + nctx_c - 1 - j, lat_base + nc - 1 - j)

    def tok_spec(dd):
        return pl.BlockSpec((blk_rows, HP), lambda b, j: (chunk_of(dd, b, j), 0))

    def dir_spec(dd, rows=blk_rows):
        return pl.BlockSpec((1, rows, HP), lambda b, j: (dd, chunk_of(dd, b, j), 0))

    o_f, o_b = pl.pallas_call(
        _scan_kernel,
        grid=(bn, nc),
        in_specs=[tok_spec(0), tok_spec(1)] + [dir_spec(dd) for _ in range(4) for dd in range(2)]
                 + [dir_spec(0, 8 * SCAN_STEP), dir_spec(1, 8 * SCAN_STEP)],
        out_specs=[tok_spec(0), tok_spec(1)],
        out_shape=[hp_f32, hp_f32],
        scratch_shapes=[pltpu.VMEM((2, C_HEADS, LANE, LANE), F32)],
        compiler_params=_params(2),
        name="rwkv_scan",
    )(vv, vv, ab, ab, bh, bh, kh, kh, rbar, rbar, pe, pe)

    ctx_blk = n_lat // n_ctx
    tq = ATTN_TQ
    ctx_k =pl.BlockSpec((n_ctx, LANE), lambda b, hh, qi: (ctx_blk + b, hh))
    ctx_vt = pl.BlockSpec((B_VDIM, n_ctx), lambda b, hh, qi: (hh, ctx_blk + b))
    lat_k = pl.BlockSpec((t, LANE), lambda b, hh, qi: (b, hh))
    lat_vt = pl.BlockSpec((B_VDIM, t), lambda b, hh, qi: (hh, b))
    q_lat = pl.BlockSpec((tq, LANE), lambda b, hh, qi: (b * (t // tq) + qi, hh))
    yb_lat_t = pl.pallas_call(
        functools.partial(_attn_kernel, k_chunk=ATTN_KC),
        grid=(bn, B_HEADS, t // tq),
        in_specs=[q_lat, ctx_k, ctx_vt, lat_k, lat_vt],
        out_specs=pl.BlockSpec((B_VDIM, tq), lambda b, hh, qi: (hh, b * (t // tq) + qi)),
        out_shape=jax.ShapeDtypeStruct((B_HEADS * B_VDIM, n_lat), BF16),
        compiler_params=_params(3),
        name="mla_attention",
    )(q, k, vt, k, vt)
    ctx_q = pl.BlockSpec((n_ctx, LANE), lambda b, hh: (ctx_blk + b, hh))
    yb_ctx_t = pl.pallas_call(
        _attn_ctx_kernel,
        grid=(bn, B_HEADS),
        in_specs=[ctx_q, ctx_q, pl.BlockSpec((B_VDIM, n_ctx), lambda b, hh: (hh, ctx_blk + b))],
        out_specs=pl.BlockSpec((B_VDIM, n_ctx), lambda b, hh: (hh, b)),
        out_shape=jax.ShapeDtypeStruct((yb_w, bn * n_ctx), BF16),
        compiler_params=_params(2),
        name="mla_attention_ctx",
    )(q, k, vt)

    n_ctx_tiles = n_tiles - n_lat_tiles
    xn, h2, comb = pl.pallas_call(
        functools.partial(_merge_kernel, n_lat_tiles=n_lat_tiles),
        grid=(n_tiles,),
        in_specs=[hp_spec, hp_spec, hp_spec, hp_spec, pl.BlockSpec((TM, A_WIDTH), row),
                  pl.BlockSpec((yb_w, TM), lambda i: (0, jnp.minimum(i, n_lat_tiles - 1))),
                  pl.BlockSpec((yb_w, TM), lambda i: (0, jnp.clip(i - n_lat_tiles, 0, n_ctx_tiles - 1))),
                  pl.BlockSpec((TM, G_COLS), row), x_spec, mod_spec, _full((1, HP)), _full((1, HP)),
                  _full(w['wup_a'].shape), _full(w['wup_b'].shape), _full(w['wup_c'].shape),
                  _full(w['wo'].shape), _full((1, d)), _full(rw.shape), _full(rb.shape)],
        out_specs=[x_spec, x_spec, pl.BlockSpec((TM, LANE), row)],
        out_shape=[jax.ShapeDtypeStruct((m, d), F32), jax.ShapeDtypeStruct((m, d), BF16),
                   jax.ShapeDtypeStruct((m, LANE), F32)],
        compiler_params=_params(1),
        name="merge_router",
    )(o_f, o_b, bonus, gg, ya, yb_lat_t, yb_ctx_t, gate, xs, mods, w['gng'], w['gnb'], w['wup_a'], w['wup_b'],
      w['wup_c'], w['wo'], w['n2'], rw, rb)

    tm_moe = MOE_SUB * SEG_ROWS
    counts = jnp.sum((comb > 0.0).reshape(m // tm_moe, tm_moe, LANE), axis=1, dtype=jnp.int32)
    tri = jnp.tril(jnp.ones((tm_moe, tm_moe), BF16), -1)
    mrow = lambda i, e, cnt: (i, 0)
    once = pl.Buffered(1)
    mseg = lambda s: pl.BlockSpec((1, 6, d), lambda i, e, cnt: (seg_of(i * MOE_SUB + s), 0, 0))
    wexp = lambda shape: pl.BlockSpec((1, 1) + shape, lambda i, e, cnt: (layer, e, 0, 0))
    grid_spec = pltpu.PrefetchScalarGridSpec(
        num_scalar_prefetch=1,
        grid=(m // tm_moe, N_EXPERTS),
        in_specs=[pl.BlockSpec((tm_moe, d), mrow), pl.BlockSpec((tm_moe, LANE), mrow),
                  pl.BlockSpec((tm_moe, d), mrow, pipeline_mode=once)]
                 + [mseg(s) for s in range(MOE_SUB)]
                 + [wexp((d, D_EXPERT)), wexp((d, D_EXPERT)), wexp((D_EXPERT, d)),
                    pl.BlockSpec((tm_moe, tm_moe), lambda i, e, cnt: (0, 0), pipeline_mode=once),
                    pl.BlockSpec((1, d), lambda i, e, cnt: (0, 0))],
        out_specs=pl.BlockSpec((tm_moe, d), mrow),
        scratch_shapes=[pltpu.VMEM((LANE, tm_moe), F32),
                        pltpu.VMEM((LANE, tm_moe), F32)],
    )
    return pl.pallas_call(
        functools.partial(_moe_kernel, last=last),
        grid_spec=grid_spec,
        out_shape=jax.ShapeDtypeStruct((m, d), F32),
        compiler_params=_params(2),
        name="moe",
    )(counts, h2, comb, xn, *([mods] * MOE_SUB), moe_w[0], moe_w[1], moe_w[2], tri, fg)


def kernel(x, c, ctx, c_ctx, mod_w, mod_b, norm1_g, norm2_g, w_in, a_v_gain, a_ws, a_bs, b_q_norm, b_w_uq, b_kv_norm, b_w_ukv, c_mu, c_w0, c_w2, c_a0, c_a2, c_g2, c_k_k, c_k_a, c_r_k, c_gn_g, c_gn_b, w_up_a, w_up_b, w_up_c, w_o, router_w, router_b, moe_w1, moe_w3, moe_w2, final_g):
    p = dict(w_in=w_in, a_v_gain=a_v_gain, a_ws=a_ws, a_bs=a_bs, b_q_norm=b_q_norm, b_w_uq=b_w_uq,
             b_kv_norm=b_kv_norm, b_w_ukv=b_w_ukv, c_mu=c_mu, c_w0=c_w0, c_w2=c_w2, c_a0=c_a0,
             c_a2=c_a2, c_g2=c_g2, c_k_k=c_k_k, c_k_a=c_k_a, c_r_k=c_r_k, c_gn_g=c_gn_g,
             c_gn_b=c_gn_b, w_up_a=w_up_a, w_up_b=w_up_b, w_up_c=w_up_c, w_o=w_o,
             norm1_g=norm1_g, norm2_g=norm2_g)
    bn, t, d = x.shape
    n_ctx = ctx.shape[1]
    depth = mod_w.shape[0]
    m = bn * (t + n_ctx)
    assert n_ctx == SEG_ROWS and (bn * n_ctx) % TM == 0 and t % TM == 0 and t % ATTN_TQ == 0
    assert t % ATTN_KC == 0 and m % (MOE_SUB * SEG_ROWS) == 0 and 1 + bn <= 8

    cvec = jnp.concatenate([c_ctx[None], c, jnp.zeros((8 - 1 - bn, d), F32)], axis=0)
    mods = _modulation(cvec, mod_w, mod_b).reshape(depth, 8, 6, d)

    tabs = _rope_tables(bn * n_ctx, t)
    rw = jnp.pad(router_w, ((0, 0), (0, LANE - N_EXPERTS))).astype(BF16)
    rb = jnp.pad(router_b, (0, LANE - N_EXPERTS))[None]
    fg = final_g[None]

    xs = jnp.concatenate([x.reshape(bn * t, d), ctx.reshape(bn * n_ctx, d)], axis=0)
    for l in range(depth):
        xs = _layer(xs, mods[l], _layer_weights(p, l), (moe_w1, moe_w3, moe_w2), tabs, rw, rb, fg,
                    layer=l, bn=bn, t=t, n_ctx=n_ctx, last=(l == depth - 1))
    return xs[:bn * t].reshape(bn, t, d)
```

```python
import functools
import math

import jax
import jax.numpy as jnp
from jax import lax
from jax.experimental import pallas as pl
from jax.experimental.pallas import tpu as pltpu

F32 = jnp.float32
BF16 = jnp.bfloat16
HIGHEST = lax.Precision.HIGHEST

D_MODEL = 1024
GRID_W = 64
EPS = 1e-6

A_WIDTH = 512
A_GROUPS = 4
A_CHUNK = 128

B_HEADS = 8
B_Q_LORA = 384
B_KV_LORA = 256
B_NOPE = 64
B_ROPE = 32
B_VDIM = 64
ROPE_BASE = 10000.0
ROPE_FREQS = B_ROPE // 4
MLA_SCALE = (B_NOPE + B_ROPE) ** -0.5

C_HEADS = 8
C_HEAD = 64
C_WIDTH = C_HEADS * C_HEAD
C_DECAY_LORA = 64
C_AAA_LORA = 64
C_GATE_LORA = 128
C_GN_EPS = 64e-5
DECAY_SCALE = 0.6065306597126334

B_COLS = B_Q_LORA + B_KV_LORA + B_ROPE
C_COLS = 3 * C_WIDTH + 2 * C_DECAY_LORA + 2 * C_AAA_LORA + C_GATE_LORA
A_COLS = 2 * A_WIDTH
G_COLS = 3 * D_MODEL

N_EXPERTS = 16
N_GROUPS = 4
EXPERTS_PER_GROUP = 4
D_EXPERT = 512

LANE = 128
HP = C_HEADS * LANE
TM = 512
TM_FEAT = 256
SEG_ROWS = 256
SCAN_C = 64
SCAN_STEP = 4
CUM_ROWS = 256
MOE_SUB = 6
MOE_CAP = 256
ATTN_TQ = 512
ATTN_KC = 1024
VMEM_LIMIT = 52 * 1024 * 1024


def _dot(a, b, precision=None):
    return jnp.dot(a, b, preferred_element_type=F32, precision=precision)


def _dot_nt(a, b, precision=None):
    return lax.dot_general(a, b, (((1,), (1,)), ((), ())),
                           preferred_element_type=F32, precision=precision)


def _dot_tn(a, b, precision=None):
    return lax.dot_general(a, b, (((0,), (0,)), ((), ())),
                           preferred_element_type=F32, precision=precision)


def _rms(x, eps=EPS):
    return x * lax.rsqrt(jnp.mean(x * x, axis=-1, keepdims=True) + eps)


def _normmod(x, g, mod, first):
    return _rms(x) * g * (1.0 + mod[first + 1:first + 2]) + mod[first:first + 1]


def _sigmoid(x):
    return 0.5 * jnp.tanh(0.5 * x) + 0.5


def _params(n_grid):
    return pltpu.CompilerParams(
        dimension_semantics=("arbitrary",) * n_grid, vmem_limit_bytes=VMEM_LIMIT)


def _full(shape):
    n = len(shape)
    return pl.BlockSpec(shape, lambda *_: (0,) * n)


def _mod_kernel(c_ref, w_ref, b_ref, o_ref):
    c = c_ref[...]
    act = (c * _sigmoid(c)).astype(BF16)
    o_ref[0] = _dot(act, w_ref[0].astype(BF16)) + b_ref[0]


def _modulation(cvec, mod_w, mod_b):
    depth, d, n = mod_w.shape
    tn = 1536
    return pl.pallas_call(
        _mod_kernel,
        grid=(depth, n // tn),
        in_specs=[
            pl.BlockSpec((8, d), lambda l, j: (0, 0)),
            pl.BlockSpec((1, d, tn), lambda l, j: (l, 0, j)),
            pl.BlockSpec((1, 1, tn), lambda l, j: (l, 0, j)),
        ],
        out_specs=pl.BlockSpec((1, 8, tn), lambda l, j: (l, 0, j)),
        out_shape=jax.ShapeDtypeStruct((depth, 8, n), F32),
        compiler_params=_params(2),
        name="modulation",
    )(cvec, mod_w, mod_b.reshape(depth, 1, n))


def _mla_kernel(x_ref, g1_ref, mod_ref, win_ref, qg_ref, kvg_ref, wq_ref, wqs_ref,
                wk_ref, wv_ref, place_ref, cq_ref, sq_ref, ck_ref, sk_ref,
                q_out, k_out, vt_out):
    h = _normmod(x_ref[...], g1_ref[...], mod_ref[0], 0).astype(BF16)
    z = _dot(h, win_ref[...])
    zq = z[:, :B_Q_LORA]
    zkv = z[:, B_Q_LORA:B_Q_LORA + B_KV_LORA]
    kr_a = z[:, B_Q_LORA + B_KV_LORA:B_Q_LORA + B_KV_LORA + LANE]
    kr_b = z[:, B_Q_LORA + B_KV_LORA + LANE:]
    qn = (_rms(zq) * qg_ref[...]).astype(BF16)
    kvn = (_rms(zkv) * kvg_ref[...]).astype(BF16)
    q1 = _dot(qn, wq_ref[...])
    q2 = _dot(qn, wqs_ref[...])
    kr = (kr_a * ck_ref[...] + kr_b * sk_ref[...]).astype(BF16)
    k = _dot(kvn, wk_ref[...]) + _dot(kr, place_ref[...])
    cq = cq_ref[...]
    sq = sq_ref[...]
    for hh in range(B_HEADS):
        sl = slice(hh * LANE, (hh + 1) * LANE)
        q_out[:, sl] = (q1[:, sl] * cq + q2[:, sl] * sq).astype(BF16)
    k_out[...] = k.astype(BF16)
    vt_out[...] = _dot_nt(wv_ref[...], kvn).astype(BF16)


def _gelu_tanh(x):
    return 0.5 * x * (1.0 + jnp.tanh(math.sqrt(2.0 / math.pi) * (x + 0.044715 * (x * x * x))))


def _gmlp_kernel(x_ref, g1_ref, mod_ref, wa_ref, wg_ref, vg_ref, ws_ref, bsb_ref,
                 ya_out, gate_out):
    h = _normmod(x_ref[...], g1_ref[...], mod_ref[0], 0).astype(BF16)
    gate_out[...] = _sigmoid(_dot(h, wg_ref[...])).astype(BF16)
    zg = _gelu_tanh(_dot(h, wa_ref[...]))
    u = zg[:, :A_WIDTH]
    v = (_rms(zg[:, A_WIDTH:]) * vg_ref[...]).astype(BF16)
    for c in range(TM // A_CHUNK):
        rows = slice(c * A_CHUNK, (c + 1) * A_CHUNK)
        for g in range(A_GROUPS):
            cols = slice(g * LANE, (g + 1) * LANE)
            mixed = _dot(ws_ref[g], v[rows, cols]) + bsb_ref[:, cols]
            ya_out[rows, cols] = (u[rows, cols] * mixed).astype(BF16)


def _feat_kernel(x_ref, xp_ref, xn_ref, g1_ref, mod_ref, wc_ref, mu_ref,
                 w0_ref, w2_ref, a0_ref, a2_ref, kk_ref, ka_ref, rk_ref, g2_ref,
                 v_out, ab_out, bh_out, kh_out, rb_out, pe_out, bonus_out, g_out,
                 *, seq_starts, seq_ends):
    mod = mod_ref[0]
    g1 = g1_ref[...]
    h = _normmod(x_ref[...], g1, mod, 0).astype(BF16)
    hp = _normmod(xp_ref[...], g1, mod, 0).astype(BF16)
    hn = _normmod(xn_ref[...], g1, mod, 0).astype(BF16)
    tm = x_ref.shape[0]
    row = lax.broadcasted_iota(jnp.int32, (tm, 1), 0)
    first = pl.program_id(0) * tm
    has_prev = 1.0 - functools.reduce(jnp.maximum, [jnp.where(first == s, 1.0, 0.0) for s in seq_starts])
    has_next = 1.0 - functools.reduce(jnp.maximum,
                                      [jnp.where(first + tm - 1 == e, 1.0, 0.0) for e in seq_ends])

    def shifted_proj(cols):
        w = wc_ref[:, cols]
        z = _dot(h, w)
        z_up = pltpu.roll(z, 1, 0)
        z_dn = pltpu.roll(z, tm - 1, 0)
        z_up = jnp.concatenate(
            [jnp.where(row[:8] == 0, _dot(hp, w)[7:8] * has_prev, z_up[:8]), z_up[8:]], axis=0)
        z_dn = jnp.concatenate(
            [z_dn[:tm - 8], jnp.where(row[tm - 8:] == tm - 1, _dot(hn, w)[0:1] * has_next, z_dn[tm - 8:])],
            axis=0)
        return z + mu_ref[:, cols] * (0.5 * (z_up + z_dn) - z)

    zl = shifted_proj(slice(3 * HP, 3 * HP + 3 * LANE))
    zw = jnp.tanh(zl[:, :LANE]).astype(BF16)
    za = zl[:, LANE:2 * LANE].astype(BF16)
    zg = _sigmoid(zl[:, 2 * LANE:]).astype(BF16)
    w_logit = [w0_ref[d:d + 1] + _dot(zw, w2_ref[d]) for d in range(2)]
    a_logit = [a0_ref[d:d + 1] + _dot(za, a2_ref[d]) for d in range(2)]
    g_out[...] = _dot(zg, g2_ref[...])
    r = shifted_proj(slice(0, HP))

    ti = lax.broadcasted_iota(jnp.int32, (CUM_ROWS, CUM_ROWS), 0)
    tj = lax.broadcasted_iota(jnp.int32, (CUM_ROWS, CUM_ROWS), 1)
    shift = SCAN_C.bit_length() - 1
    same = lax.shift_right_logical(ti, shift) == lax.shift_right_logical(tj, shift)
    lws, css = [], []
    for d in range(2):
        lw = -DECAY_SCALE * _sigmoid(w_logit[d])
        tri = jnp.where(same & ((tj <= ti) if d == 0 else (tj >= ti)), 1.0, 0.0).astype(BF16)
        hi = lw.astype(BF16)
        lo = (lw - hi.astype(F32)).astype(BF16)
        css.append(jnp.concatenate(
            [_dot(tri, hi[rows]) + _dot(tri, lo[rows])
             for rows in (slice(c0, c0 + CUM_ROWS) for c0 in range(0, tm, CUM_ROWS))], axis=0))
        lws.append(lw)
    k = shifted_proj(slice(HP, 2 * HP))
    v = shifted_proj(slice(2 * HP, 3 * HP))
    v_out[...] = v.astype(BF16)

    kk = k * kk_ref[...]
    ka = ka_ref[...]
    rk = rk_ref[...]
    a = [_sigmoid(a_logit[d]) for d in range(2)]
    kd = [k * (1.0 + (a[d] - 1.0) * ka) for d in range(2)]
    p_in, p_prev, p_inv = [], [], []
    for d in range(2):
        p_in.append(jnp.exp(css[d]))
        p_prev.append(jnp.exp(css[d] - lws[d]))
        p_inv.append(jnp.exp(-css[d]))
        kh_out[d] = (kd[d] * p_inv[d]).astype(BF16)
        rb_out[d] = (r * p_in[d]).astype(BF16)
        for cc in range(tm // SCAN_C):
            last = cc * SCAN_C + (SCAN_C - 1 if d == 0 else 0)
            pe_out[d, cc * 8:(cc + 1) * 8, :] = jnp.broadcast_to(p_in[d][last:last + 1], (8, HP))

    for hh in range(C_HEADS):
        sl = slice(hh * LANE, (hh + 1) * LANE)
        kh = kk[:, sl]
        kh = kh * lax.rsqrt(jnp.maximum(jnp.sum(kh * kh, axis=-1, keepdims=True), 1e-12))
        rh = r[:, sl] * rk[:, sl]
        bon = jnp.zeros((tm, 1), F32)
        for d in range(2):
            ab_out[d, :, sl] = (-kh * p_prev[d][:, sl]).astype(BF16)
            bh_out[d, :, sl] = (kh * a[d][:, sl] * p_inv[d][:, sl]).astype(BF16)
            bon = bon + jnp.sum(rh * kd[d][:, sl], axis=-1, keepdims=True)
        bonus_out[:, sl] = bon * v[:, sl]


def _mm(a, b, dims=((1,), (0,))):
    return lax.dot_general(a.astype(BF16), b.astype(BF16), (dims, ((), ())),
                           preferred_element_type=F32)


_NT = ((1,), (1,))
_TN = ((0,), (0,))


def _scan_kernel(vf_ref, vb_ref, abf_ref, abb_ref, bhf_ref, bhb_ref, khf_ref, khb_ref,
                 rbf_ref, rbb_ref, pef_ref, peb_ref, of_ref, ob_ref, s_ref):
    @pl.when(pl.program_id(1) == 0)
    def _():
        s_ref[...] = jnp.zeros_like(s_ref)

    c = SCAN_C
    row = lax.broadcasted_iota(jnp.int32, (c, c), 0)
    col = lax.broadcasted_iota(jnp.int32, (c, c), 1)
    refs = ((vf_ref, abf_ref, bhf_ref, khf_ref, rbf_ref, pef_ref, of_ref),
            (vb_ref, abb_ref, bhb_ref, khb_ref, rbb_ref, peb_ref, ob_ref))

    chains = [(d, hh) for d in range(2) for hh in range(C_HEADS)]

    def sl(hh):
        return slice(hh * LANE, (hh + 1) * LANE)

    def rows(key):
        sub, (d, _) = key
        k = sub if d == 0 else SCAN_STEP - 1 - sub
        return k, slice(k * c, (k + 1) * c)

    al = lambda key: refs[key[1][0]][1][0, rows(key)[1], sl(key[1][1])]
    bh = lambda key: refs[key[1][0]][2][0, rows(key)[1], sl(key[1][1])]
    kh = lambda key: refs[key[1][0]][3][0, rows(key)[1], sl(key[1][1])]
    rb = lambda key: refs[key[1][0]][4][0, rows(key)[1], sl(key[1][1])]
    vv = lambda key: refs[key[1][0]][0][rows(key)[1], sl(key[1][1])]
    pe = lambda key: refs[key[1][0]][5][0, 8 * rows(key)[0]:8 * rows(key)[0] + 1, sl(key[1][1])]
    strict = lambda key: (col < row) if key[1][0] == 0 else (col > row)
    incl = lambda key: (col <= row) if key[1][0] == 0 else (col >= row)

    keys = [(sub, ch) for sub in range(SCAN_STEP) for ch in chains]
    xb = {key: _mm(jnp.concatenate([al(key), rb(key)], axis=0), bh(key), _NT) for key in keys}
    xk = {key: _mm(jnp.concatenate([al(key), rb(key)], axis=0), kh(key), _NT) for key in keys}
    l_pows = {key: [jnp.where(strict(key), xb[key][:c], 0.0)] for key in keys}
    lakv = {key: _mm(jnp.where(strict(key), xk[key][:c], 0.0), vv(key)) for key in keys}
    for _ in range(5):
        for key in keys:
            l_pows[key].append(_mm(l_pows[key][-1], l_pows[key][-1]))

    for sub in range(SCAN_STEP):
        now = [(sub, ch) for ch in chains]
        u = {key: lakv[key] + _mm(al(key), s_ref[key[1]], _NT) for key in now}
        for it in range(6):
            u = {key: u[key] + _mm(l_pows[key][it], u[key]) for key in now}
        for key in now:
            o_ref = refs[key[1][0]][6]
            o_ref[rows(key)[1], sl(key[1][1])] = (
                _mm(rb(key), s_ref[key[1]], _NT)
                + _mm(jnp.where(incl(key), xb[key][c:], 0.0), u[key])
                + _mm(jnp.where(incl(key), xk[key][c:], 0.0), vv(key)))
        for key in now:
            upd = _mm(jnp.concatenate([u[key].astype(BF16), vv(key)], axis=0),
                      jnp.concatenate([bh(key), kh(key)], axis=0), _TN)
            s_ref[key[1]] = (s_ref[key[1]] + upd) * pe(key)


def _attn_ctx_kernel(q_ref, k_ref, vt_ref, o_ref):
    s = _dot_nt(k_ref[...], q_ref[...])
    p = jnp.exp2(s - jnp.max(s, axis=0, keepdims=True))
    l = jnp.sum(p, axis=0, keepdims=True)
    o_ref[...] = (_dot(vt_ref[...], p.astype(BF16)) / l).astype(BF16)


def _attn_kernel(q_ref, kc_ref, vtc_ref, kl_ref, vtl_ref, o_ref, *, k_chunk):
    q = q_ref[...]
    tq = q.shape[0]
    parts = [(kc_ref, vtc_ref, 0, kc_ref.shape[0])]
    parts += [(kl_ref, vtl_ref, c0, k_chunk) for c0 in range(0, kl_ref.shape[0], k_chunk)]
    scores = lambda part: _dot_nt(part[0][part[2]:part[2] + part[3], :], q)
    m = jnp.full((1, tq), -jnp.inf, F32)
    l = jnp.zeros((1, tq), F32)
    acc = jnp.zeros((o_ref.shape[0], tq), F32)
    s_next = scores(parts[0])
    for n, (_, vt_ref, c0, size) in enumerate(parts):
        s = s_next
        if n + 1 < len(parts):
            s_next = scores(parts[n + 1])
        m_new = jnp.maximum(m, jnp.max(s, axis=0, keepdims=True))
        p = jnp.exp2(s - m_new)
        corr = jnp.exp2(m - m_new)
        l = l * corr + jnp.sum(p, axis=0, keepdims=True)
        acc = acc * corr + _dot(vt_ref[:, c0:c0 + size], p.astype(BF16))
        m = m_new
    o_ref[...] = (acc / l).astype(BF16)


def _merge_kernel(of_ref, ob_ref, bonus_ref, g_ref, ya_ref, ybl_ref, ybc_ref, gate_ref, x_ref,
                  mod_ref, gng_ref, gnb_ref, wa_ref, wb_ref, wc_ref, wo_ref, n2_ref,
                  rw_ref, rb_ref, xn_out, h2_out, comb_out, *, n_lat_tiles):
    mod = mod_ref[0]
    yb_t = jnp.where(pl.program_id(0) < n_lat_tiles, ybl_ref[...], ybc_ref[...])
    o = of_ref[...] + ob_ref[...]
    lane = lax.broadcasted_iota(jnp.int32, (1, LANE), 1)
    real = lane < C_HEAD
    ycs = []
    for hh in range(C_HEADS):
        sl = slice(hh * LANE, (hh + 1) * LANE)
        oh = o[:, sl]
        mean = jnp.sum(oh, axis=-1, keepdims=True) * (1.0 / C_HEAD)
        dev = jnp.where(real, oh - mean, 0.0)
        var = jnp.sum(dev * dev, axis=-1, keepdims=True) * (1.0 / C_HEAD)
        y = dev * lax.rsqrt(var + C_GN_EPS) * gng_ref[:, sl] + gnb_ref[:, sl]
        ycs.append(((y + bonus_ref[:, sl]) * g_ref[:, sl]).astype(BF16))
    yc = jnp.concatenate(ycs, axis=1)
    gate = gate_ref[...].astype(F32)
    merged = (gate[:, :D_MODEL] * _dot(ya_ref[...], wa_ref[...])
              + gate[:, D_MODEL:2 * D_MODEL] * _dot_tn(yb_t, wb_ref[...])
              + gate[:, 2 * D_MODEL:] * _dot(yc, wc_ref[...]))
    xn = x_ref[...] + mod[2:3] * _dot(merged.astype(BF16), wo_ref[...])
    xn_out[...] = xn
    h2 = _normmod(xn, n2_ref[...], mod, 3).astype(BF16)
    h2_out[...] = h2

    scores = 1.0 / (1.0 + jnp.exp(-_dot(h2, rw_ref[...])))
    neg = -jnp.inf
    sel = jnp.where(lane < N_EXPERTS, scores + rb_ref[...], neg)

    lane_f = lane.astype(F32)

    def top1(s):
        mx = jnp.max(s, axis=-1, keepdims=True)
        idx = jnp.min(jnp.where(s == mx, lane_f, float(LANE)), axis=-1, keepdims=True)
        return mx, idx.astype(jnp.int32)

    best = None
    for g in range(N_GROUPS):
        in_g = (lane >= g * EXPERTS_PER_GROUP) & (lane < (g + 1) * EXPERTS_PER_GROUP)
        s = jnp.where(in_g, sel, neg)
        m1, i1 = top1(s)
        m2, _ = top1(jnp.where(lane == i1, neg, s))
        score = m1 + m2
        if best is None:
            best, gidx = score, jnp.zeros_like(i1)
        else:
            better = score > best
            gidx = jnp.where(better, g, gidx)
            best = jnp.where(better, score, best)
    lo = gidx * EXPERTS_PER_GROUP
    s = jnp.where((lane >= lo) & (lane < lo + EXPERTS_PER_GROUP), sel, neg)
    _, i1 = top1(s)
    _, i2 = top1(jnp.where(lane == i1, neg, s))
    picked = jnp.where((lane == i1) | (lane == i2), scores, 0.0)
    comb_out[...] = picked / jnp.sum(picked, axis=-1, keepdims=True)


def _moe_kernel(cnt_ref, h2_ref, comb_ref, xn_ref, *rest, last):
    mod_refs = rest[:MOE_SUB]
    w1_ref, w3_ref, w2_ref, tri_ref, fg_ref, out_ref, rank_t_ref, comb_t_ref = rest[MOE_SUB:]
    i = pl.program_id(0)
    e = pl.program_id(1)
    tm = h2_ref.shape[0]
    lane = lax.broadcasted_iota(jnp.int32, (1, LANE), 1)

    @pl.when(e == 0)
    def _():
        comb = comb_ref[...]
        picked = jnp.where(comb > 0.0, 1.0, 0.0).astype(BF16)
        rank = _dot(tri_ref[...], picked)
        rank_t_ref[...] = rank.T
        comb_t_ref[...] = comb.T
        out_ref[...] = jnp.zeros_like(out_ref)

    comb_e = jnp.sum(jnp.where(lane == e, comb_ref[...], 0.0), axis=-1, keepdims=True)
    rank_row = rank_t_ref[pl.ds(e, 1), :]
    picked_row = comb_t_ref[pl.ds(e, 1), :] > 0.0
    slot_col = lax.broadcasted_iota(jnp.int32, (MOE_CAP, 1), 0)
    n_blocks = lax.shift_right_logical(cnt_ref[i, e] + (MOE_CAP - 1), MOE_CAP.bit_length() - 1)

    def block(blk, carry):
        base = blk * MOE_CAP
        gather = jnp.where(picked_row & (rank_row == (slot_col + base).astype(F32)), 1.0, 0.0).astype(BF16)
        xg = _dot(gather, h2_ref[...]).astype(BF16)
        a = _dot(xg, w1_ref[0, 0].astype(BF16))
        he = (a * _sigmoid(a) * _dot(xg, w3_ref[0, 0].astype(BF16))).astype(BF16)
        y = _dot(he, w2_ref[0, 0].astype(BF16)).astype(BF16)
        out_ref[...] += comb_e * _dot_tn(gather, y)
        return carry

    lax.fori_loop(0, n_blocks, block, 0)

    @pl.when(e == N_EXPERTS - 1)
    def _():
        for s, m_ref in enumerate(mod_refs):
            rows = slice(s * SEG_ROWS, (s + 1) * SEG_ROWS)
            y = xn_ref[rows] + m_ref[0][5:6] * out_ref[rows]
            if last:
                y = _rms(y) * fg_ref[...]
            out_ref[rows] = y


def _head_pad_cols(w, width):
    lead = w.shape[:-1]
    w = w.reshape(lead + (C_HEADS, width))
    w = jnp.pad(w, [(0, 0)] * len(lead) + [(0, 0), (0, LANE - width)])
    return w.reshape(lead + (HP,))


def _head_pad_rows(w, width):
    n = w.shape[-1]
    w = w.reshape(C_HEADS, width, n)
    w = jnp.pad(w, ((0, 0), (0, LANE - width), (0, 0)))
    return w.reshape(HP, n)


def _split_rope(w):
    lead = w.shape[:-1]
    w = w.reshape(lead + (2, 2, ROPE_FREQS))
    x1 = w[..., 0, :].reshape(lead + (2 * ROPE_FREQS,))
    x2 = w[..., 1, :].reshape(lead + (2 * ROPE_FREQS,))
    return x1, x2


def _rope_tables(n_ctx, t):
    rows = t // GRID_W
    row = jnp.repeat(jnp.arange(rows, dtype=F32), GRID_W)
    col = jnp.tile(jnp.arange(GRID_W, dtype=F32), rows)
    inv = jnp.power(ROPE_BASE, -jnp.arange(ROPE_FREQS, dtype=F32) / ROPE_FREQS)
    ang = jnp.concatenate([row[:, None] * inv, col[:, None] * inv], axis=1)
    cos = jnp.concatenate([jnp.cos(ang), jnp.ones((n_ctx, 2 * ROPE_FREQS), F32)], axis=0)
    sin = jnp.concatenate([jnp.sin(ang), jnp.zeros((n_ctx, 2 * ROPE_FREQS), F32)], axis=0)
    n = n_ctx + t
    one = jnp.ones((n, B_NOPE), F32)
    zero = lambda w: jnp.zeros((n, w), F32)
    qs = MLA_SCALE * math.log2(math.e)
    cq = jnp.concatenate([one, cos, cos, zero(LANE - B_NOPE - B_ROPE)], axis=1) * qs
    sq = jnp.concatenate([zero(B_NOPE), -sin, sin, zero(LANE - B_NOPE - B_ROPE)], axis=1) * qs
    ck = jnp.concatenate([cos, cos, zero(LANE - B_ROPE)], axis=1)
    sk = jnp.concatenate([-sin, sin, zero(LANE - B_ROPE)], axis=1)
    return cq, sq, ck, sk


def _layer_weights(p, l):
    w = {}
    w_in = p['w_in'][l]
    d = w_in.shape[0]
    o = 0
    w_q = w_in[:, o:o + B_Q_LORA]; o += B_Q_LORA
    w_kv = w_in[:, o:o + B_KV_LORA]; o += B_KV_LORA
    w_kr = w_in[:, o:o + B_ROPE]; o += B_ROPE
    w_c = w_in[:, o:o + C_COLS]; o += C_COLS
    w_a = w_in[:, o:o + A_COLS]; o += A_COLS
    w_g = w_in[:, o:o + G_COLS]

    k1, k2 = _split_rope(w_kr)
    zpad = jnp.zeros((d, LANE - B_ROPE), F32)
    w['win_b'] = jnp.concatenate([w_q, w_kv, k1, k2, zpad, k2, k1, zpad], axis=1).astype(BF16)

    wuq = p['b_w_uq'][l].reshape(B_Q_LORA, B_HEADS, B_NOPE + B_ROPE)
    q1, q2 = _split_rope(wuq[..., B_NOPE:])
    qz = jnp.zeros((B_Q_LORA, B_HEADS, LANE - B_NOPE - B_ROPE), F32)
    w['wq'] = jnp.concatenate([wuq[..., :B_NOPE], q1, q2, qz], axis=-1).reshape(B_Q_LORA, HP).astype(BF16)
    w['wqs'] = jnp.concatenate([wuq[..., :B_NOPE], q2, q1, qz], axis=-1).reshape(B_Q_LORA, HP).astype(BF16)
    wukv = p['b_w_ukv'][l].reshape(B_KV_LORA, B_HEADS, B_NOPE + B_VDIM)
    w['wk'] = _head_pad_cols(wukv[..., :B_NOPE].reshape(B_KV_LORA, -1), B_NOPE).astype(BF16)
    w['wv'] = wukv[..., B_NOPE:].reshape(B_KV_LORA, -1).T.astype(BF16)
    place = jnp.zeros((LANE, B_HEADS, LANE), F32)
    idx = jnp.arange(B_ROPE)
    place = place.at[idx, :, B_NOPE + idx].set(1.0)
    w['place'] = place.reshape(LANE, HP).astype(BF16)
    w['qg'] = p['b_q_norm'][l][None]
    w['kvg'] = p['b_kv_norm'][l][None]

    w['wa'] = w_a.astype(BF16)
    w['wg'] = w_g.astype(BF16)
    w['vg'] = p['a_v_gain'][l][None]
    w['ws'] = p['a_ws'][l].astype(BF16)
    w['bsb'] = jnp.broadcast_to(p['a_bs'][l].T[:, :, None], (A_CHUNK, A_GROUPS, LANE)).reshape(A_CHUNK, A_WIDTH)

    cw = C_WIDTH
    pieces = [_head_pad_cols(w_c[:, i * cw:(i + 1) * cw], C_HEAD) for i in range(3)]
    w['wc'] = jnp.concatenate(pieces + [w_c[:, 3 * cw:]], axis=1).astype(BF16)
    mu = p['c_mu'][l]
    w['mu'] = jnp.concatenate([_head_pad_cols(mu[i * cw:(i + 1) * cw], C_HEAD) for i in range(3)]
                              + [mu[3 * cw:]])[None]

    def lora_block(m, rank):
        m = _head_pad_cols(m, C_HEAD)
        z = jnp.zeros_like(m[0])
        return jnp.stack([jnp.concatenate([m[0], z], axis=0), jnp.concatenate([z, m[1]], axis=0)]).astype(BF16)

    w['w2'] = lora_block(p['c_w2'][l], C_DECAY_LORA)
    w['a2'] = lora_block(p['c_a2'][l], C_AAA_LORA)
    w['w0'] = _head_pad_cols(p['c_w0'][l], C_HEAD)
    w['a0'] = _head_pad_cols(p['c_a0'][l], C_HEAD)
    w['kk'] = _head_pad_cols(p['c_k_k'][l], C_HEAD)[None]
    w['ka'] = _head_pad_cols(p['c_k_a'][l], C_HEAD)[None]
    w['rk'] = _head_pad_cols(p['c_r_k'][l].reshape(-1), C_HEAD)[None]
    w['g2'] = _head_pad_cols(p['c_g2'][l], C_HEAD).astype(BF16)
    w['gng'] = _head_pad_cols(p['c_gn_g'][l], C_HEAD)[None]
    w['gnb'] = _head_pad_cols(p['c_gn_b'][l], C_HEAD)[None]

    w['wup_a'] = p['w_up_a'][l].astype(BF16)
    w['wup_b'] = p['w_up_b'][l].astype(BF16)
    w['wup_c'] = _head_pad_rows(p['w_up_c'][l], C_HEAD).astype(BF16)
    w['wo'] = p['w_o'][l].astype(BF16)
    w['g1'] = p['norm1_g'][l][None]
    w['n2'] = p['norm2_g'][l][None]
    return w


def _layer(xs, mods, w, moe_w, tabs, rw, rb, fg, *, layer, bn, t, n_ctx, last):
    m, d = xs.shape
    n_tiles = m // TM
    n_lat = bn * t
    n_lat_tiles = n_lat // TM
    tpl = t // TM
    row = lambda i: (i, 0)
    seg_of = lambda blk: jnp.where(blk < n_lat // SEG_ROWS, 1 + blk // (t // SEG_ROWS), 0)
    seg = lambda i: (seg_of(i * (TM // SEG_ROWS)), 0, 0)
    x_spec = pl.BlockSpec((TM, d), row)
    mod_spec = pl.BlockSpec((1, 6, d), seg)
    hp_spec = pl.BlockSpec((TM, HP), row)
    hp2_spec = pl.BlockSpec((2, TM, HP), lambda i: (0, i, 0))
    hp_f32 = jax.ShapeDtypeStruct((m, HP), F32)
    hp_bf16 = jax.ShapeDtypeStruct((m, HP), BF16)
    hp2_bf16 = jax.ShapeDtypeStruct((2, m, HP), BF16)
    cq, sq, ck, sk = tabs
    tab_spec = pl.BlockSpec(
        (TM, LANE), lambda i: (jnp.where(i < n_lat_tiles, i % tpl, tpl + i - n_lat_tiles), 0))

    yb_w = B_HEADS * B_VDIM
    q, k, vt = pl.pallas_call(
        _mla_kernel,
        grid=(n_tiles,),
        in_specs=[x_spec, _full((1, d)), mod_spec, _full(w['win_b'].shape), _full((1, B_Q_LORA)),
                  _full((1, B_KV_LORA)), _full(w['wq'].shape), _full(w['wqs'].shape),
                  _full(w['wk'].shape), _full(w['wv'].shape), _full(w['place'].shape),
                  tab_spec, tab_spec, tab_spec, tab_spec],
        out_specs=[hp_spec, hp_spec, pl.BlockSpec((yb_w, TM), lambda i: (0, i))],
        out_shape=[hp_bf16, hp_bf16, jax.ShapeDtypeStruct((yb_w, m), BF16)],
        compiler_params=_params(1),
        name="mla_proj",
    )(xs, w['g1'], mods, w['win_b'], w['qg'], w['kvg'], w['wq'], w['wqs'], w['wk'], w['wv'],
      w['place'], cq, sq, ck, sk)

    ya, gate = pl.pallas_call(
        _gmlp_kernel,
        grid=(n_tiles,),
        in_specs=[x_spec, _full((1, d)), mod_spec, _full(w['wa'].shape), _full(w['wg'].shape),
                  _full((1, A_WIDTH)), _full(w['ws'].shape), _full(w['bsb'].shape)],
        out_specs=[pl.BlockSpec((TM, A_WIDTH), row), pl.BlockSpec((TM, G_COLS), row)],
        out_shape=[jax.ShapeDtypeStruct((m, A_WIDTH), BF16), jax.ShapeDtypeStruct((m, G_COLS), BF16)],
        compiler_params=_params(1),
        name="gmlp_gates",
    )(xs, w['g1'], mods, w['wa'], w['wg'], w['vg'], w['ws'], w['bsb'])

    tf = TM_FEAT
    sub = tf // 8
    seq_starts = tuple(b * t for b in range(bn)) + tuple(n_lat + b * n_ctx for b in range(bn))
    seq_ends = tuple(s - 1 for s in seq_starts[1:]) + (m - 1,)
    assert all(s % tf == 0 for s in seq_starts)
    f_hp = pl.BlockSpec((tf, HP), row)
    f_hp2 = pl.BlockSpec((2, tf, HP), lambda i: (0, i, 0))
    pe_spec = pl.BlockSpec((2, 8 * (tf // SCAN_C), HP), lambda i: (0, i, 0))
    pe_shape = jax.ShapeDtypeStruct((2, 8 * (m // SCAN_C), HP), F32)
    vv, ab, bh, kh, rbar, pe, bonus, gg = pl.pallas_call(
        functools.partial(_feat_kernel, seq_starts=seq_starts, seq_ends=seq_ends),
        grid=(m // tf,),
        in_specs=[pl.BlockSpec((tf, d), row),
                  pl.BlockSpec((8, d), lambda i: (jnp.maximum(i * sub - 1, 0), 0)),
                  pl.BlockSpec((8, d), lambda i: (jnp.minimum((i + 1) * sub, m // 8 - 1), 0)),
                  _full((1, d)), pl.BlockSpec((1, 6, d), lambda i: (seg_of(i * (tf // SEG_ROWS)), 0, 0)),
                  _full(w['wc'].shape), _full(w['mu'].shape),
                  _full((2, HP)), _full(w['w2'].shape), _full((2, HP)), _full(w['a2'].shape),
                  _full((1, HP)), _full((1, HP)), _full((1, HP)), _full(w['g2'].shape)],
        out_specs=[f_hp, f_hp2, f_hp2, f_hp2, f_hp2, pe_spec, f_hp, f_hp],
        out_shape=[hp_bf16, hp2_bf16, hp2_bf16, hp2_bf16, hp2_bf16, pe_shape, hp_f32, hp_f32],
        compiler_params=_params(1),
        name="rwkv_features",
    )(xs, xs, xs, w['g1'], mods, w['wc'], w['mu'], w['w0'], w['w2'], w['a0'], w['a2'],
      w['kk'], w['ka'], w['rk'], w['g2'])

    blk_rows = SCAN_C * SCAN_STEP
    nctx_c = n_ctx // blk_rows
    nlat_c = t // blk_rows
    nc = nctx_c + nlat_c

    def chunk_of(dd, b, j):
        ctx_base = n_lat // blk_rows + b * nctx_c
        lat_base = b * nlat_c
        if dd == 0:
            return jnp.where(j < nctx_c, ctx_base + j, lat_base + j - nctx_c)
        return jnp.where(j < nctx_c, ctx_base + nctx_c - 1 - j, lat_base + nc - 1 - j)

    def tok_spec(dd):
        return pl.BlockSpec((blk_rows, HP), lambda b, j: (chunk_of(dd, b, j), 0))

    def dir_spec(dd, rows=blk_rows):
        return pl.BlockSpec((1, rows, HP), lambda b, j: (dd, chunk_of(dd, b, j), 0))

    o_f, o_b = pl.pallas_call(
        _scan_kernel,
        grid=(bn, nc),
        in_specs=[tok_spec(0), tok_spec(1)] + [dir_spec(dd) for _ in range(4) for dd in range(2)]
                 + [dir_spec(0, 8 * SCAN_STEP), dir_spec(1, 8 * SCAN_STEP)],
        out_specs=[tok_spec(0), tok_spec(1)],
        out_shape=[hp_f32, hp_f32],
        scratch_shapes=[pltpu.VMEM((2, C_HEADS, LANE, LANE), F32)],
        compiler_params=_params(2),
        name="rwkv_scan",
    )(vv, vv, ab, ab, bh, bh, kh, kh, rbar, rbar, pe, pe)

    ctx_blk = n_lat // n_ctx
    tq = ATTN_TQ
    ctx_k =pl.BlockSpec((n_ctx, LANE), lambda b, hh, qi: (ctx_blk + b, hh))
    ctx_vt = pl.BlockSpec((B_VDIM, n_ctx), lambda b, hh, qi: (hh, ctx_blk + b))
    lat_k = pl.BlockSpec((t, LANE), lambda b, hh, qi: (b, hh))
    lat_vt = pl.BlockSpec((B_VDIM, t), lambda b, hh, qi: (hh, b))
    q_lat = pl.BlockSpec((tq, LANE), lambda b, hh, qi: (b * (t // tq) + qi, hh))
    yb_lat_t = pl.pallas_call(
        functools.partial(_attn_kernel, k_chunk=ATTN_KC),
        grid=(bn, B_HEADS, t // tq),
        in_specs=[q_lat, ctx_k, ctx_vt, lat_k, lat_vt],
        out_specs=pl.BlockSpec((B_VDIM, tq), lambda b, hh, qi: (hh, b * (t // tq) + qi)),
        out_shape=jax.ShapeDtypeStruct((B_HEADS * B_VDIM, n_lat), BF16),
        compiler_params=_params(3),
        name="mla_attention",
    )(q, k, vt, k, vt)
    ctx_q = pl.BlockSpec((n_ctx, LANE), lambda b, hh: (ctx_blk + b, hh))
    yb_ctx_t = pl.pallas_call(
        _attn_ctx_kernel,
        grid=(bn, B_HEADS),
        in_specs=[ctx_q, ctx_q, pl.BlockSpec((B_VDIM, n_ctx), lambda b, hh: (hh, ctx_blk + b))],
        out_specs=pl.BlockSpec((B_VDIM, n_ctx), lambda b, hh: (hh, b)),
        out_shape=jax.ShapeDtypeStruct((yb_w, bn * n_ctx), BF16),
        compiler_params=_params(2),
        name="mla_attention_ctx",
    )(q, k, vt)

    n_ctx_tiles = n_tiles - n_lat_tiles
    xn, h2, comb = pl.pallas_call(
        functools.partial(_merge_kernel, n_lat_tiles=n_lat_tiles),
        grid=(n_tiles,),
        in_specs=[hp_spec, hp_spec, hp_spec, hp_spec, pl.BlockSpec((TM, A_WIDTH), row),
                  pl.BlockSpec((yb_w, TM), lambda i: (0, jnp.minimum(i, n_lat_tiles - 1))),
                  pl.BlockSpec((yb_w, TM), lambda i: (0, jnp.clip(i - n_lat_tiles, 0, n_ctx_tiles - 1))),
                  pl.BlockSpec((TM, G_COLS), row), x_spec, mod_spec, _full((1, HP)), _full((1, HP)),
                  _full(w['wup_a'].shape), _full(w['wup_b'].shape), _full(w['wup_c'].shape),
                  _full(w['wo'].shape), _full((1, d)), _full(rw.shape), _full(rb.shape)],
        out_specs=[x_spec, x_spec, pl.BlockSpec((TM, LANE), row)],
        out_shape=[jax.ShapeDtypeStruct((m, d), F32), jax.ShapeDtypeStruct((m, d), BF16),
                   jax.ShapeDtypeStruct((m, LANE), F32)],
        compiler_params=_params(1),
        name="merge_router",
    )(o_f, o_b, bonus, gg, ya, yb_lat_t, yb_ctx_t, gate, xs, mods, w['gng'], w['gnb'], w['wup_a'], w['wup_b'],
      w['wup_c'], w['wo'], w['n2'], rw, rb)

    tm_moe = MOE_SUB * SEG_ROWS
    counts = jnp.sum((comb > 0.0).reshape(m // tm_moe, tm_moe, LANE), axis=1, dtype=jnp.int32)
    tri = jnp.tril(jnp.ones((tm_moe, tm_moe), BF16), -1)
    mrow = lambda i, e, cnt: (i, 0)
    once = pl.Buffered(1)
    mseg = lambda s: pl.BlockSpec((1, 6, d), lambda i, e, cnt: (seg_of(i * MOE_SUB + s), 0, 0))
    wexp = lambda shape: pl.BlockSpec((1, 1) + shape, lambda i, e, cnt: (layer, e, 0, 0))
    grid_spec = pltpu.PrefetchScalarGridSpec(
        num_scalar_prefetch=1,
        grid=(m // tm_moe, N_EXPERTS),
        in_specs=[pl.BlockSpec((tm_moe, d), mrow), pl.BlockSpec((tm_moe, LANE), mrow),
                  pl.BlockSpec((tm_moe, d), mrow, pipeline_mode=once)]
                 + [mseg(s) for s in range(MOE_SUB)]
                 + [wexp((d, D_EXPERT)), wexp((d, D_EXPERT)), wexp((D_EXPERT, d)),
                    pl.BlockSpec((tm_moe, tm_moe), lambda i, e, cnt: (0, 0), pipeline_mode=once),
                    pl.BlockSpec((1, d), lambda i, e, cnt: (0, 0))],
        out_specs=pl.BlockSpec((tm_moe, d), mrow),
        scratch_shapes=[pltpu.VMEM((LANE, tm_moe), F32),
                        pltpu.VMEM((LANE, tm_moe), F32)],
    )
    return pl.pallas_call(
        functools.partial(_moe_kernel, last=last),
        grid_spec=grid_spec,
        out_shape=jax.ShapeDtypeStruct((n_lat if last else m, d), F32),
        compiler_params=_params(2),
        name="moe",
    )(counts, h2, comb, xn, *([mods] * MOE_SUB), moe_w[0], moe_w[1], moe_w[2], tri, fg)


def kernel(x, c, ctx, c_ctx, mod_w, mod_b, norm1_g, norm2_g, w_in, a_v_gain, a_ws, a_bs, b_q_norm, b_w_uq, b_kv_norm, b_w_ukv, c_mu, c_w0, c_w2, c_a0, c_a2, c_g2, c_k_k, c_k_a, c_r_k, c_gn_g, c_gn_b, w_up_a, w_up_b, w_up_c, w_o, router_w, router_b, moe_w1, moe_w3, moe_w2, final_g):
    p = dict(w_in=w_in, a_v_gain=a_v_gain, a_ws=a_ws, a_bs=a_bs, b_q_norm=b_q_norm, b_w_uq=b_w_uq,
             b_kv_norm=b_kv_norm, b_w_ukv=b_w_ukv, c_mu=c_mu, c_w0=c_w0, c_w2=c_w2, c_a0=c_a0,
             c_a2=c_a2, c_g2=c_g2, c_k_k=c_k_k, c_k_a=c_k_a, c_r_k=c_r_k, c_gn_g=c_gn_g,
             c_gn_b=c_gn_b, w_up_a=w_up_a, w_up_b=w_up_b, w_up_c=w_up_c, w_o=w_o,
             norm1_g=norm1_g, norm2_g=norm2_g)
    bn, t, d = x.shape
    n_ctx = ctx.shape[1]
    depth = mod_w.shape[0]
    m = bn * (t + n_ctx)
    assert n_ctx == SEG_ROWS and (bn * n_ctx) % TM == 0 and t % TM == 0 and t % ATTN_TQ == 0
    assert t % ATTN_KC == 0 and m % (MOE_SUB * SEG_ROWS) == 0 and 1 + bn <= 8

    cvec = jnp.concatenate([c_ctx[None], c, jnp.zeros((8 - 1 - bn, d), F32)], axis=0)
    mods = _modulation(cvec, mod_w, mod_b).reshape(depth, 8, 6, d)

    tabs = _rope_tables(bn * n_ctx, t)
    rw = jnp.pad(router_w, ((0, 0), (0, LANE - N_EXPERTS))).astype(BF16)
    rb = jnp.pad(router_b, (0, LANE - N_EXPERTS))[None]
    fg = final_g[None]

    xs = jnp.concatenate([x.reshape(bn * t, d), ctx.reshape(bn * n_ctx, d)], axis=0)
    for l in range(depth):
        xs = _layer(xs, mods[l], _layer_weights(p, l), (moe_w1, moe_w3, moe_w2), tabs, rw, rb, fg,
                    layer=l, bn=bn, t=t, n_ctx=n_ctx, last=(l == depth - 1))
    return xs.reshape(bn, t, d)
```

```python
import functools
import math

import jax
import jax.numpy as jnp
from jax import lax
from jax.experimental import pallas as pl
from jax.experimental.pallas import tpu as pltpu

F32 = jnp.float32
BF16 = jnp.bfloat16

D_MODEL = 1024
GRID_W = 64
EPS = 1e-6

A_WIDTH = 512
A_GROUPS = 4
A_CHUNK = 128

B_HEADS = 8
B_Q_LORA = 384
B_KV_LORA = 256
B_NOPE = 64
B_ROPE = 32
B_VDIM = 64
ROPE_BASE = 10000.0
ROPE_FREQS = B_ROPE // 4
MLA_SCALE = (B_NOPE + B_ROPE) ** -0.5

C_HEADS = 8
C_HEAD = 64
C_WIDTH = C_HEADS * C_HEAD
C_DECAY_LORA = 64
C_AAA_LORA = 64
C_GATE_LORA = 128
C_GN_EPS = 64e-5
DECAY_SCALE = 0.6065306597126334

B_COLS = B_Q_LORA + B_KV_LORA + B_ROPE
C_COLS = 3 * C_WIDTH + 2 * C_DECAY_LORA + 2 * C_AAA_LORA + C_GATE_LORA
A_COLS = 2 * A_WIDTH
G_COLS = 3 * D_MODEL

N_EXPERTS = 16
N_GROUPS = 4
EXPERTS_PER_GROUP = 4
D_EXPERT = 512

LANE = 128
HP = C_HEADS * LANE
TM = 512
TM_FEAT = 256
SEG_ROWS = 256
SCAN_C = 64
SCAN_STEP = 4
CUM_ROWS = 256
MOE_SUB = 6
MOE_CAP = 256
ATTN_TQ = 512
ATTN_KC = 1024
V7X_VMEM_BYTES = 64 * 1024 * 1024
VMEM_LIMIT = V7X_VMEM_BYTES - 12 * 1024 * 1024


def _dot(a, b):
    return jnp.dot(a, b, preferred_element_type=F32)


def _dot_nt(a, b):
    return lax.dot_general(a, b, (((1,), (1,)), ((), ())), preferred_element_type=F32)


def _dot_tn(a, b):
    return lax.dot_general(a, b, (((0,), (0,)), ((), ())), preferred_element_type=F32)


def _rms(x, eps=EPS):
    return x * lax.rsqrt(jnp.mean(x * x, axis=-1, keepdims=True) + eps)


def _normmod(x, g, mod, first):
    return _rms(x) * g * (1.0 + mod[first + 1:first + 2]) + mod[first:first + 1]


def _sigmoid(x):
    return 0.5 * jnp.tanh(0.5 * x) + 0.5


def _params(n_grid):
    return pltpu.CompilerParams(
        dimension_semantics=("arbitrary",) * n_grid, vmem_limit_bytes=VMEM_LIMIT)


def _full(shape):
    n = len(shape)
    return pl.BlockSpec(shape, lambda *_: (0,) * n)


def _mod_kernel(c_ref, w_ref, b_ref, o_ref):
    c = c_ref[...]
    act = (c * _sigmoid(c)).astype(BF16)
    o_ref[0] = _dot(act, w_ref[0].astype(BF16)) + b_ref[0]


def _modulation(cvec, mod_w, mod_b):
    depth, d, n = mod_w.shape
    tn = 1536
    return pl.pallas_call(
        _mod_kernel,
        grid=(depth, n // tn),
        in_specs=[
            pl.BlockSpec((8, d), lambda l, j: (0, 0)),
            pl.BlockSpec((1, d, tn), lambda l, j: (l, 0, j)),
            pl.BlockSpec((1, 1, tn), lambda l, j: (l, 0, j)),
        ],
        out_specs=pl.BlockSpec((1, 8, tn), lambda l, j: (l, 0, j)),
        out_shape=jax.ShapeDtypeStruct((depth, 8, n), F32),
        compiler_params=_params(2),
        name="modulation",
    )(cvec, mod_w, mod_b.reshape(depth, 1, n))


def _mla_kernel(x_ref, g1_ref, mod_ref, win_ref, qg_ref, kvg_ref, wq_ref, wqs_ref,
                wk_ref, wv_ref, place_ref, cq_ref, sq_ref, ck_ref, sk_ref,
                q_out, k_out, vt_out):
    h = _normmod(x_ref[...], g1_ref[...], mod_ref[0], 0).astype(BF16)
    z = _dot(h, win_ref[...])
    zq = z[:, :B_Q_LORA]
    zkv = z[:, B_Q_LORA:B_Q_LORA + B_KV_LORA]
    kr_a = z[:, B_Q_LORA + B_KV_LORA:B_Q_LORA + B_KV_LORA + LANE]
    kr_b = z[:, B_Q_LORA + B_KV_LORA + LANE:]
    qn = (_rms(zq) * qg_ref[...]).astype(BF16)
    kvn = (_rms(zkv) * kvg_ref[...]).astype(BF16)
    q1 = _dot(qn, wq_ref[...])
    q2 = _dot(qn, wqs_ref[...])
    kr = (kr_a * ck_ref[...] + kr_b * sk_ref[...]).astype(BF16)
    k = _dot(kvn, wk_ref[...]) + _dot(kr, place_ref[...])
    cq = cq_ref[...]
    sq = sq_ref[...]
    for hh in range(B_HEADS):
        sl = slice(hh * LANE, (hh + 1) * LANE)
        q_out[:, sl] = (q1[:, sl] * cq + q2[:, sl] * sq).astype(BF16)
    k_out[...] = k.astype(BF16)
    vt_out[...] = _dot_nt(wv_ref[...], kvn).astype(BF16)


def _gelu_tanh(x):
    return 0.5 * x * (1.0 + jnp.tanh(math.sqrt(2.0 / math.pi) * (x + 0.044715 * (x * x * x))))


def _gmlp_kernel(x_ref, g1_ref, mod_ref, wa_ref, wg_ref, vg_ref, ws_ref, bsb_ref,
                 ya_out, gate_out):
    h = _normmod(x_ref[...], g1_ref[...], mod_ref[0], 0).astype(BF16)
    gate_out[...] = _sigmoid(_dot(h, wg_ref[...])).astype(BF16)
    zg = _gelu_tanh(_dot(h, wa_ref[...]))
    u = zg[:, :A_WIDTH]
    v = (_rms(zg[:, A_WIDTH:]) * vg_ref[...]).astype(BF16)
    for c in range(TM // A_CHUNK):
        rows = slice(c * A_CHUNK, (c + 1) * A_CHUNK)
        for g in range(A_GROUPS):
            cols = slice(g * LANE, (g + 1) * LANE)
            mixed = _dot(ws_ref[g], v[rows, cols]) + bsb_ref[:, cols]
            ya_out[rows, cols] = (u[rows, cols] * mixed).astype(BF16)


def _feat_kernel(x_ref, xp_ref, xn_ref, g1_ref, mod_ref, wc_ref, mu_ref,
                 w0_ref, w2_ref, a0_ref, a2_ref, kk_ref, ka_ref, rk_ref, g2_ref,
                 v_out, ab_out, bh_out, kh_out, rb_out, pe_out, bonus_out, g_out,
                 *, seq_starts, seq_ends):
    mod = mod_ref[0]
    g1 = g1_ref[...]
    h = _normmod(x_ref[...], g1, mod, 0).astype(BF16)
    hp = _normmod(xp_ref[...], g1, mod, 0).astype(BF16)
    hn = _normmod(xn_ref[...], g1, mod, 0).astype(BF16)
    tm = x_ref.shape[0]
    row = lax.broadcasted_iota(jnp.int32, (tm, 1), 0)
    first = pl.program_id(0) * tm
    has_prev = 1.0 - functools.reduce(jnp.maximum, [jnp.where(first == s, 1.0, 0.0) for s in seq_starts])
    has_next = 1.0 - functools.reduce(jnp.maximum,
                                      [jnp.where(first + tm - 1 == e, 1.0, 0.0) for e in seq_ends])

    def shifted_proj(cols):
        w = wc_ref[:, cols]
        z = _dot(h, w)
        z_up = pltpu.roll(z, 1, 0)
        z_dn = pltpu.roll(z, tm - 1, 0)
        z_up = jnp.concatenate(
            [jnp.where(row[:8] == 0, _dot(hp, w)[7:8] * has_prev, z_up[:8]), z_up[8:]], axis=0)
        z_dn = jnp.concatenate(
            [z_dn[:tm - 8], jnp.where(row[tm - 8:] == tm - 1, _dot(hn, w)[0:1] * has_next, z_dn[tm - 8:])],
            axis=0)
        return z + mu_ref[:, cols] * (0.5 * (z_up + z_dn) - z)

    zl = shifted_proj(slice(3 * HP, 3 * HP + 3 * LANE))
    zw = jnp.tanh(zl[:, :LANE]).astype(BF16)
    za = zl[:, LANE:2 * LANE].astype(BF16)
    zg = _sigmoid(zl[:, 2 * LANE:]).astype(BF16)
    w_logit = [w0_ref[d:d + 1] + _dot(zw, w2_ref[d]) for d in range(2)]
    a_logit = [a0_ref[d:d + 1] + _dot(za, a2_ref[d]) for d in range(2)]
    g_out[...] = _dot(zg, g2_ref[...])
    r = shifted_proj(slice(0, HP))

    ti = lax.broadcasted_iota(jnp.int32, (CUM_ROWS, CUM_ROWS), 0)
    tj = lax.broadcasted_iota(jnp.int32, (CUM_ROWS, CUM_ROWS), 1)
    shift = SCAN_C.bit_length() - 1
    same = lax.shift_right_logical(ti, shift) == lax.shift_right_logical(tj, shift)
    lws, css = [], []
    for d in range(2):
        lw = -DECAY_SCALE * _sigmoid(w_logit[d])
        tri = jnp.where(same & ((tj <= ti) if d == 0 else (tj >= ti)), 1.0, 0.0).astype(BF16)
        hi = lw.astype(BF16)
        lo = (lw - hi.astype(F32)).astype(BF16)
        css.append(jnp.concatenate(
            [_dot(tri, hi[rows]) + _dot(tri, lo[rows])
             for rows in (slice(c0, c0 + CUM_ROWS) for c0 in range(0, tm, CUM_ROWS))], axis=0))
        lws.append(lw)
    k = shifted_proj(slice(HP, 2 * HP))
    v = shifted_proj(slice(2 * HP, 3 * HP))
    v_out[...] = v.astype(BF16)

    kk = k * kk_ref[...]
    ka = ka_ref[...]
    rk = rk_ref[...]
    a = [_sigmoid(a_logit[d]) for d in range(2)]
    kd = [k * (1.0 + (a[d] - 1.0) * ka) for d in range(2)]
    p_in, p_prev, p_inv = [], [], []
    for d in range(2):
        p_in.append(jnp.exp(css[d]))
        p_prev.append(jnp.exp(css[d] - lws[d]))
        p_inv.append(jnp.exp(-css[d]))
        kh_out[d] = (kd[d] * p_inv[d]).astype(BF16)
        rb_out[d] = (r * p_in[d]).astype(BF16)
        for cc in range(tm // SCAN_C):
            last = cc * SCAN_C + (SCAN_C - 1 if d == 0 else 0)
            pe_out[d, cc * 8:(cc + 1) * 8, :] = jnp.broadcast_to(p_in[d][last:last + 1], (8, HP))

    for hh in range(C_HEADS):
        sl = slice(hh * LANE, (hh + 1) * LANE)
        kh = kk[:, sl]
        kh = kh * lax.rsqrt(jnp.maximum(jnp.sum(kh * kh, axis=-1, keepdims=True), 1e-12))
        rh = r[:, sl] * rk[:, sl]
        bon = jnp.zeros((tm, 1), F32)
        for d in range(2):
            ab_out[d, :, sl] = (-kh * p_prev[d][:, sl]).astype(BF16)
            bh_out[d, :, sl] = (kh * a[d][:, sl] * p_inv[d][:, sl]).astype(BF16)
            bon = bon + jnp.sum(rh * kd[d][:, sl], axis=-1, keepdims=True)
        bonus_out[:, sl] = bon * v[:, sl]


def _mm(a, b, dims=((1,), (0,))):
    return lax.dot_general(a.astype(BF16), b.astype(BF16), (dims, ((), ())),
                           preferred_element_type=F32)


_NT = ((1,), (1,))
_TN = ((0,), (0,))


def _scan_kernel(vf_ref, vb_ref, abf_ref, abb_ref, bhf_ref, bhb_ref, khf_ref, khb_ref,
                 rbf_ref, rbb_ref, pef_ref, peb_ref, of_ref, ob_ref, s_ref):
    @pl.when(pl.program_id(1) == 0)
    def _():
        s_ref[...] = jnp.zeros_like(s_ref)

    c = SCAN_C
    row = lax.broadcasted_iota(jnp.int32, (c, c), 0)
    col = lax.broadcasted_iota(jnp.int32, (c, c), 1)
    refs = ((vf_ref, abf_ref, bhf_ref, khf_ref, rbf_ref, pef_ref, of_ref),
            (vb_ref, abb_ref, bhb_ref, khb_ref, rbb_ref, peb_ref, ob_ref))

    chains = [(d, hh) for d in range(2) for hh in range(C_HEADS)]

    def sl(hh):
        return slice(hh * LANE, (hh + 1) * LANE)

    def rows(key):
        sub, (d, _) = key
        k = sub if d == 0 else SCAN_STEP - 1 - sub
        return k, slice(k * c, (k + 1) * c)

    al = lambda key: refs[key[1][0]][1][0, rows(key)[1], sl(key[1][1])]
    bh = lambda key: refs[key[1][0]][2][0, rows(key)[1], sl(key[1][1])]
    kh = lambda key: refs[key[1][0]][3][0, rows(key)[1], sl(key[1][1])]
    rb = lambda key: refs[key[1][0]][4][0, rows(key)[1], sl(key[1][1])]
    vv = lambda key: refs[key[1][0]][0][rows(key)[1], sl(key[1][1])]
    pe = lambda key: refs[key[1][0]][5][0, 8 * rows(key)[0]:8 * rows(key)[0] + 1, sl(key[1][1])]
    strict = lambda key: (col < row) if key[1][0] == 0 else (col > row)
    incl = lambda key: (col <= row) if key[1][0] == 0 else (col >= row)

    keys = [(sub, ch) for sub in range(SCAN_STEP) for ch in chains]
    xb = {key: _mm(jnp.concatenate([al(key), rb(key)], axis=0), bh(key), _NT) for key in keys}
    xk = {key: _mm(jnp.concatenate([al(key), rb(key)], axis=0), kh(key), _NT) for key in keys}
    l_pows = {key: [jnp.where(strict(key), xb[key][:c], 0.0)] for key in keys}
    lakv = {key: _mm(jnp.where(strict(key), xk[key][:c], 0.0), vv(key)) for key in keys}
    for _ in range(5):
        for key in keys:
            l_pows[key].append(_mm(l_pows[key][-1], l_pows[key][-1]))

    for sub in range(SCAN_STEP):
        now = [(sub, ch) for ch in chains]
        u = {key: lakv[key] + _mm(al(key), s_ref[key[1]], _NT) for key in now}
        for it in range(6):
            u = {key: u[key] + _mm(l_pows[key][it], u[key]) for key in now}
        for key in now:
            o_ref = refs[key[1][0]][6]
            o_ref[rows(key)[1], sl(key[1][1])] = (
                _mm(rb(key), s_ref[key[1]], _NT)
                + _mm(jnp.where(incl(key), xb[key][c:], 0.0), u[key])
                + _mm(jnp.where(incl(key), xk[key][c:], 0.0), vv(key)))
        for key in now:
            upd = _mm(jnp.concatenate([u[key].astype(BF16), vv(key)], axis=0),
                      jnp.concatenate([bh(key), kh(key)], axis=0), _TN)
            s_ref[key[1]] = (s_ref[key[1]] + upd) * pe(key)


def _attn_ctx_kernel(q_ref, k_ref, vt_ref, o_ref):
    s = _dot_nt(k_ref[...], q_ref[...])
    p = jnp.exp2(s - jnp.max(s, axis=0, keepdims=True))
    l = jnp.sum(p, axis=0, keepdims=True)
    o_ref[...] = (_dot(vt_ref[...], p.astype(BF16)) / l).astype(BF16)


def _attn_kernel(q_ref, kc_ref, vtc_ref, kl_ref, vtl_ref, o_ref, *, k_chunk):
    q = q_ref[...]
    tq = q.shape[0]
    parts = [(kc_ref, vtc_ref, 0, kc_ref.shape[0])]
    parts += [(kl_ref, vtl_ref, c0, k_chunk) for c0 in range(0, kl_ref.shape[0], k_chunk)]
    scores = lambda part: _dot_nt(part[0][part[2]:part[2] + part[3], :], q)
    m = jnp.full((1, tq), -jnp.inf, F32)
    l = jnp.zeros((1, tq), F32)
    acc = jnp.zeros((o_ref.shape[0], tq), F32)
    s_next = scores(parts[0])
    for n, (_, vt_ref, c0, size) in enumerate(parts):
        s = s_next
        if n + 1 < len(parts):
            s_next = scores(parts[n + 1])
        m_new = jnp.maximum(m, jnp.max(s, axis=0, keepdims=True))
        p = jnp.exp2(s - m_new)
        corr = jnp.exp2(m - m_new)
        l = l * corr + jnp.sum(p, axis=0, keepdims=True)
        acc = acc * corr + _dot(vt_ref[:, c0:c0 + size], p.astype(BF16))
        m = m_new
    o_ref[...] = (acc / l).astype(BF16)


def _merge_kernel(of_ref, ob_ref, bonus_ref, g_ref, ya_ref, ybl_ref, ybc_ref, gate_ref, x_ref,
                  mod_ref, gng_ref, gnb_ref, wa_ref, wb_ref, wc_ref, wo_ref, n2_ref,
                  rw_ref, rb_ref, xn_out, h2_out, comb_out, *, n_lat_tiles):
    mod = mod_ref[0]
    yb_t = jnp.where(pl.program_id(0) < n_lat_tiles, ybl_ref[...], ybc_ref[...])
    o = of_ref[...] + ob_ref[...]
    lane = lax.broadcasted_iota(jnp.int32, (1, LANE), 1)
    real = lane < C_HEAD
    ycs = []
    for hh in range(C_HEADS):
        sl = slice(hh * LANE, (hh + 1) * LANE)
        oh = o[:, sl]
        mean = jnp.sum(oh, axis=-1, keepdims=True) * (1.0 / C_HEAD)
        dev = jnp.where(real, oh - mean, 0.0)
        var = jnp.sum(dev * dev, axis=-1, keepdims=True) * (1.0 / C_HEAD)
        y = dev * lax.rsqrt(var + C_GN_EPS) * gng_ref[:, sl] + gnb_ref[:, sl]
        ycs.append(((y + bonus_ref[:, sl]) * g_ref[:, sl]).astype(BF16))
    yc = jnp.concatenate(ycs, axis=1)
    gate = gate_ref[...].astype(F32)
    merged = (gate[:, :D_MODEL] * _dot(ya_ref[...], wa_ref[...])
              + gate[:, D_MODEL:2 * D_MODEL] * _dot_tn(yb_t, wb_ref[...])
              + gate[:, 2 * D_MODEL:] * _dot(yc, wc_ref[...]))
    xn = x_ref[...] + mod[2:3] * _dot(merged.astype(BF16), wo_ref[...])
    xn_out[...] = xn
    h2 = _normmod(xn, n2_ref[...], mod, 3).astype(BF16)
    h2_out[...] = h2

    scores = 1.0 / (1.0 + jnp.exp(-_dot(h2, rw_ref[...])))
    neg = -jnp.inf
    sel = jnp.where(lane < N_EXPERTS, scores + rb_ref[...], neg)

    lane_f = lane.astype(F32)

    def top1(s):
        mx = jnp.max(s, axis=-1, keepdims=True)
        idx = jnp.min(jnp.where(s == mx, lane_f, float(LANE)), axis=-1, keepdims=True)
        return mx, idx.astype(jnp.int32)

    best = None
    for g in range(N_GROUPS):
        in_g = (lane >= g * EXPERTS_PER_GROUP) & (lane < (g + 1) * EXPERTS_PER_GROUP)
        s = jnp.where(in_g, sel, neg)
        m1, i1 = top1(s)
        m2, _ = top1(jnp.where(lane == i1, neg, s))
        score = m1 + m2
        if best is None:
            best, gidx = score, jnp.zeros_like(i1)
        else:
            better = score > best
            gidx = jnp.where(better, g, gidx)
            best = jnp.where(better, score, best)
    lo = gidx * EXPERTS_PER_GROUP
    s = jnp.where((lane >= lo) & (lane < lo + EXPERTS_PER_GROUP), sel, neg)
    _, i1 = top1(s)
    _, i2 = top1(jnp.where(lane == i1, neg, s))
    picked = jnp.where((lane == i1) | (lane == i2), scores, 0.0)
    comb_out[...] = picked / jnp.sum(picked, axis=-1, keepdims=True)


def _moe_kernel(cnt_ref, h2_ref, comb_ref, xn_ref, *rest, last):
    mod_refs = rest[:MOE_SUB]
    w1_ref, w3_ref, w2_ref, tri_ref, fg_ref, out_ref, rank_t_ref, comb_t_ref = rest[MOE_SUB:]
    i = pl.program_id(0)
    e = pl.program_id(1)
    lane = lax.broadcasted_iota(jnp.int32, (1, LANE), 1)

    @pl.when(e == 0)
    def _():
        comb = comb_ref[...]
        picked = jnp.where(comb > 0.0, 1.0, 0.0).astype(BF16)
        rank = _dot(tri_ref[...], picked)
        rank_t_ref[...] = rank.T
        comb_t_ref[...] = comb.T
        out_ref[...] = jnp.zeros_like(out_ref)

    comb_e = jnp.sum(jnp.where(lane == e, comb_ref[...], 0.0), axis=-1, keepdims=True)
    rank_row = rank_t_ref[pl.ds(e, 1), :]
    picked_row = comb_t_ref[pl.ds(e, 1), :] > 0.0
    slot_col = lax.broadcasted_iota(jnp.int32, (MOE_CAP, 1), 0)
    n_blocks = lax.shift_right_logical(cnt_ref[i, e] + (MOE_CAP - 1), MOE_CAP.bit_length() - 1)

    def block(blk, carry):
        base = blk * MOE_CAP
        gather = jnp.where(picked_row & (rank_row == (slot_col + base).astype(F32)), 1.0, 0.0).astype(BF16)
        xg = _dot(gather, h2_ref[...]).astype(BF16)
        a = _dot(xg, w1_ref[0, 0].astype(BF16))
        he = (a * _sigmoid(a) * _dot(xg, w3_ref[0, 0].astype(BF16))).astype(BF16)
        y = _dot(he, w2_ref[0, 0].astype(BF16)).astype(BF16)
        out_ref[...] += comb_e * _dot_tn(gather, y)
        return carry

    lax.fori_loop(0, n_blocks, block, 0)

    @pl.when(e == N_EXPERTS - 1)
    def _():
        for s, m_ref in enumerate(mod_refs):
            rows = slice(s * SEG_ROWS, (s + 1) * SEG_ROWS)
            y = xn_ref[rows] + m_ref[0][5:6] * out_ref[rows]
            if last:
                y = _rms(y) * fg_ref[...]
            out_ref[rows] = y


def _head_pad_cols(w, width):
    lead = w.shape[:-1]
    w = w.reshape(lead + (C_HEADS, width))
    w = jnp.pad(w, [(0, 0)] * len(lead) + [(0, 0), (0, LANE - width)])
    return w.reshape(lead + (HP,))


def _head_pad_rows(w, width):
    n = w.shape[-1]
    w = w.reshape(C_HEADS, width, n)
    w = jnp.pad(w, ((0, 0), (0, LANE - width), (0, 0)))
    return w.reshape(HP, n)


def _split_rope(w):
    lead = w.shape[:-1]
    w = w.reshape(lead + (2, 2, ROPE_FREQS))
    x1 = w[..., 0, :].reshape(lead + (2 * ROPE_FREQS,))
    x2 = w[..., 1, :].reshape(lead + (2 * ROPE_FREQS,))
    return x1, x2


def _rope_tables(n_ctx, t):
    rows = t // GRID_W
    row = jnp.repeat(jnp.arange(rows, dtype=F32), GRID_W)
    col = jnp.tile(jnp.arange(GRID_W, dtype=F32), rows)
    inv = jnp.power(ROPE_BASE, -jnp.arange(ROPE_FREQS, dtype=F32) / ROPE_FREQS)
    ang = jnp.concatenate([row[:, None] * inv, col[:, None] * inv], axis=1)
    cos = jnp.concatenate([jnp.cos(ang), jnp.ones((n_ctx, 2 * ROPE_FREQS), F32)], axis=0)
    sin = jnp.concatenate([jnp.sin(ang), jnp.zeros((n_ctx, 2 * ROPE_FREQS), F32)], axis=0)
    n = n_ctx + t
    one = jnp.ones((n, B_NOPE), F32)
    zero = lambda w: jnp.zeros((n, w), F32)
    qs = MLA_SCALE * math.log2(math.e)
    cq = jnp.concatenate([one, cos, cos, zero(LANE - B_NOPE - B_ROPE)], axis=1) * qs
    sq = jnp.concatenate([zero(B_NOPE), -sin, sin, zero(LANE - B_NOPE - B_ROPE)], axis=1) * qs
    ck = jnp.concatenate([cos, cos, zero(LANE - B_ROPE)], axis=1)
    sk = jnp.concatenate([-sin, sin, zero(LANE - B_ROPE)], axis=1)
    return cq, sq, ck, sk


def _layer_weights(p, l):
    w = {}
    w_in = p['w_in'][l]
    d = w_in.shape[0]
    o = 0
    w_q = w_in[:, o:o + B_Q_LORA]; o += B_Q_LORA
    w_kv = w_in[:, o:o + B_KV_LORA]; o += B_KV_LORA
    w_kr = w_in[:, o:o + B_ROPE]; o += B_ROPE
    w_c = w_in[:, o:o + C_COLS]; o += C_COLS
    w_a = w_in[:, o:o + A_COLS]; o += A_COLS
    w_g = w_in[:, o:o + G_COLS]

    k1, k2 = _split_rope(w_kr)
    zpad = jnp.zeros((d, LANE - B_ROPE), F32)
    w['win_b'] = jnp.concatenate([w_q, w_kv, k1, k2, zpad, k2, k1, zpad], axis=1).astype(BF16)

    wuq = p['b_w_uq'][l].reshape(B_Q_LORA, B_HEADS, B_NOPE + B_ROPE)
    q1, q2 = _split_rope(wuq[..., B_NOPE:])
    qz = jnp.zeros((B_Q_LORA, B_HEADS, LANE - B_NOPE - B_ROPE), F32)
    w['wq'] = jnp.concatenate([wuq[..., :B_NOPE], q1, q2, qz], axis=-1).reshape(B_Q_LORA, HP).astype(BF16)
    w['wqs'] = jnp.concatenate([wuq[..., :B_NOPE], q2, q1, qz], axis=-1).reshape(B_Q_LORA, HP).astype(BF16)
    wukv = p['b_w_ukv'][l].reshape(B_KV_LORA, B_HEADS, B_NOPE + B_VDIM)
    w['wk'] = _head_pad_cols(wukv[..., :B_NOPE].reshape(B_KV_LORA, -1), B_NOPE).astype(BF16)
    w['wv'] = wukv[..., B_NOPE:].reshape(B_KV_LORA, -1).T.astype(BF16)
    place = jnp.zeros((LANE, B_HEADS, LANE), F32)
    idx = jnp.arange(B_ROPE)
    place = place.at[idx, :, B_NOPE + idx].set(1.0)
    w['place'] = place.reshape(LANE, HP).astype(BF16)
    w['qg'] = p['b_q_norm'][l][None]
    w['kvg'] = p['b_kv_norm'][l][None]

    w['wa'] = w_a.astype(BF16)
    w['wg'] = w_g.astype(BF16)
    w['vg'] = p['a_v_gain'][l][None]
    w['ws'] = p['a_ws'][l].astype(BF16)
    w['bsb'] = jnp.broadcast_to(p['a_bs'][l].T[:, :, None], (A_CHUNK, A_GROUPS, LANE)).reshape(A_CHUNK, A_WIDTH)

    cw = C_WIDTH
    pieces = [_head_pad_cols(w_c[:, i * cw:(i + 1) * cw], C_HEAD) for i in range(3)]
    w['wc'] = jnp.concatenate(pieces + [w_c[:, 3 * cw:]], axis=1).astype(BF16)
    mu = p['c_mu'][l]
    w['mu'] = jnp.concatenate([_head_pad_cols(mu[i * cw:(i + 1) * cw], C_HEAD) for i in range(3)]
                              + [mu[3 * cw:]])[None]

    def lora_block(m, rank):
        m = _head_pad_cols(m, C_HEAD)
        z = jnp.zeros_like(m[0])
        return jnp.stack([jnp.concatenate([m[0], z], axis=0), jnp.concatenate([z, m[1]], axis=0)]).astype(BF16)

    w['w2'] = lora_block(p['c_w2'][l], C_DECAY_LORA)
    w['a2'] = lora_block(p['c_a2'][l], C_AAA_LORA)
    w['w0'] = _head_pad_cols(p['c_w0'][l], C_HEAD)
    w['a0'] = _head_pad_cols(p['c_a0'][l], C_HEAD)
    w['kk'] = _head_pad_cols(p['c_k_k'][l], C_HEAD)[None]
    w['ka'] = _head_pad_cols(p['c_k_a'][l], C_HEAD)[None]
    w['rk'] = _head_pad_cols(p['c_r_k'][l].reshape(-1), C_HEAD)[None]
    w['g2'] = _head_pad_cols(p['c_g2'][l], C_HEAD).astype(BF16)
    w['gng'] = _head_pad_cols(p['c_gn_g'][l], C_HEAD)[None]
    w['gnb'] = _head_pad_cols(p['c_gn_b'][l], C_HEAD)[None]

    w['wup_a'] = p['w_up_a'][l].astype(BF16)
    w['wup_b'] = p['w_up_b'][l].astype(BF16)
    w['wup_c'] = _head_pad_rows(p['w_up_c'][l], C_HEAD).astype(BF16)
    w['wo'] = p['w_o'][l].astype(BF16)
    w['g1'] = p['norm1_g'][l][None]
    w['n2'] = p['norm2_g'][l][None]
    return w


def _layer(xs, mods, w, moe_w, tabs, rw, rb, fg, *, layer, bn, t, n_ctx, last):
    m, d = xs.shape
    n_tiles = m // TM
    n_lat = bn * t
    n_lat_tiles = n_lat // TM
    tpl = t // TM
    row = lambda i: (i, 0)
    seg_of = lambda blk: jnp.where(blk < n_lat // SEG_ROWS, 1 + blk // (t // SEG_ROWS), 0)
    seg = lambda i: (seg_of(i * (TM // SEG_ROWS)), 0, 0)
    x_spec = pl.BlockSpec((TM, d), row)
    mod_spec = pl.BlockSpec((1, 6, d), seg)
    hp_spec = pl.BlockSpec((TM, HP), row)
    hp2_spec = pl.BlockSpec((2, TM, HP), lambda i: (0, i, 0))
    hp_f32 = jax.ShapeDtypeStruct((m, HP), F32)
    hp_bf16 = jax.ShapeDtypeStruct((m, HP), BF16)
    hp2_bf16 = jax.ShapeDtypeStruct((2, m, HP), BF16)
    cq, sq, ck, sk = tabs
    tab_spec = pl.BlockSpec(
        (TM, LANE), lambda i: (jnp.where(i < n_lat_tiles, i % tpl, tpl + i - n_lat_tiles), 0))

    yb_w = B_HEADS * B_VDIM
    q, k, vt = pl.pallas_call(
        _mla_kernel,
        grid=(n_tiles,),
        in_specs=[x_spec, _full((1, d)), mod_spec, _full(w['win_b'].shape), _full((1, B_Q_LORA)),
                  _full((1, B_KV_LORA)), _full(w['wq'].shape), _full(w['wqs'].shape),
                  _full(w['wk'].shape), _full(w['wv'].shape), _full(w['place'].shape),
                  tab_spec, tab_spec, tab_spec, tab_spec],
        out_specs=[hp_spec, hp_spec, pl.BlockSpec((yb_w, TM), lambda i: (0, i))],
        out_shape=[hp_bf16, hp_bf16, jax.ShapeDtypeStruct((yb_w, m), BF16)],
        compiler_params=_params(1),
        name="mla_proj",
    )(xs, w['g1'], mods, w['win_b'], w['qg'], w['kvg'], w['wq'], w['wqs'], w['wk'], w['wv'],
      w['place'], cq, sq, ck, sk)

    ya, gate = pl.pallas_call(
        _gmlp_kernel,
        grid=(n_tiles,),
        in_specs=[x_spec, _full((1, d)), mod_spec, _full(w['wa'].shape), _full(w['wg'].shape),
                  _full((1, A_WIDTH)), _full(w['ws'].shape), _full(w['bsb'].shape)],
        out_specs=[pl.BlockSpec((TM, A_WIDTH), row), pl.BlockSpec((TM, G_COLS), row)],
        out_shape=[jax.ShapeDtypeStruct((m, A_WIDTH), BF16), jax.ShapeDtypeStruct((m, G_COLS), BF16)],
        compiler_params=_params(1),
        name="gmlp_gates",
    )(xs, w['g1'], mods, w['wa'], w['wg'], w['vg'], w['ws'], w['bsb'])

    tf = TM_FEAT
    sub = tf // 8
    seq_starts = tuple(b * t for b in range(bn)) + tuple(n_lat + b * n_ctx for b in range(bn))
    seq_ends = tuple(s - 1 for s in seq_starts[1:]) + (m - 1,)
    assert all(s % tf == 0 for s in seq_starts)
    f_hp = pl.BlockSpec((tf, HP), row)
    f_hp2 = pl.BlockSpec((2, tf, HP), lambda i: (0, i, 0))
    pe_spec = pl.BlockSpec((2, 8 * (tf // SCAN_C), HP), lambda i: (0, i, 0))
    pe_shape = jax.ShapeDtypeStruct((2, 8 * (m // SCAN_C), HP), F32)
    vv, ab, bh, kh, rbar, pe, bonus, gg = pl.pallas_call(
        functools.partial(_feat_kernel, seq_starts=seq_starts, seq_ends=seq_ends),
        grid=(m // tf,),
        in_specs=[pl.BlockSpec((tf, d), row),
                  pl.BlockSpec((8, d), lambda i: (jnp.maximum(i * sub - 1, 0), 0)),
                  pl.BlockSpec((8, d), lambda i: (jnp.minimum((i + 1) * sub, m // 8 - 1), 0)),
                  _full((1, d)), pl.BlockSpec((1, 6, d), lambda i: (seg_of(i * (tf // SEG_ROWS)), 0, 0)),
                  _full(w['wc'].shape), _full(w['mu'].shape),
                  _full((2, HP)), _full(w['w2'].shape), _full((2, HP)), _full(w['a2'].shape),
                  _full((1, HP)), _full((1, HP)), _full((1, HP)), _full(w['g2'].shape)],
        out_specs=[f_hp, f_hp2, f_hp2, f_hp2, f_hp2, pe_spec, f_hp, f_hp],
        out_shape=[hp_bf16, hp2_bf16, hp2_bf16, hp2_bf16, hp2_bf16, pe_shape, hp_f32, hp_f32],
        compiler_params=_params(1),
        name="rwkv_features",
    )(xs, xs, xs, w['g1'], mods, w['wc'], w['mu'], w['w0'], w['w2'], w['a0'], w['a2'],
      w['kk'], w['ka'], w['rk'], w['g2'])

    blk_rows = SCAN_C * SCAN_STEP
    nctx_c = n_ctx // blk_rows
    nlat_c = t // blk_rows
    nc = nctx_c + nlat_c

    def chunk_of(dd, b, j):
        ctx_base = n_lat // blk_rows + b * nctx_c
        lat_base = b * nlat_c
        if dd == 0:
            return jnp.where(j < nctx_c, ctx_base + j, lat_base + j - nctx_c)
        return jnp.where(j < nctx_c, ctx_base + nctx_c - 1 - j, lat_base + nc - 1 - j)

    def tok_spec(dd):
        return pl.BlockSpec((blk_rows, HP), lambda b, j: (chunk_of(dd, b, j), 0))

    def dir_spec(dd, rows=blk_rows):
        return pl.BlockSpec((1, rows, HP), lambda b, j: (dd, chunk_of(dd, b, j), 0))

    o_f, o_b = pl.pallas_call(
        _scan_kernel,
        grid=(bn, nc),
        in_specs=[tok_spec(0), tok_spec(1)] + [dir_spec(dd) for _ in range(4) for dd in range(2)]
                 + [dir_spec(0, 8 * SCAN_STEP), dir_spec(1, 8 * SCAN_STEP)],
        out_specs=[tok_spec(0), tok_spec(1)],
        out_shape=[hp_f32, hp_f32],
        scratch_shapes=[pltpu.VMEM((2, C_HEADS, LANE, LANE), F32)],
        compiler_params=_params(2),
        name="rwkv_scan",
    )(vv, vv, ab, ab, bh, bh, kh, kh, rbar, rbar, pe, pe)

    ctx_blk = n_lat // n_ctx
    tq = ATTN_TQ
    ctx_k =pl.BlockSpec((n_ctx, LANE), lambda b, hh, qi: (ctx_blk + b, hh))
    ctx_vt = pl.BlockSpec((B_VDIM, n_ctx), lambda b, hh, qi: (hh, ctx_blk + b))
    lat_k = pl.BlockSpec((t, LANE), lambda b, hh, qi: (b, hh))
    lat_vt = pl.BlockSpec((B_VDIM, t), lambda b, hh, qi: (hh, b))
    q_lat = pl.BlockSpec((tq, LANE), lambda b, hh, qi: (b * (t // tq) + qi, hh))
    yb_lat_t = pl.pallas_call(
        functools.partial(_attn_kernel, k_chunk=ATTN_KC),
        grid=(bn, B_HEADS, t // tq),
        in_specs=[q_lat, ctx_k, ctx_vt, lat_k, lat_vt],
        out_specs=pl.BlockSpec((B_VDIM, tq), lambda b, hh, qi: (hh, b * (t // tq) + qi)),
        out_shape=jax.ShapeDtypeStruct((B_HEADS * B_VDIM, n_lat), BF16),
        compiler_params=_params(3),
        name="mla_attention",
    )(q, k, vt, k, vt)
    ctx_q = pl.BlockSpec((n_ctx, LANE), lambda b, hh: (ctx_blk + b, hh))
    yb_ctx_t = pl.pallas_call(
        _attn_ctx_kernel,
        grid=(bn, B_HEADS),
        in_specs=[ctx_q, ctx_q, pl.BlockSpec((B_VDIM, n_ctx), lambda b, hh: (hh, ctx_blk + b))],
        out_specs=pl.BlockSpec((B_VDIM, n_ctx), lambda b, hh: (hh, b)),
        out_shape=jax.ShapeDtypeStruct((yb_w, bn * n_ctx), BF16),
        compiler_params=_params(2),
        name="mla_attention_ctx",
    )(q, k, vt)

    n_ctx_tiles = n_tiles - n_lat_tiles
    xn, h2, comb = pl.pallas_call(
        functools.partial(_merge_kernel, n_lat_tiles=n_lat_tiles),
        grid=(n_tiles,),
        in_specs=[hp_spec, hp_spec, hp_spec, hp_spec, pl.BlockSpec((TM, A_WIDTH), row),
                  pl.BlockSpec((yb_w, TM), lambda i: (0, jnp.minimum(i, n_lat_tiles - 1))),
                  pl.BlockSpec((yb_w, TM), lambda i: (0, jnp.clip(i - n_lat_tiles, 0, n_ctx_tiles - 1))),
                  pl.BlockSpec((TM, G_COLS), row), x_spec, mod_spec, _full((1, HP)), _full((1, HP)),
                  _full(w['wup_a'].shape), _full(w['wup_b'].shape), _full(w['wup_c'].shape),
                  _full(w['wo'].shape), _full((1, d)), _full(rw.shape), _full(rb.shape)],
        out_specs=[x_spec, x_spec, pl.BlockSpec((TM, LANE), row)],
        out_shape=[jax.ShapeDtypeStruct((m, d), F32), jax.ShapeDtypeStruct((m, d), BF16),
                   jax.ShapeDtypeStruct((m, LANE), F32)],
        compiler_params=_params(1),
        name="merge_router",
    )(o_f, o_b, bonus, gg, ya, yb_lat_t, yb_ctx_t, gate, xs, mods, w['gng'], w['gnb'], w['wup_a'], w['wup_b'],
      w['wup_c'], w['wo'], w['n2'], rw, rb)

    tm_moe = MOE_SUB * SEG_ROWS
    counts = jnp.sum((comb > 0.0).reshape(m // tm_moe, tm_moe, LANE), axis=1, dtype=jnp.int32)
    tri = jnp.tril(jnp.ones((tm_moe, tm_moe), BF16), -1)
    mrow = lambda i, e, cnt: (i, 0)
    once = pl.Buffered(1)
    mseg = lambda s: pl.BlockSpec((1, 6, d), lambda i, e, cnt: (seg_of(i * MOE_SUB + s), 0, 0))
    wexp = lambda shape: pl.BlockSpec((1, 1) + shape, lambda i, e, cnt: (layer, e, 0, 0))
    grid_spec = pltpu.PrefetchScalarGridSpec(
        num_scalar_prefetch=1,
        grid=(m // tm_moe, N_EXPERTS),
        in_specs=[pl.BlockSpec((tm_moe, d), mrow), pl.BlockSpec((tm_moe, LANE), mrow),
                  pl.BlockSpec((tm_moe, d), mrow, pipeline_mode=once)]
                 + [mseg(s) for s in range(MOE_SUB)]
                 + [wexp((d, D_EXPERT)), wexp((d, D_EXPERT)), wexp((D_EXPERT, d)),
                    pl.BlockSpec((tm_moe, tm_moe), lambda i, e, cnt: (0, 0), pipeline_mode=once),
                    pl.BlockSpec((1, d), lambda i, e, cnt: (0, 0))],
        out_specs=pl.BlockSpec((tm_moe, d), mrow),
        scratch_shapes=[pltpu.VMEM((LANE, tm_moe), F32),
                        pltpu.VMEM((LANE, tm_moe), F32)],
    )
    return pl.pallas_call(
        functools.partial(_moe_kernel, last=last),
        grid_spec=grid_spec,
        out_shape=jax.ShapeDtypeStruct((n_lat if last else m, d), F32),
        compiler_params=_params(2),
        name="moe",
    )(counts, h2, comb, xn, *([mods] * MOE_SUB), moe_w[0], moe_w[1], moe_w[2], tri, fg)


def kernel(x, c, ctx, c_ctx, mod_w, mod_b, norm1_g, norm2_g, w_in, a_v_gain, a_ws, a_bs, b_q_norm, b_w_uq, b_kv_norm, b_w_ukv, c_mu, c_w0, c_w2, c_a0, c_a2, c_g2, c_k_k, c_k_a, c_r_k, c_gn_g, c_gn_b, w_up_a, w_up_b, w_up_c, w_o, router_w, router_b, moe_w1, moe_w3, moe_w2, final_g):
    p = dict(w_in=w_in, a_v_gain=a_v_gain, a_ws=a_ws, a_bs=a_bs, b_q_norm=b_q_norm, b_w_uq=b_w_uq,
             b_kv_norm=b_kv_norm, b_w_ukv=b_w_ukv, c_mu=c_mu, c_w0=c_w0, c_w2=c_w2, c_a0=c_a0,
             c_a2=c_a2, c_g2=c_g2, c_k_k=c_k_k, c_k_a=c_k_a, c_r_k=c_r_k, c_gn_g=c_gn_g,
             c_gn_b=c_gn_b, w_up_a=w_up_a, w_up_b=w_up_b, w_up_c=w_up_c, w_o=w_o,
             norm1_g=norm1_g, norm2_g=norm2_g)
    bn, t, d = x.shape
    n_ctx = ctx.shape[1]
    depth = mod_w.shape[0]
    m = bn * (t + n_ctx)
    assert n_ctx == SEG_ROWS and (bn * n_ctx) % TM == 0 and t % TM == 0 and t % ATTN_TQ == 0
    assert t % ATTN_KC == 0 and m % (MOE_SUB * SEG_ROWS) == 0 and 1 + bn <= 8

    cvec = jnp.concatenate([c_ctx[None], c, jnp.zeros((8 - 1 - bn, d), F32)], axis=0)
    mods = _modulation(cvec, mod_w, mod_b).reshape(depth, 8, 6, d)

    tabs = _rope_tables(bn * n_ctx, t)
    rw = jnp.pad(router_w, ((0, 0), (0, LANE - N_EXPERTS))).astype(BF16)
    rb = jnp.pad(router_b, (0, LANE - N_EXPERTS))[None]
    fg = final_g[None]

    xs = jnp.concatenate([x.reshape(bn * t, d), ctx.reshape(bn * n_ctx, d)], axis=0)
    for l in range(depth):
        xs = _layer(xs, mods[l], _layer_weights(p, l), (moe_w1, moe_w3, moe_w2), tabs, rw, rb, fg,
                    layer=l, bn=bn, t=t, n_ctx=n_ctx, last=(l == depth - 1))
    return xs.reshape(bn, t, d)
```
